```python
import math
import jax
import jax.numpy as jnp
from jax import lax
import numpy as np

D_MODEL = 1024
BATCH = 8
SEQ = 8192
DEPTH = 4

GRID_W = 64
CTX_LEN = 256
N_MIXERS = 3
Q_BLOCK = 128
ROPE_BASE = 10000.0
NORM_EPS = 1e-6

DIFF_DH = 64
DIFF_HEADS = D_MODEL // (2 * DIFF_DH)
GQA_DH = 128
GQA_HEADS = D_MODEL // GQA_DH
GQA_KV_HEADS = max(GQA_HEADS // 4, 1)
GQA_GROUP = GQA_HEADS // GQA_KV_HEADS
GLA_HEADS = 4
GLA_DK = D_MODEL // 2
GLA_DV = D_MODEL
GLA_DK_H = GLA_DK // GLA_HEADS
GLA_DV_H = GLA_DV // GLA_HEADS
GLA_GATE_RANK = 16
GLA_TAU = 16.0
GLA_CHUNK = 64
FFN_HIDDEN = ((8 * D_MODEL // 3 + 255) // 256) * 256

N_A = len(range(0, DEPTH, N_MIXERS))
N_B = len(range(1, DEPTH, N_MIXERS))
N_C = len(range(2, DEPTH, N_MIXERS))

kernel_name = 'hybrid_diff_gqa_gla_prefix_dit'


def rmsnorm(x, g):
    x32 = x.astype(jnp.float32)
    y = x32 * lax.rsqrt(jnp.mean(x32 * x32, axis=-1, keepdims=True) + NORM_EPS)
    return (y * g.astype(jnp.float32)).astype(x.dtype)


def modulate(t, shift, scale):
    return t * (1 + scale) + shift


def swiglu(t, w_gu, w_down):
    gate, up = jnp.split(t @ w_gu, 2, axis=-1)
    return (jax.nn.silu(gate) * up) @ w_down


def axial_rope_tables(n_tokens, head_dim):
    rows = n_tokens // GRID_W
    row = jnp.broadcast_to(jnp.arange(rows)[:, None], (rows, GRID_W)).reshape(-1)
    col = jnp.broadcast_to(jnp.arange(GRID_W)[None, :], (rows, GRID_W)).reshape(-1)
    half = head_dim // 2
    inv = ROPE_BASE ** (-jnp.arange(0, half, 2, dtype=jnp.float32) / half)

    def axis_angles(pos):
        a = pos.astype(jnp.float32)[:, None] * inv[None, :]
        return jnp.concatenate([a, a], axis=-1)

    ang = jnp.concatenate([axis_angles(row), axis_angles(col)], axis=-1)
    return jnp.cos(ang), jnp.sin(ang)


def apply_axial_rope(x, cos, sin):
    q = x.shape[-1] // 4
    x32 = x.astype(jnp.float32)
    xr = x32.reshape(*x.shape[:-1], 2, 2, q)
    rot = jnp.stack([-xr[..., 1, :], xr[..., 0, :]], axis=-2).reshape(x.shape)
    return (x32 * cos + rot * sin).astype(x.dtype)


def sweep_query_blocks(fn, q):
    B, H, L, d = q.shape
    nb = L // Q_BLOCK
    qb = q.reshape(B, H, nb, Q_BLOCK, d).transpose(2, 0, 1, 3, 4)
    out = lax.map(fn, qb)
    return out.transpose(1, 2, 0, 3, 4).reshape(B, out.shape[2], L, out.shape[-1])


def diff_attention(h, hc, w_in, w_out, q_norm, k_norm, lq1, lk1, lq2, lk2, subln, lambda_init, ctx_out):
    f32 = jnp.float32
    L = h.shape[1]

    def project(t):
        Bt, Lt, _ = t.shape
        q, k, v = jnp.split(t @ w_in, 3, axis=-1)
        q = rmsnorm(q.reshape(Bt, Lt, 2 * DIFF_HEADS, DIFF_DH), q_norm).transpose(0, 2, 1, 3)
        k = rmsnorm(k.reshape(Bt, Lt, 2 * DIFF_HEADS, DIFF_DH), k_norm).transpose(0, 2, 1, 3)
        v = v.reshape(Bt, Lt, DIFF_HEADS, 2 * DIFF_DH).transpose(0, 2, 1, 3)
        return q, k, v

    q, k, v = project(h)
    qc, kc, vc = project(hc)
    cos, sin = axial_rope_tables(L, DIFF_DH)
    q = apply_axial_rope(q, cos, sin)
    k = apply_axial_rope(k, cos, sin)
    lam = (jnp.exp(jnp.sum(lq1.astype(f32) * lk1.astype(f32)))
           - jnp.exp(jnp.sum(lq2.astype(f32) * lk2.astype(f32))) + lambda_init)

    def pair(t):
        return t.reshape(t.shape[0], DIFF_HEADS, 2, t.shape[2], t.shape[3])

    def attend(qb, keys, vals):
        s = jnp.einsum('bhmtd,bhmsd->bhmts', pair(qb), keys).astype(f32) * DIFF_DH ** -0.5
        p = jax.nn.softmax(s, axis=-1)
        a = (p[:, :, 0] - lam * p[:, :, 1]).astype(vals.dtype)
        return jnp.einsum('bhts,bhse->bhte', a, vals)

    def finish(o):
        Bt, _, Lt, _ = o.shape
        o = rmsnorm(o, subln) * (1.0 - lambda_init)
        return o.transpose(0, 2, 1, 3).reshape(Bt, Lt, DIFF_HEADS * 2 * DIFF_DH) @ w_out

    keys = pair(jnp.concatenate([kc, k], axis=2))
    vals = jnp.concatenate([vc, v], axis=2)
    y = finish(sweep_query_blocks(lambda qb: attend(qb, keys, vals), q))
    yc = finish(attend(qc, pair(kc), vc)) if ctx_out else None
    return y, yc


def gqa_attention(h, hc, w_in, w_out, q_norm, k_norm, ctx_out):
    f32 = jnp.float32
    L = h.shape[1]
    kv_w = GQA_KV_HEADS * GQA_DH

    def project(t):
        Bt, Lt, _ = t.shape
        q, k, v = jnp.split(t @ w_in, [GQA_HEADS * GQA_DH, GQA_HEADS * GQA_DH + kv_w], axis=-1)
        q = rmsnorm(q.reshape(Bt, Lt, GQA_HEADS, GQA_DH), q_norm).transpose(0, 2, 1, 3)
        k = rmsnorm(k.reshape(Bt, Lt, GQA_KV_HEADS, GQA_DH), k_norm).transpose(0, 2, 1, 3)
        v = v.reshape(Bt, Lt, GQA_KV_HEADS, GQA_DH).transpose(0, 2, 1, 3)
        return q, k, v

    q, k, v = project(h)
    qc, kc, vc = project(hc)
    cos, sin = axial_rope_tables(L, GQA_DH)
    q = apply_axial_rope(q, cos, sin)
    k = apply_axial_rope(k, cos, sin)

    def attend(qb, keys, vals):
        Bq, _, T, _ = qb.shape
        qg = qb.reshape(Bq, GQA_KV_HEADS, GQA_GROUP, T, GQA_DH)
        s = jnp.einsum('bkgtd,bksd->bkgts', qg, keys).astype(f32) * GQA_DH ** -0.5
        p = jax.nn.softmax(s, axis=-1).astype(vals.dtype)
        o = jnp.einsum('bkgts,bksd->bkgtd', p, vals)
        return o.reshape(Bq, GQA_HEADS, T, GQA_DH)

    def finish(o):
        Bt, _, Lt, _ = o.shape
        return o.transpose(0, 2, 1, 3).reshape(Bt, Lt, GQA_HEADS * GQA_DH) @ w_out

    keys = jnp.concatenate([kc, k], axis=2)
    vals = jnp.concatenate([vc, v], axis=2)
    y = finish(sweep_query_blocks(lambda qb: attend(qb, keys, vals), q))
    yc = finish(attend(qc, kc, vc)) if ctx_out else None
    return y, yc


def gla_chunk_scan(q, k, v, g, s0):
    f32 = jnp.float32
    B, H, L, _ = q.shape
    n = L // GLA_CHUNK
    mask = jnp.tril(jnp.ones((GLA_CHUNK, GLA_CHUNK), dtype=bool))[:, :, None]

    def to_chunks(t):
        return t.reshape(B, H, n, GLA_CHUNK, t.shape[-1]).transpose(2, 0, 1, 3, 4)

    def step(S, inp):
        qc, kc, vc, gc = inp
        qf, kf, vf = qc.astype(f32), kc.astype(f32), vc.astype(f32)
        b = jnp.cumsum(gc.astype(f32), axis=-2)
        o_inter = jnp.einsum('bhtk,bhkv->bhtv', qf * jnp.exp(b), S)
        rel = jnp.where(mask, b[:, :, :, None, :] - b[:, :, None, :, :], -jnp.inf)
        A = jnp.einsum('bhtk,bhtsk,bhsk->bhts', qf, jnp.exp(rel), kf)
        o = o_inter + jnp.einsum('bhts,bhsv->bhtv', A, vf)
        b_last = b[:, :, -1, :]
        S = jnp.exp(b_last)[..., None] * S + jnp.einsum(
            'bhsk,bhsv->bhkv', kf * jnp.exp(b_last[:, :, None, :] - b), vf)
        return S, o.astype(v.dtype)

    S, o = lax.scan(step, s0, (to_chunks(q), to_chunks(k), to_chunks(v), to_chunks(g)))
    o = o.transpose(1, 2, 0, 3, 4).reshape(B, H, L, v.shape[-1])
    return o, S


def gla_mixer(h, hc, w_in, gw1_f, gw2_f, gb_f, gw1_b, gw2_b, gb_b, out_norm, w_out, ctx_out):
    B = h.shape[0]

    def project(t):
        Bt, Lt, _ = t.shape
        q, k, v, r = jnp.split(t @ w_in, [GLA_DK, 2 * GLA_DK, 2 * GLA_DK + GLA_DV], axis=-1)

        def heads(z, e):
            return z.reshape(Bt, Lt, GLA_HEADS, e).transpose(0, 2, 1, 3)

        def log_decay(w1, w2, b):
            return heads(jax.nn.log_sigmoid(((t @ w1) @ w2 + b).astype(jnp.float32)) / GLA_TAU, GLA_DK_H)

        return (heads(q, GLA_DK_H) * GLA_DK_H ** -0.5, heads(k, GLA_DK_H), heads(v, GLA_DV_H), r,
                log_decay(gw1_f, gw2_f, gb_f), log_decay(gw1_b, gw2_b, gb_b))

    def flip(z):
        return jnp.flip(z, axis=2)

    q, k, v, r, g_f, g_b = project(h)
    qc, kc, vc, rc, gc_f, gc_b = project(hc)
    zeros = jnp.zeros((B, GLA_HEADS, GLA_DK_H, GLA_DV_H), jnp.float32)
    oc_f, sc_f = gla_chunk_scan(qc, kc, vc, gc_f, zeros)
    oc_b, sc_b = gla_chunk_scan(flip(qc), flip(kc), flip(vc), flip(gc_b), zeros)
    o_f, _ = gla_chunk_scan(q, k, v, g_f, sc_f)
    o_b, _ = gla_chunk_scan(flip(q), flip(k), flip(v), flip(g_b), sc_b)

    def finish(o, gate_in):
        Bt, _, Lt, _ = o.shape
        o = rmsnorm(o, out_norm).transpose(0, 2, 1, 3).reshape(Bt, Lt, GLA_DV)
        return (o * jax.nn.silu(gate_in)) @ w_out

    y = finish(o_f + flip(o_b), r)
    yc = finish(oc_f + flip(oc_b), rc) if ctx_out else None
    return y, yc


def setup_inputs(seed: int = 0) -> dict:
    key = jax.random.key(seed)
    ks = jax.random.split(key, 32)
    f32 = jnp.float32
    D = D_MODEL

    def nrm(k, shape, scale):
        return scale * jax.random.normal(k, shape, f32)

    def gain(k, shape):
        return 1.0 + 0.02 * jax.random.normal(k, shape, f32)

    return {
        'x': nrm(ks[0], (BATCH, SEQ, D), 1.0),
        'c': nrm(ks[1], (BATCH, D), 1.0),
        'ctx': nrm(ks[2], (BATCH, CTX_LEN, D), 1.0),
        'c_ctx': nrm(ks[3], (D,), 1.0),
        'ada_w': nrm(ks[4], (DEPTH, D, 6 * D), 0.5 * D ** -0.5),
        'ada_b': nrm(ks[5], (DEPTH, 6 * D), 0.01),
        'norm1_g': gain(ks[6], (DEPTH, D)),
        'norm2_g': gain(ks[7], (DEPTH, D)),
        'ffn_w_gu': nrm(ks[8], (DEPTH, D, 2 * FFN_HIDDEN), D ** -0.5),
        'ffn_w_down': nrm(ks[9], (DEPTH, FFN_HIDDEN, D), FFN_HIDDEN ** -0.5),
        'diff_w_in': nrm(ks[10], (N_A, D, 3 * D), D ** -0.5),
        'diff_w_out': nrm(ks[11], (N_A, D, D), D ** -0.5),
        'diff_q_norm': gain(ks[12], (N_A, DIFF_DH)),
        'diff_k_norm': gain(ks[13], (N_A, DIFF_DH)),
        'diff_lambda_q1': nrm(ks[14], (N_A, DIFF_DH), 0.1),
        'diff_lambda_k1': nrm(ks[15], (N_A, DIFF_DH), 0.1),
        'diff_lambda_q2': nrm(ks[16], (N_A, DIFF_DH), 0.1),
        'diff_lambda_k2': nrm(ks[17], (N_A, DIFF_DH), 0.1),
        'diff_subln': gain(ks[18], (N_A, 2 * DIFF_DH)),
        'gqa_w_in': nrm(ks[19], (N_B, D, GQA_HEADS * GQA_DH + 2 * GQA_KV_HEADS * GQA_DH), D ** -0.5),
        'gqa_w_out': nrm(ks[20], (N_B, GQA_HEADS * GQA_DH, D), (GQA_HEADS * GQA_DH) ** -0.5),
        'gqa_q_norm': gain(ks[21], (N_B, GQA_DH)),
        'gqa_k_norm': gain(ks[22], (N_B, GQA_DH)),
        'gla_w_in': nrm(ks[23], (N_C, D, 2 * GLA_DK + 2 * GLA_DV), D ** -0.5),
        'gla_gate_w1_fwd': nrm(ks[24], (N_C, D, GLA_GATE_RANK), D ** -0.5),
        'gla_gate_w2_fwd': nrm(ks[25], (N_C, GLA_GATE_RANK, GLA_DK), GLA_GATE_RANK ** -0.5),
        'gla_gate_b_fwd': nrm(ks[26], (N_C, GLA_DK), 0.01),
        'gla_gate_w1_bwd': nrm(ks[27], (N_C, D, GLA_GATE_RANK), D ** -0.5),
        'gla_gate_w2_bwd': nrm(ks[28], (N_C, GLA_GATE_RANK, GLA_DK), GLA_GATE_RANK ** -0.5),
        'gla_gate_b_bwd': nrm(ks[29], (N_C, GLA_DK), 0.01),
        'gla_out_norm': gain(ks[30], (N_C, GLA_DV_H)),
        'gla_w_out': nrm(ks[31], (N_C, GLA_DV, D), GLA_DV ** -0.5),
    }


def reference(x, c, ctx, c_ctx, ada_w, ada_b, norm1_g, norm2_g, ffn_w_gu, ffn_w_down,
              diff_w_in, diff_w_out, diff_q_norm, diff_k_norm, diff_lambda_q1, diff_lambda_k1,
              diff_lambda_q2, diff_lambda_k2, diff_subln, gqa_w_in, gqa_w_out, gqa_q_norm, gqa_k_norm,
              gla_w_in, gla_gate_w1_fwd, gla_gate_w2_fwd, gla_gate_b_fwd, gla_gate_w1_bwd,
              gla_gate_w2_bwd, gla_gate_b_bwd, gla_out_norm, gla_w_out):
    xc = ctx
    cond_lat = jax.nn.silu(c)
    cond_ctx = jax.nn.silu(c_ctx)
    for i in range(DEPTH):
        last = i == DEPTH - 1
        mod = (cond_lat @ ada_w[i] + ada_b[i])[:, None, :]
        modc = (cond_ctx @ ada_w[i] + ada_b[i])[None, None, :]
        sh1, sc1, g1, sh2, sc2, g2 = jnp.split(mod, 6, axis=-1)
        csh1, csc1, cg1, csh2, csc2, cg2 = jnp.split(modc, 6, axis=-1)

        h = modulate(rmsnorm(x, norm1_g[i]), sh1, sc1)
        hc = modulate(rmsnorm(xc, norm1_g[i]), csh1, csc1)
        kind, j = i % N_MIXERS, i // N_MIXERS
        if kind == 0:
            lambda_init = 0.8 - 0.6 * math.exp(-0.3 * i)
            y, yc = diff_attention(h, hc, diff_w_in[j], diff_w_out[j], diff_q_norm[j], diff_k_norm[j],
                                   diff_lambda_q1[j], diff_lambda_k1[j], diff_lambda_q2[j],
                                   diff_lambda_k2[j], diff_subln[j], lambda_init, not last)
        elif kind == 1:
            y, yc = gqa_attention(h, hc, gqa_w_in[j], gqa_w_out[j], gqa_q_norm[j], gqa_k_norm[j], not last)
        else:
            y, yc = gla_mixer(h, hc, gla_w_in[j], gla_gate_w1_fwd[j], gla_gate_w2_fwd[j], gla_gate_b_fwd[j],
                              gla_gate_w1_bwd[j], gla_gate_w2_bwd[j], gla_gate_b_bwd[j],
                              gla_out_norm[j], gla_w_out[j], not last)

        x = x + g1 * y
        x = x + g2 * swiglu(modulate(rmsnorm(x, norm2_g[i]), sh2, sc2), ffn_w_gu[i], ffn_w_down[i])
        if not last:
            xc = xc + cg1 * yc
            xc = xc + cg2 * swiglu(modulate(rmsnorm(xc, norm2_g[i]), csh2, csc2), ffn_w_gu[i], ffn_w_down[i])
    return x
```

```python
import functools
import math

import jax
import jax.numpy as jnp
from jax import lax
from jax.experimental import pallas as pl
from jax.experimental.pallas import tpu as pltpu

F32 = jnp.float32
BF16 = jnp.bfloat16

GRID_W = 64
N_MIXERS = 3
ROPE_BASE = 10000.0
NORM_EPS = 1e-6
DIFF_DH = 64
GQA_DH = 128
GQA_GROUP = 4
GLA_HEADS = 4
GLA_GATE_RANK = 16
GLA_TAU = 16.0
GLA_CHUNK = 64
GLA_SUB = 16

LANES = 128
MXU_N = 256
VMEM_LIMIT = 56 * 1024 * 1024

TM_PRE = 512
TM_POST = 512
FFN_CHUNK = 256
TQ_DIFF = 256
TQ_GQA = 128
TK_ATTN = 1024
GLA_R = 4


def _const_spec(shape):
    nd = len(shape)
    return pl.BlockSpec(shape, lambda *_: (0,) * nd, pipeline_mode=pl.Buffered(1))


def _params(sem, vmem=VMEM_LIMIT):
    return pltpu.CompilerParams(dimension_semantics=sem, vmem_limit_bytes=vmem)


def _sigmoid(x):
    return 1.0 / (1.0 + jnp.exp(-x))


def _norm_mod(x, g, shift, scale):
    ms = jnp.mean(x * x, axis=-1, keepdims=True)
    return (x * lax.rsqrt(ms + NORM_EPS) * g) * (1.0 + scale) + shift


def _dot(a, b):
    return jnp.dot(a, b, preferred_element_type=F32)


def _dot_nt(a, b):
    return lax.dot_general(a, b, (((1,), (1,)), ((), ())), preferred_element_type=F32)


def _dot_tn(a, b):
    return lax.dot_general(a, b, (((0,), (0,)), ((), ())), preferred_element_type=F32)


def _split3(x):
    hi = x.astype(BF16)
    r1 = x - hi.astype(F32)
    mid = r1.astype(BF16)
    lo = (r1 - mid.astype(F32)).astype(BF16)
    return hi, mid, lo


def _mod_kernel(cond_ref, w_ref, b_ref, o_ref):
    c = cond_ref[...]
    s = (c * _sigmoid(c)).astype(BF16)
    o_ref[0] = _dot(s, w_ref[0].astype(BF16)) + b_ref[0]


def _modulation(cond, ada_w, ada_b):
    depth, d, n = ada_w.shape
    rows = cond.shape[0]
    tn = 1536
    return pl.pallas_call(
        _mod_kernel,
        grid=(depth, n // tn),
        in_specs=[
            pl.BlockSpec((rows, d), lambda i, j: (0, 0)),
            pl.BlockSpec((1, d, tn), lambda i, j: (i, 0, j)),
            pl.BlockSpec((1, 1, tn), lambda i, j: (i, 0, j)),
        ],
        out_specs=pl.BlockSpec((1, rows, tn), lambda i, j: (i, 0, j)),
        out_shape=jax.ShapeDtypeStruct((depth, rows, n), F32),
        compiler_params=_params(("parallel", "parallel")),
        name="adaln_mod",
    )(cond, ada_w, ada_b.reshape(depth, 1, n))


def _seg_mean(x2, p):
    hi = x2.astype(BF16)
    lo = (x2 - hi.astype(F32)).astype(BF16)
    return _dot(hi, p) + _dot(lo, p)


def _pre_attn_kernel(x_ref, mod_ref, g1_ref, w_ref, p_ref, gq_ref, gk_ref, cos_ref, sa_ref, sb_ref,
                     q_ref, k_ref, v_ref, *, d, nq, nk, nv, rot):
    x = x_ref[...]
    m = mod_ref[0]
    h = _norm_mod(x, g1_ref[...], m[:, 0:d], m[:, d:2 * d]).astype(BF16)
    p = p_ref[...]
    cos, sa, sb = cos_ref[...], sa_ref[...], sb_ref[...]

    def norm_rope(xs, gain):
        ms = _seg_mean(xs * xs, p)
        xn = xs * lax.rsqrt(ms + NORM_EPS) * gain
        return xn * cos + pltpu.roll(xn, LANES - rot, 1) * sa + pltpu.roll(xn, rot, 1) * sb

    per = MXU_N // LANES
    for grp in range((nq + nk + nv) // per):
        y = _dot(h, w_ref[:, grp * MXU_N:(grp + 1) * MXU_N])
        for t in range(per):
            slab = grp * per + t
            ys = y[:, t * LANES:(t + 1) * LANES]
            if slab < nq:
                q_ref[:, slab * LANES:(slab + 1) * LANES] = norm_rope(ys, gq_ref[...]).astype(BF16)
            elif slab < nq + nk:
                j = slab - nq
                k_ref[:, j * LANES:(j + 1) * LANES] = norm_rope(ys, gk_ref[...]).astype(BF16)
            else:
                j = slab - nq - nk
                v_ref[:, j * LANES:(j + 1) * LANES] = ys.astype(BF16)


def _pre_attn(x2d, modx, rows_per_mod, g1, w, pmat, gq, gk, cos, sa, sb, tab_rows, *, nq, nk, nv, rot):
    n, d = x2d.shape
    tm = min(TM_PRE, tab_rows)
    tab_blocks = tab_rows // tm
    mod_blocks = rows_per_mod // tm
    row = lambda c: pl.BlockSpec((tm, c), lambda i: (i, 0))
    tab = pl.BlockSpec((tm, LANES), lambda i: (i % tab_blocks, 0))
    kern = functools.partial(_pre_attn_kernel, d=d, nq=nq, nk=nk, nv=nv, rot=rot)
    return pl.pallas_call(
        kern,
        grid=(n // tm,),
        in_specs=[
            row(d),
            pl.BlockSpec((1, 1, 6 * d), lambda i: (i // mod_blocks, 0, 0)),
            _const_spec((1, d)),
            _const_spec(w.shape),
            _const_spec((LANES, LANES)),
            _const_spec((1, LANES)),
            _const_spec((1, LANES)),
            tab, tab, tab,
        ],
        out_specs=[row(nq * LANES), row(nk * LANES), row(nv * LANES)],
        out_shape=[jax.ShapeDtypeStruct((n, c * LANES), BF16) for c in (nq, nk, nv)],
        compiler_params=_params(("parallel",)),
        name="pre_attn",
    )(x2d, modx, g1, w, pmat, gq, gk, cos, sa, sb)


def _pre_gla_kernel(x_ref, mod_ref, g1_ref, w_ref, w2_ref, gb_ref,
                    q_ref, k_ref, v_ref, r_ref, g_ref, *, d, dk, dv, qscale):
    x = x_ref[...]
    m = mod_ref[0]
    h = _norm_mod(x, g1_ref[...], m[:, 0:d], m[:, d:2 * d]).astype(BF16)
    bounds = (dk, 2 * dk, 2 * dk + dv, 2 * dk + 2 * dv)
    ngrp = w_ref.shape[1] // MXU_N
    for grp in range(ngrp):
        c0 = grp * MXU_N
        y = _dot(h, w_ref[:, c0:c0 + MXU_N])
        if c0 < bounds[0]:
            q_ref[:, c0:c0 + MXU_N] = (y * qscale).astype(BF16)
        elif c0 < bounds[1]:
            k_ref[:, c0 - bounds[0]:c0 - bounds[0] + MXU_N] = y.astype(BF16)
        elif c0 < bounds[2]:
            v_ref[:, c0 - bounds[1]:c0 - bounds[1] + MXU_N] = y.astype(BF16)
        elif c0 < bounds[3]:
            r_ref[:, c0 - bounds[2]:c0 - bounds[2] + MXU_N] = y.astype(BF16)
        else:
            z = _dot(y.astype(BF16), w2_ref[...]) + gb_ref[...]
            logsig = jnp.minimum(z, 0.0) - jnp.log(1.0 + jnp.exp(-jnp.abs(z)))
            g_ref[...] = logsig * (1.0 / GLA_TAU)


def _pre_gla(x2d, modx, rows_per_mod, g1, w, w2, gb, *, dk, dv, qscale):
    n, d = x2d.shape
    tm = min(TM_PRE, n)
    mod_blocks = rows_per_mod // tm
    row = lambda c: pl.BlockSpec((tm, c), lambda i: (i, 0))
    kern = functools.partial(_pre_gla_kernel, d=d, dk=dk, dv=dv, qscale=qscale)
    return pl.pallas_call(
        kern,
        grid=(n // tm,),
        in_specs=[
            row(d),
            pl.BlockSpec((1, 1, 6 * d), lambda i: (i // mod_blocks, 0, 0)),
            _const_spec((1, d)),
            _const_spec(w.shape),
            _const_spec(w2.shape),
            _const_spec((1, 2 * dk)),
        ],
        out_specs=[row(dk), row(dk), row(dv), row(dv), row(2 * dk)],
        out_shape=[jax.ShapeDtypeStruct((n, c), t) for c, t in
                   ((dk, BF16), (dk, BF16), (dv, BF16), (dv, BF16), (2 * dk, F32))],
        compiler_params=_params(("parallel",)),
        name="pre_gla",
    )(x2d, modx, g1, w, w2, gb)


def _attn_kernel(*refs, diff, tq, tk, n_lat, lambda_init):
    if n_lat:
        q_ref, kc_ref, vc_ref, k_ref, v_ref, lam_ref, sub_ref, o_ref, m_sc, l_sc, acc_sc = refs
    else:
        q_ref, kc_ref, vc_ref, lam_ref, sub_ref, o_ref, m_sc, l_sc, acc_sc = refs
        k_ref = v_ref = None
    q = q_ref[0]
    if diff:
        lane = lax.broadcasted_iota(jnp.int32, (1, LANES), 1)
        lo_mask = jnp.where(lane < DIFF_DH, 1.0, 0.0).astype(BF16)
        qs = jnp.concatenate([q * lo_mask, q * (1.0 - lo_mask)], axis=0)
    else:
        qs = jnp.concatenate([q[:, g * LANES:(g + 1) * LANES] for g in range(GQA_GROUP)], axis=0)

    def step(kb, vb, first):
        s = _dot_nt(qs, kb)
        m_cur = jnp.max(s, axis=-1, keepdims=True)
        if first:
            m_new = m_cur
            p = jnp.exp(s - m_new)
            l_sc[...] = jnp.sum(p, axis=-1, keepdims=True)
            acc_sc[...] = _dot(p.astype(BF16), vb)
        else:
            m_old = m_sc[...]
            m_new = jnp.maximum(m_old, m_cur)
            alpha = jnp.exp(m_old - m_new)
            p = jnp.exp(s - m_new)
            l_sc[...] = alpha * l_sc[...] + jnp.sum(p, axis=-1, keepdims=True)
            acc_sc[...] = alpha * acc_sc[...] + _dot(p.astype(BF16), vb)
        m_sc[...] = m_new

    step(kc_ref[0], vc_ref[0], True)
    if n_lat:
        def body(i, carry):
            off = pl.multiple_of(i * tk, tk)
            step(k_ref[0, pl.ds(off, tk), :], v_ref[0, pl.ds(off, tk), :], False)
            return carry
        lax.fori_loop(0, n_lat, body, 0)

    o = acc_sc[...] / l_sc[...]
    if diff:
        lp = lam_ref[...]
        lam = (jnp.exp(jnp.sum(lp[0:1] * lp[1:2], axis=-1, keepdims=True))
               - jnp.exp(jnp.sum(lp[2:3] * lp[3:4], axis=-1, keepdims=True)) + lambda_init)
        a = o[:tq] - lam * o[tq:]
        ms = jnp.mean(a * a, axis=-1, keepdims=True)
        a = a * lax.rsqrt(ms + NORM_EPS) * sub_ref[...] * (1.0 - lambda_init)
        o_ref[0] = a.astype(BF16)
    else:
        for g in range(GQA_GROUP):
            o_ref[0, :, g * LANES:(g + 1) * LANES] = o[g * tq:(g + 1) * tq].astype(BF16)


def _attention(q, kc, vc, k, v, lamp, subln, *, diff, lambda_init):
    b, t, _ = q.shape
    lc = kc.shape[1]
    tq = min(TQ_DIFF if diff else TQ_GQA, t)
    qw = LANES if diff else GQA_GROUP * LANES
    heads = kc.shape[2] // LANES
    stack = (2 if diff else GQA_GROUP) * tq
    n_lat = 0
    ins = [q, kc, vc]
    specs = [
        pl.BlockSpec((1, tq, qw), lambda bi, h, i: (bi, i, h)),
        pl.BlockSpec((1, lc, LANES), lambda bi, h, i: (bi, 0, h)),
        pl.BlockSpec((1, lc, LANES), lambda bi, h, i: (bi, 0, h)),
    ]
    tk = TK_ATTN
    if k is not None:
        l = k.shape[1]
        tk = min(TK_ATTN, l)
        n_lat = l // tk
        ins += [k, v]
        specs += [pl.BlockSpec((1, l, LANES), lambda bi, h, i: (bi, 0, h))] * 2
    ins += [lamp, subln]
    specs += [pl.BlockSpec(lamp.shape, lambda bi, h, i: (0, 0)),
              pl.BlockSpec(subln.shape, lambda bi, h, i: (0, 0))]
    kern = functools.partial(_attn_kernel, diff=diff, tq=tq, tk=tk, n_lat=n_lat, lambda_init=lambda_init)
    return pl.pallas_call(
        kern,
        grid=(b, heads, t // tq),
        in_specs=specs,
        out_specs=pl.BlockSpec((1, tq, qw), lambda bi, h, i: (bi, i, h)),
        out_shape=jax.ShapeDtypeStruct((b, t, heads * qw), BF16),
        scratch_shapes=[pltpu.VMEM((stack, 1), F32), pltpu.VMEM((stack, 1), F32),
                        pltpu.VMEM((stack, LANES), F32)],
        compiler_params=_params(("parallel", "parallel", "arbitrary")),
        name="attn_diff" if diff else "attn_gqa",
    )(*ins)


def _gla_chunk(q, k, v, g, st_ref, rev):
    c = GLA_CHUNK
    row = lax.broadcasted_iota(jnp.int32, (c, c), 0)
    col = lax.broadcasted_iota(jnp.int32, (c, c), 1)
    keep = (col >= row) if rev else (col <= row)
    tri = jnp.where(keep, 1.0, 0.0).astype(BF16)
    hi, mid, lo = _split3(g)
    bsum = _dot(tri, hi) + _dot(tri, mid) + _dot(tri, lo)
    b_tot = bsum[0:1] if rev else bsum[c - 1:c]
    qf, kf = q.astype(F32), k.astype(F32)
    st = st_ref[...]
    o = _dot_nt((qf * jnp.exp(bsum)).astype(BF16), st.astype(BF16))

    krow = lax.broadcasted_iota(jnp.int32, bsum.shape, 0)
    nsub = c // GLA_SUB
    blocks = []
    for i in range(nsub):
        lo_r, hi_r = i * GLA_SUB, (i + 1) * GLA_SUB
        if rev:
            ref = bsum[hi_r:hi_r + 1] if i < nsub - 1 else jnp.zeros_like(b_tot)
            valid = krow >= lo_r
        else:
            ref = bsum[lo_r - 1:lo_r] if i > 0 else jnp.zeros_like(b_tot)
            valid = krow < hi_r
        ki = (kf * jnp.exp(jnp.where(valid, ref - bsum, 0.0))).astype(BF16)
        qi = (qf[lo_r:hi_r] * jnp.exp(bsum[lo_r:hi_r] - ref)).astype(BF16)
        blocks.append(_dot_nt(qi, ki))
    a = jnp.where(keep, jnp.concatenate(blocks, axis=0), 0.0)
    o = o + _dot(a.astype(BF16), v)

    kd = (kf * jnp.exp(b_tot - bsum)).astype(BF16)
    st_ref[...] = jnp.exp(b_tot) * st + _dot_tn(v, kd)
    return o


def _gla_kernel(qf_ref, kf_ref, vf_ref, gf_ref, qb_ref, kb_ref, vb_ref, gb_ref, s0_ref,
                of_ref, ob_ref, st_out_ref, st_sc, *, nchunk, dkh, dvh):
    j = pl.program_id(1)

    @pl.when(j == 0)
    def _():
        st_sc[...] = s0_ref[0]

    c = GLA_CHUNK
    for r in range(nchunk):
        rf, rb = r, nchunk - 1 - r
        for h in range(GLA_HEADS):
            kc, vc = slice(h * dkh, (h + 1) * dkh), slice(h * dvh, (h + 1) * dvh)
            rows = slice(rf * c, (rf + 1) * c)
            o = _gla_chunk(qf_ref[0, rows, kc], kf_ref[0, rows, kc], vf_ref[0, rows, vc], gf_ref[0, rows, kc],
                           st_sc.at[0, h], False)
            of_ref[0, rows, vc] = o.astype(BF16)
            rows = slice(rb * c, (rb + 1) * c)
            o = _gla_chunk(qb_ref[0, rows, kc], kb_ref[0, rows, kc], vb_ref[0, rows, vc], gb_ref[0, rows, kc],
                           st_sc.at[1, h], True)
            ob_ref[0, rows, vc] = o.astype(BF16)

    @pl.when(j == pl.num_programs(1) - 1)
    def _():
        st_out_ref[0] = st_sc[...]


def _gla_scan(q, k, v, g, s0):
    b, t, dk = q.shape
    dv = v.shape[2]
    dkh, dvh = dk // GLA_HEADS, dv // GLA_HEADS
    nchunk = min(GLA_R, t // GLA_CHUNK)
    rows = nchunk * GLA_CHUNK
    n = t // rows
    fwd = lambda w, cb=0: pl.BlockSpec((1, rows, w), lambda bi, j: (bi, j, cb))
    bwd = lambda w, cb=0: pl.BlockSpec((1, rows, w), lambda bi, j: (bi, n - 1 - j, cb))
    st_spec = pl.BlockSpec((1, 2, GLA_HEADS, dvh, dkh), lambda bi, j: (bi, 0, 0, 0, 0))
    kern = functools.partial(_gla_kernel, nchunk=nchunk, dkh=dkh, dvh=dvh)
    return pl.pallas_call(
        kern,
        grid=(b, n),
        in_specs=[fwd(dk), fwd(dk), fwd(dv), fwd(dk, 0), bwd(dk), bwd(dk), bwd(dv), bwd(dk, 1), st_spec],
        out_specs=[fwd(dv), bwd(dv), st_spec],
        out_shape=[jax.ShapeDtypeStruct((b, t, dv), BF16), jax.ShapeDtypeStruct((b, t, dv), BF16),
                   jax.ShapeDtypeStruct(s0.shape, F32)],
        scratch_shapes=[pltpu.VMEM((2, GLA_HEADS, dvh, dkh), F32)],
        compiler_params=_params(("parallel", "arbitrary")),
        name="gla_scan",
    )(q, k, v, g, q, k, v, g, s0)


def _post_kernel(*refs, d, f, gla, dvh):
    if gla:
        x_ref, of_ref, ob_ref, r_ref, on_ref, mod_ref, wo_ref, g2_ref, wgu_ref, wd_ref, o_ref = refs
        r = r_ref[...].astype(F32)
        gate = r * _sigmoid(r)
        parts = []
        for h in range(GLA_HEADS):
            cs = slice(h * dvh, (h + 1) * dvh)
            oh = of_ref[:, cs].astype(F32) + ob_ref[:, cs].astype(F32)
            ms = jnp.mean(oh * oh, axis=-1, keepdims=True)
            parts.append((oh * lax.rsqrt(ms + NORM_EPS) * on_ref[...] * gate[:, cs]).astype(BF16))
        a = jnp.concatenate(parts, axis=1)
    else:
        x_ref, a_ref, mod_ref, wo_ref, g2_ref, wgu_ref, wd_ref, o_ref = refs
        a = a_ref[...]
    m = mod_ref[0]
    gate1, shift2, scale2, gate2 = (m[:, 2 * d:3 * d], m[:, 3 * d:4 * d], m[:, 4 * d:5 * d], m[:, 5 * d:6 * d])
    x1 = x_ref[...] + gate1 * _dot(a, wo_ref[...])
    h2 = _norm_mod(x1, g2_ref[...], shift2, scale2).astype(BF16)
    acc = jnp.zeros(x1.shape, F32)
    for c0 in range(0, f, FFN_CHUNK):
        gt = _dot(h2, wgu_ref[:, c0:c0 + FFN_CHUNK])
        up = _dot(h2, wgu_ref[:, f + c0:f + c0 + FFN_CHUNK])
        act = (gt * _sigmoid(gt) * up).astype(BF16)
        acc = acc + _dot(act, wd_ref[c0:c0 + FFN_CHUNK, :])
    o_ref[...] = x1 + gate2 * acc


def _post(x2d, mix, modx, rows_per_mod, wo, g2, wgu, wd, *, gla_norm=None):
    n, d = x2d.shape
    f = wd.shape[0]
    tm = min(TM_POST, n)
    mod_blocks = rows_per_mod // tm
    row = lambda c: pl.BlockSpec((tm, c), lambda i: (i, 0))
    gla = gla_norm is not None
    if gla:
        ins = [x2d, *mix, gla_norm]
        specs = [row(d), row(d), row(d), row(d), _const_spec(gla_norm.shape)]
        dvh = gla_norm.shape[1]
    else:
        ins = [x2d, mix]
        specs = [row(d), row(d)]
        dvh = 0
    ins += [modx, wo, g2, wgu, wd]
    specs += [pl.BlockSpec((1, 1, 6 * d), lambda i: (i // mod_blocks, 0, 0)),
              _const_spec(wo.shape), _const_spec((1, d)), _const_spec(wgu.shape), _const_spec(wd.shape)]
    kern = functools.partial(_post_kernel, d=d, f=f, gla=gla, dvh=dvh)
    return pl.pallas_call(
        kern,
        grid=(n // tm,),
        in_specs=specs,
        out_specs=row(d),
        out_shape=jax.ShapeDtypeStruct((n, d), F32),
        compiler_params=_params(("parallel",)),
        name="post_gla" if gla else "post_attn",
    )(*ins)


def _rope_tables(n_tokens, head_dim):
    rows = n_tokens // GRID_W
    rowp = jnp.broadcast_to(jnp.arange(rows)[:, None], (rows, GRID_W)).reshape(-1)
    colp = jnp.broadcast_to(jnp.arange(GRID_W)[None, :], (rows, GRID_W)).reshape(-1)
    half = head_dim // 2
    quarter = head_dim // 4
    inv = ROPE_BASE ** (-jnp.arange(0, half, 2, dtype=F32) / half)

    def axis_angles(pos):
        a = pos.astype(F32)[:, None] * inv[None, :]
        return jnp.concatenate([a, a], axis=-1)

    ang = jnp.concatenate([axis_angles(rowp), axis_angles(colp)], axis=-1)
    reps = LANES // head_dim
    cos = jnp.tile(jnp.cos(ang), (1, reps))
    sin = jnp.tile(jnp.sin(ang), (1, reps))
    first = (jnp.arange(LANES) % (2 * quarter)) < quarter
    sa = jnp.where(first[None, :], -sin, 0.0)
    sb = jnp.where(first[None, :], 0.0, sin)
    return cos, sa, sb


def _seg_matrix(seg):
    idx = jnp.arange(LANES) // seg
    return jnp.where(idx[:, None] == idx[None, :], 1.0 / seg, 0.0).astype(BF16)


def kernel(x, c, ctx, c_ctx, ada_w, ada_b, norm1_g, norm2_g, ffn_w_gu, ffn_w_down, diff_w_in, diff_w_out, diff_q_norm, diff_k_norm, diff_lambda_q1, diff_lambda_k1, diff_lambda_q2, diff_lambda_k2, diff_subln, gqa_w_in, gqa_w_out, gqa_q_norm, gqa_k_norm, gla_w_in, gla_gate_w1_fwd, gla_gate_w2_fwd, gla_gate_b_fwd, gla_gate_w1_bwd, gla_gate_w2_bwd, gla_gate_b_bwd, gla_out_norm, gla_w_out):
    b, l, d = x.shape
    lc = ctx.shape[1]
    depth = ada_w.shape[0]
    nl, nc = b * l, b * lc

    cond_rows = ((b + 1 + 7) // 8) * 8
    cond = jnp.zeros((cond_rows, d), F32).at[:b].set(c).at[b].set(c_ctx)
    mod = _modulation(cond, ada_w, ada_b)

    xl = x.reshape(nl, d)
    xc = ctx.reshape(nc, d)
    ones_tab = (jnp.ones((lc, LANES), F32), jnp.zeros((lc, LANES), F32), jnp.zeros((lc, LANES), F32))
    dummy_lam = jnp.zeros((4, DIFF_DH), F32)
    dummy_sub = jnp.ones((1, LANES), F32)

    for i in range(depth):
        last = i == depth - 1
        kind, j = i % N_MIXERS, i // N_MIXERS
        modl = mod[i, :b].reshape(b, 1, 6 * d)
        modc = mod[i, b:b + 1].reshape(1, 1, 6 * d)
        g1 = norm1_g[i].reshape(1, d)
        g2 = norm2_g[i].reshape(1, d)
        wgu = ffn_w_gu[i].astype(BF16)
        wd = ffn_w_down[i].astype(BF16)

        if kind in (0, 1):
            if kind == 0:
                diff, dh = True, DIFF_DH
                w_in, w_out = diff_w_in[j].astype(BF16), diff_w_out[j].astype(BF16)
                gq = jnp.tile(diff_q_norm[j], LANES // dh).reshape(1, LANES) * dh ** -0.5
                gk = jnp.tile(diff_k_norm[j], LANES // dh).reshape(1, LANES)
                nq = nk = nv = d // LANES
                lamp = jnp.stack([diff_lambda_q1[j], diff_lambda_k1[j], diff_lambda_q2[j], diff_lambda_k2[j]])
                subln = diff_subln[j].reshape(1, LANES)
                lambda_init = 0.8 - 0.6 * math.exp(-0.3 * i)
            else:
                diff, dh = False, GQA_DH
                w_in, w_out = gqa_w_in[j].astype(BF16), gqa_w_out[j].astype(BF16)
                gq = gqa_q_norm[j].reshape(1, LANES) * dh ** -0.5
                gk = gqa_k_norm[j].reshape(1, LANES)
                nq = d // LANES
                nk = nv = (w_in.shape[1] - d) // (2 * LANES)
                lamp, subln, lambda_init = dummy_lam, dummy_sub, 0.0
            pmat = _seg_matrix(dh)
            cos, sa, sb = _rope_tables(l, dh)
            kw = dict(nq=nq, nk=nk, nv=nv, rot=dh // 4)
            ql, kl, vl = _pre_attn(xl, modl, l, g1, w_in, pmat, gq, gk, cos, sa, sb, l, **kw)
            qc, kc, vc = _pre_attn(xc, modc, nc, g1, w_in, pmat, gq, gk, *ones_tab, lc, **kw)
            r3 = lambda t, n: t.reshape(b, n, t.shape[1])
            kc3, vc3 = r3(kc, lc), r3(vc, lc)
            akw = dict(diff=diff, lambda_init=lambda_init)
            ol = _attention(r3(ql, l), kc3, vc3, r3(kl, l), r3(vl, l), lamp, subln, **akw).reshape(nl, d)
            xl = _post(xl, ol, modl, l, w_out, g2, wgu, wd)
            if not last:
                oc = _attention(r3(qc, lc), kc3, vc3, None, None, lamp, subln, **akw).reshape(nc, d)
                xc = _post(xc, oc, modc, nc, w_out, g2, wgu, wd)
        else:
            dk, dv = gla_gate_w2_fwd.shape[2], gla_w_out.shape[1]
            rank = gla_gate_w1_fwd.shape[2]
            w1 = jnp.zeros((d, MXU_N), F32).at[:, :rank].set(gla_gate_w1_fwd[j]).at[:, rank:2 * rank].set(gla_gate_w1_bwd[j])
            w_in = jnp.concatenate([gla_w_in[j], w1], axis=1).astype(BF16)
            w2 = (jnp.zeros((MXU_N, 2 * dk), F32).at[:rank, :dk].set(gla_gate_w2_fwd[j])
                  .at[rank:2 * rank, dk:].set(gla_gate_w2_bwd[j])).astype(BF16)
            gb = jnp.concatenate([gla_gate_b_fwd[j], gla_gate_b_bwd[j]]).reshape(1, 2 * dk)
            w_out = gla_w_out[j].astype(BF16)
            dkh, dvh = dk // GLA_HEADS, dv // GLA_HEADS
            onorm = gla_out_norm[j].reshape(1, dvh)
            kw = dict(dk=dk, dv=dv, qscale=dkh ** -0.5)
            ql, kl, vl, rl, gl = _pre_gla(xl, modl, l, g1, w_in, w2, gb, **kw)
            qc, kc, vc, rc, gc = _pre_gla(xc, modc, nc, g1, w_in, w2, gb, **kw)
            r3 = lambda t, n: t.reshape(b, n, t.shape[1])
            s0 = jnp.zeros((b, 2, GLA_HEADS, dvh, dkh), F32)
            ocf, ocb, sc = _gla_scan(r3(qc, lc), r3(kc, lc), r3(vc, lc), r3(gc, lc), s0)
            olf, olb, _ = _gla_scan(r3(ql, l), r3(kl, l), r3(vl, l), r3(gl, l), sc)
            xl = _post(xl, (olf.reshape(nl, dv), olb.reshape(nl, dv), rl), modl, l, w_out, g2, wgu, wd, gla_norm=onorm)
            if not last:
                xc = _post(xc, (ocf.reshape(nc, dv), ocb.reshape(nc, dv), rc), modc, nc, w_out, g2, wgu, wd, gla_norm=onorm)
    return xl.reshape(b, l, d)
```

```python
import functools
import math

import jax
import jax.numpy as jnp
from jax import lax
from jax.experimental import pallas as pl
from jax.experimental.pallas import tpu as pltpu

F32 = jnp.float32
BF16 = jnp.bfloat16

GRID_W = 64
N_MIXERS = 3
ROPE_BASE = 10000.0
NORM_EPS = 1e-6
DIFF_DH = 64
GQA_DH = 128
GQA_GROUP = 4
GLA_HEADS = 4
GLA_GATE_RANK = 16
GLA_TAU = 16.0
GLA_CHUNK = 64
GLA_SUB = 16
LOG2E = math.log2(math.e)
ATTN_MIN_DENOM = 1e-22

LANES = 128
MXU_N = 256
VMEM_LIMIT = 56 * 1024 * 1024

TM_PRE = 512
TM_POST = 512
FFN_CHUNK = 256
TQ_DIFF = 256
TQ_GQA = 128
TK_ATTN = 1024
GLA_R = 4


def _const_spec(shape):
    nd = len(shape)
    return pl.BlockSpec(shape, lambda *_: (0,) * nd, pipeline_mode=pl.Buffered(1))


def _params(sem, vmem=VMEM_LIMIT):
    return pltpu.CompilerParams(dimension_semantics=sem, vmem_limit_bytes=vmem)


def _sigmoid(x):
    return 1.0 / (1.0 + jnp.exp(-x))


def _norm_mod(x, g, shift, scale):
    ms = jnp.mean(x * x, axis=-1, keepdims=True)
    return (x * lax.rsqrt(ms + NORM_EPS) * g) * (1.0 + scale) + shift


def _dot(a, b):
    return jnp.dot(a, b, preferred_element_type=F32)


def _dot_nt(a, b):
    return lax.dot_general(a, b, (((1,), (1,)), ((), ())), preferred_element_type=F32)


def _dot_tn(a, b):
    return lax.dot_general(a, b, (((0,), (0,)), ((), ())), preferred_element_type=F32)


def _split3(x):
    hi = x.astype(BF16)
    r1 = x - hi.astype(F32)
    mid = r1.astype(BF16)
    lo = (r1 - mid.astype(F32)).astype(BF16)
    return hi, mid, lo


def _mod_kernel(cond_ref, w_ref, b_ref, o_ref):
    c = cond_ref[...]
    s = (c * _sigmoid(c)).astype(BF16)
    o_ref[0] = _dot(s, w_ref[0].astype(BF16)) + b_ref[0]


def _modulation(cond, ada_w, ada_b):
    depth, d, n = ada_w.shape
    rows = cond.shape[0]
    tn = 1536
    return pl.pallas_call(
        _mod_kernel,
        grid=(depth, n // tn),
        in_specs=[
            pl.BlockSpec((rows, d), lambda i, j: (0, 0)),
            pl.BlockSpec((1, d, tn), lambda i, j: (i, 0, j)),
            pl.BlockSpec((1, 1, tn), lambda i, j: (i, 0, j)),
        ],
        out_specs=pl.BlockSpec((1, rows, tn), lambda i, j: (i, 0, j)),
        out_shape=jax.ShapeDtypeStruct((depth, rows, n), F32),
        compiler_params=_params(("parallel", "parallel")),
        name="adaln_mod",
    )(cond, ada_w, ada_b.reshape(depth, 1, n))


def _seg_mean(x2, p):
    hi = x2.astype(BF16)
    lo = (x2 - hi.astype(F32)).astype(BF16)
    return _dot(hi, p) + _dot(lo, p)


def _pre_attn_kernel(x_ref, mod_ref, g1_ref, w_ref, p_ref, gq_ref, gk_ref, cos_ref, sa_ref, sb_ref,
                     q_ref, k_ref, v_ref, *, d, nq, nk, nv, rot):
    x = x_ref[...]
    m = mod_ref[0]
    h = _norm_mod(x, g1_ref[...], m[:, 0:d], m[:, d:2 * d]).astype(BF16)
    p = p_ref[...]
    cos, sa, sb = cos_ref[...], sa_ref[...], sb_ref[...]

    def norm_rope(xs, gain):
        ms = _seg_mean(xs * xs, p)
        xn = xs * lax.rsqrt(ms + NORM_EPS) * gain
        return xn * cos + pltpu.roll(xn, LANES - rot, 1) * sa + pltpu.roll(xn, rot, 1) * sb

    per = MXU_N // LANES
    for grp in range((nq + nk + nv) // per):
        y = _dot(h, w_ref[:, grp * MXU_N:(grp + 1) * MXU_N])
        for t in range(per):
            slab = grp * per + t
            ys = y[:, t * LANES:(t + 1) * LANES]
            if slab < nq:
                q_ref[:, slab * LANES:(slab + 1) * LANES] = norm_rope(ys, gq_ref[...]).astype(BF16)
            elif slab < nq + nk:
                j = slab - nq
                k_ref[:, j * LANES:(j + 1) * LANES] = norm_rope(ys, gk_ref[...]).astype(BF16)
            else:
                j = slab - nq - nk
                v_ref[:, j * LANES:(j + 1) * LANES] = ys.astype(BF16)


def _pre_attn(x2d, modx, rows_per_mod, g1, w, pmat, gq, gk, cos, sa, sb, tab_rows, *, nq, nk, nv, rot):
    n, d = x2d.shape
    tm = min(TM_PRE, tab_rows)
    tab_blocks = tab_rows // tm
    mod_blocks = rows_per_mod // tm
    row = lambda c: pl.BlockSpec((tm, c), lambda i: (i, 0))
    tab = pl.BlockSpec((tm, LANES), lambda i: (i % tab_blocks, 0))
    kern = functools.partial(_pre_attn_kernel, d=d, nq=nq, nk=nk, nv=nv, rot=rot)
    return pl.pallas_call(
        kern,
        grid=(n // tm,),
        in_specs=[
            row(d),
            pl.BlockSpec((1, 1, 6 * d), lambda i: (i // mod_blocks, 0, 0)),
            _const_spec((1, d)),
            _const_spec(w.shape),
            _const_spec((LANES, LANES)),
            _const_spec((1, LANES)),
            _const_spec((1, LANES)),
            tab, tab, tab,
        ],
        out_specs=[row(nq * LANES), row(nk * LANES), row(nv * LANES)],
        out_shape=[jax.ShapeDtypeStruct((n, c * LANES), BF16) for c in (nq, nk, nv)],
        compiler_params=_params(("parallel",)),
        name="pre_attn",
    )(x2d, modx, g1, w, pmat, gq, gk, cos, sa, sb)


def _pre_gla_kernel(x_ref, mod_ref, g1_ref, w_ref, w2_ref, gb_ref,
                    q_ref, k_ref, v_ref, r_ref, g_ref, *, d, dk, dv, qscale):
    x = x_ref[...]
    m = mod_ref[0]
    h = _norm_mod(x, g1_ref[...], m[:, 0:d], m[:, d:2 * d]).astype(BF16)
    bounds = (dk, 2 * dk, 2 * dk + dv, 2 * dk + 2 * dv)
    ngrp = w_ref.shape[1] // MXU_N
    for grp in range(ngrp):
        c0 = grp * MXU_N
        y = _dot(h, w_ref[:, c0:c0 + MXU_N])
        if c0 < bounds[0]:
            q_ref[:, c0:c0 + MXU_N] = (y * qscale).astype(BF16)
        elif c0 < bounds[1]:
            k_ref[:, c0 - bounds[0]:c0 - bounds[0] + MXU_N] = y.astype(BF16)
        elif c0 < bounds[2]:
            v_ref[:, c0 - bounds[1]:c0 - bounds[1] + MXU_N] = y.astype(BF16)
        elif c0 < bounds[3]:
            r_ref[:, c0 - bounds[2]:c0 - bounds[2] + MXU_N] = y.astype(BF16)
        else:
            z = _dot(y.astype(BF16), w2_ref[...]) + gb_ref[...]
            logsig = jnp.minimum(z, 0.0) - jnp.log(1.0 + jnp.exp(-jnp.abs(z)))
            g_ref[...] = logsig * (1.0 / GLA_TAU)


def _pre_gla(x2d, modx, rows_per_mod, g1, w, w2, gb, *, dk, dv, qscale):
    n, d = x2d.shape
    tm = min(TM_PRE, n)
    mod_blocks = rows_per_mod // tm
    row = lambda c: pl.BlockSpec((tm, c), lambda i: (i, 0))
    kern = functools.partial(_pre_gla_kernel, d=d, dk=dk, dv=dv, qscale=qscale)
    return pl.pallas_call(
        kern,
        grid=(n // tm,),
        in_specs=[
            row(d),
            pl.BlockSpec((1, 1, 6 * d), lambda i: (i // mod_blocks, 0, 0)),
            _const_spec((1, d)),
            _const_spec(w.shape),
            _const_spec(w2.shape),
            _const_spec((1, 2 * dk)),
        ],
        out_specs=[row(dk), row(dk), row(dv), row(dv), row(2 * dk)],
        out_shape=[jax.ShapeDtypeStruct((n, c), t) for c, t in
                   ((dk, BF16), (dk, BF16), (dv, BF16), (dv, BF16), (2 * dk, F32))],
        compiler_params=_params(("parallel",)),
        name="pre_gla",
    )(x2d, modx, g1, w, w2, gb)


def _attn_kernel(*refs, diff, tq, tk, n_lat, lambda_init):
    if n_lat:
        q_ref, kc_ref, vc_ref, k_ref, v_ref, pn_ref, lam_ref, sub_ref, o_ref, kn_sc, m_sc, l1_sc, l_sc, acc_sc = refs
    else:
        q_ref, kc_ref, vc_ref, pn_ref, lam_ref, sub_ref, o_ref, kn_sc, m_sc, l1_sc, l_sc, acc_sc = refs
        k_ref = v_ref = None
    lane = lax.broadcasted_iota(jnp.int32, (1, LANES), 1)
    lo_lane = lane < DIFF_DH

    @pl.when(pl.program_id(2) == 0)
    def _():
        pn = pn_ref[...]

        def seg_norm2_max(kb):
            kf = kb.astype(F32)
            return jnp.max(_dot((kf * kf).astype(BF16), pn), axis=0, keepdims=True)

        r = seg_norm2_max(kc_ref[0])
        if n_lat:
            def body(i, r):
                off = pl.multiple_of(i * tk, tk)
                return jnp.maximum(r, seg_norm2_max(k_ref[0, pl.ds(off, tk), :]))
            r = lax.fori_loop(0, n_lat, body, r)
        if diff:
            kn_sc[0:1, :] = jnp.broadcast_to(jnp.max(jnp.where(lo_lane, r, 0.0), axis=1, keepdims=True), (1, LANES))
            kn_sc[1:2, :] = jnp.broadcast_to(jnp.max(jnp.where(lo_lane, 0.0, r), axis=1, keepdims=True), (1, LANES))
        else:
            kn_sc[0:1, :] = r

    q = q_ref[0]
    if diff:
        lo_mask = jnp.where(lo_lane, 1.0, 0.0).astype(BF16)
        qs = jnp.concatenate([q * lo_mask, q * (1.0 - lo_mask)], axis=0)
        kn = jnp.concatenate([jnp.broadcast_to(kn_sc[0:1, :], (tq, LANES)),
                              jnp.broadcast_to(kn_sc[1:2, :], (tq, LANES))], axis=0)
    else:
        qs = jnp.concatenate([q[:, g * LANES:(g + 1) * LANES] for g in range(GQA_GROUP)], axis=0)
        kn = kn_sc[0:1, :]
    qf = qs.astype(F32)
    qn2 = _dot((qf * qf).astype(BF16), jnp.ones((LANES, LANES), BF16))
    shift = jnp.sqrt(qn2 * kn)

    def fast_step(kb, vb, first):
        s = _dot_nt(qs, kb)
        psum, ps = None, []
        for c in range(kb.shape[0] // LANES):
            pc = jnp.exp2(s[:, c * LANES:(c + 1) * LANES] - shift)
            psum = pc if psum is None else psum + pc
            ps.append(pc.astype(BF16))
        pv = _dot(jnp.concatenate(ps, axis=1), vb)
        if first:
            l_sc[...] = psum
            acc_sc[...] = pv
        else:
            l_sc[...] += psum
            acc_sc[...] += pv

    def exact_step(kb, vb, first):
        s = _dot_nt(qs, kb)
        m_cur = jnp.max(s, axis=-1, keepdims=True)
        if first:
            m_new = m_cur
            p = jnp.exp2(s - m_new)
            l1_sc[...] = jnp.sum(p, axis=-1, keepdims=True)
            acc_sc[...] = _dot(p.astype(BF16), vb)
        else:
            m_old = m_sc[...]
            m_new = jnp.maximum(m_old, m_cur)
            alpha = jnp.exp2(m_old - m_new)
            p = jnp.exp2(s - m_new)
            l1_sc[...] = alpha * l1_sc[...] + jnp.sum(p, axis=-1, keepdims=True)
            acc_sc[...] = alpha * acc_sc[...] + _dot(p.astype(BF16), vb)
        m_sc[...] = m_new

    def sweep(step):
        step(kc_ref[0], vc_ref[0], True)
        if n_lat:
            def body(i, carry):
                off = pl.multiple_of(i * tk, tk)
                step(k_ref[0, pl.ds(off, tk), :], v_ref[0, pl.ds(off, tk), :], False)
                return carry
            lax.fori_loop(0, n_lat, body, 0)

    def finish(o):
        if diff:
            lp = lam_ref[...]
            lam = (jnp.exp(jnp.sum(lp[0:1] * lp[1:2], axis=-1, keepdims=True))
                   - jnp.exp(jnp.sum(lp[2:3] * lp[3:4], axis=-1, keepdims=True)) + lambda_init)
            a = o[:tq] - lam * o[tq:]
            ms = jnp.mean(a * a, axis=-1, keepdims=True)
            a = a * lax.rsqrt(ms + NORM_EPS) * sub_ref[...] * (1.0 - lambda_init)
            o_ref[0] = a.astype(BF16)
        else:
            for g in range(GQA_GROUP):
                o_ref[0, :, g * LANES:(g + 1) * LANES] = o[g * tq:(g + 1) * tq].astype(BF16)

    sweep(fast_step)
    denom = jnp.sum(l_sc[...], axis=-1, keepdims=True)
    finish(acc_sc[...] / denom)

    @pl.when(jnp.logical_not(jnp.min(denom) >= ATTN_MIN_DENOM))
    def _():
        sweep(exact_step)
        finish(acc_sc[...] / l1_sc[...])


def _attention(q, kc, vc, k, v, lamp, subln, *, diff, lambda_init):
    b, t, _ = q.shape
    lc = kc.shape[1]
    tq = min(TQ_DIFF if diff else TQ_GQA, t)
    qw = LANES if diff else GQA_GROUP * LANES
    heads = kc.shape[2] // LANES
    stack = (2 if diff else GQA_GROUP) * tq
    n_lat = 0
    ins = [q, kc, vc]
    specs = [
        pl.BlockSpec((1, tq, qw), lambda bi, h, i: (bi, i, h)),
        pl.BlockSpec((1, lc, LANES), lambda bi, h, i: (bi, 0, h)),
        pl.BlockSpec((1, lc, LANES), lambda bi, h, i: (bi, 0, h)),
    ]
    tk = TK_ATTN
    if k is not None:
        l = k.shape[1]
        tk = min(TK_ATTN, l)
        n_lat = l // tk
        ins += [k, v]
        specs += [pl.BlockSpec((1, l, LANES), lambda bi, h, i: (bi, 0, h))] * 2
    seg = DIFF_DH if diff else LANES
    idx = jnp.arange(LANES) // seg
    pn = jnp.where(idx[:, None] == idx[None, :], 1.0, 0.0).astype(BF16)
    ins += [pn, lamp, subln]
    specs += [pl.BlockSpec(a.shape, lambda bi, h, i: (0, 0)) for a in (pn, lamp, subln)]
    kern = functools.partial(_attn_kernel, diff=diff, tq=tq, tk=tk, n_lat=n_lat, lambda_init=lambda_init)
    return pl.pallas_call(
        kern,
        grid=(b, heads, t // tq),
        in_specs=specs,
        out_specs=pl.BlockSpec((1, tq, qw), lambda bi, h, i: (bi, i, h)),
        out_shape=jax.ShapeDtypeStruct((b, t, heads * qw), BF16),
        scratch_shapes=[pltpu.VMEM((8, LANES), F32), pltpu.VMEM((stack, 1), F32), pltpu.VMEM((stack, 1), F32),
                        pltpu.VMEM((stack, LANES), F32), pltpu.VMEM((stack, LANES), F32)],
        compiler_params=_params(("parallel", "parallel", "arbitrary")),
        name="attn_diff" if diff else "attn_gqa",
    )(*ins)


def _gla_chunk(q, k, v, g, st_ref, rev):
    c = GLA_CHUNK
    row = lax.broadcasted_iota(jnp.int32, (c, c), 0)
    col = lax.broadcasted_iota(jnp.int32, (c, c), 1)
    keep = (col >= row) if rev else (col <= row)
    tri = jnp.where(keep, 1.0, 0.0).astype(BF16)
    hi, mid, lo = _split3(g)
    bsum = _dot(tri, hi) + _dot(tri, mid) + _dot(tri, lo)
    b_tot = bsum[0:1] if rev else bsum[c - 1:c]
    qf, kf = q.astype(F32), k.astype(F32)
    st = st_ref[...]
    o = _dot_nt((qf * jnp.exp(bsum)).astype(BF16), st.astype(BF16))

    krow = lax.broadcasted_iota(jnp.int32, bsum.shape, 0)
    nsub = c // GLA_SUB
    blocks = []
    for i in range(nsub):
        lo_r, hi_r = i * GLA_SUB, (i + 1) * GLA_SUB
        if rev:
            ref = bsum[hi_r:hi_r + 1] if i < nsub - 1 else jnp.zeros_like(b_tot)
            valid = krow >= lo_r
        else:
            ref = bsum[lo_r - 1:lo_r] if i > 0 else jnp.zeros_like(b_tot)
            valid = krow < hi_r
        ki = (kf * jnp.exp(jnp.where(valid, ref - bsum, 0.0))).astype(BF16)
        qi = (qf[lo_r:hi_r] * jnp.exp(bsum[lo_r:hi_r] - ref)).astype(BF16)
        blocks.append(_dot_nt(qi, ki))
    a = jnp.where(keep, jnp.concatenate(blocks, axis=0), 0.0)
    o = o + _dot(a.astype(BF16), v)

    kd = (kf * jnp.exp(b_tot - bsum)).astype(BF16)
    st_ref[...] = jnp.exp(b_tot) * st + _dot_tn(v, kd)
    return o


def _gla_kernel(qf_ref, kf_ref, vf_ref, gf_ref, qb_ref, kb_ref, vb_ref, gb_ref, s0_ref,
                of_ref, ob_ref, st_out_ref, st_sc, *, nchunk, dkh, dvh):
    j = pl.program_id(1)

    @pl.when(j == 0)
    def _():
        st_sc[...] = s0_ref[0]

    c = GLA_CHUNK
    for r in range(nchunk):
        rf, rb = r, nchunk - 1 - r
        for h in range(GLA_HEADS):
            kc, vc = slice(h * dkh, (h + 1) * dkh), slice(h * dvh, (h + 1) * dvh)
            rows = slice(rf * c, (rf + 1) * c)
            o = _gla_chunk(qf_ref[0, rows, kc], kf_ref[0, rows, kc], vf_ref[0, rows, vc], gf_ref[0, rows, kc],
                           st_sc.at[0, h], False)
            of_ref[0, rows, vc] = o.astype(BF16)
            rows = slice(rb * c, (rb + 1) * c)
            o = _gla_chunk(qb_ref[0, rows, kc], kb_ref[0, rows, kc], vb_ref[0, rows, vc], gb_ref[0, rows, kc],
                           st_sc.at[1, h], True)
            ob_ref[0, rows, vc] = o.astype(BF16)

    @pl.when(j == pl.num_programs(1) - 1)
    def _():
        st_out_ref[0] = st_sc[...]


def _gla_scan(q, k, v, g, s0):
    b, t, dk = q.shape
    dv = v.shape[2]
    dkh, dvh = dk // GLA_HEADS, dv // GLA_HEADS
    nchunk = min(GLA_R, t // GLA_CHUNK)
    rows = nchunk * GLA_CHUNK
    n = t // rows
    fwd = lambda w, cb=0: pl.BlockSpec((1, rows, w), lambda bi, j: (bi, j, cb))
    bwd = lambda w, cb=0: pl.BlockSpec((1, rows, w), lambda bi, j: (bi, n - 1 - j, cb))
    st_spec = pl.BlockSpec((1, 2, GLA_HEADS, dvh, dkh), lambda bi, j: (bi, 0, 0, 0, 0))
    kern = functools.partial(_gla_kernel, nchunk=nchunk, dkh=dkh, dvh=dvh)
    return pl.pallas_call(
        kern,
        grid=(b, n),
        in_specs=[fwd(dk), fwd(dk), fwd(dv), fwd(dk, 0), bwd(dk), bwd(dk), bwd(dv), bwd(dk, 1), st_spec],
        out_specs=[fwd(dv), bwd(dv), st_spec],
        out_shape=[jax.ShapeDtypeStruct((b, t, dv), BF16), jax.ShapeDtypeStruct((b, t, dv), BF16),
                   jax.ShapeDtypeStruct(s0.shape, F32)],
        scratch_shapes=[pltpu.VMEM((2, GLA_HEADS, dvh, dkh), F32)],
        compiler_params=_params(("parallel", "arbitrary")),
        name="gla_scan",
    )(q, k, v, g, q, k, v, g, s0)


def _post_kernel(*refs, d, f, gla, dvh):
    if gla:
        x_ref, of_ref, ob_ref, r_ref, on_ref, mod_ref, wo_ref, g2_ref, wgu_ref, wd_ref, o_ref = refs
        r = r_ref[...].astype(F32)
        gate = r * _sigmoid(r)
        parts = []
        for h in range(GLA_HEADS):
            cs = slice(h * dvh, (h + 1) * dvh)
            oh = of_ref[:, cs].astype(F32) + ob_ref[:, cs].astype(F32)
            ms = jnp.mean(oh * oh, axis=-1, keepdims=True)
            parts.append((oh * lax.rsqrt(ms + NORM_EPS) * on_ref[...] * gate[:, cs]).astype(BF16))
        a = jnp.concatenate(parts, axis=1)
    else:
        x_ref, a_ref, mod_ref, wo_ref, g2_ref, wgu_ref, wd_ref, o_ref = refs
        a = a_ref[...]
    m = mod_ref[0]
    gate1, shift2, scale2, gate2 = (m[:, 2 * d:3 * d], m[:, 3 * d:4 * d], m[:, 4 * d:5 * d], m[:, 5 * d:6 * d])
    x1 = x_ref[...] + gate1 * _dot(a, wo_ref[...])
    h2 = _norm_mod(x1, g2_ref[...], shift2, scale2).astype(BF16)
    acc = jnp.zeros(x1.shape, F32)
    for c0 in range(0, f, FFN_CHUNK):
        gt = _dot(h2, wgu_ref[:, c0:c0 + FFN_CHUNK])
        up = _dot(h2, wgu_ref[:, f + c0:f + c0 + FFN_CHUNK])
        act = (gt * _sigmoid(gt) * up).astype(BF16)
        acc = acc + _dot(act, wd_ref[c0:c0 + FFN_CHUNK, :])
    o_ref[...] = x1 + gate2 * acc


def _post(x2d, mix, modx, rows_per_mod, wo, g2, wgu, wd, *, gla_norm=None):
    n, d = x2d.shape
    f = wd.shape[0]
    tm = min(TM_POST, n)
    mod_blocks = rows_per_mod // tm
    row = lambda c: pl.BlockSpec((tm, c), lambda i: (i, 0))
    gla = gla_norm is not None
    if gla:
        ins = [x2d, *mix, gla_norm]
        specs = [row(d), row(d), row(d), row(d), _const_spec(gla_norm.shape)]
        dvh = gla_norm.shape[1]
    else:
        ins = [x2d, mix]
        specs = [row(d), row(d)]
        dvh = 0
    ins += [modx, wo, g2, wgu, wd]
    specs += [pl.BlockSpec((1, 1, 6 * d), lambda i: (i // mod_blocks, 0, 0)),
              _const_spec(wo.shape), _const_spec((1, d)), _const_spec(wgu.shape), _const_spec(wd.shape)]
    kern = functools.partial(_post_kernel, d=d, f=f, gla=gla, dvh=dvh)
    return pl.pallas_call(
        kern,
        grid=(n // tm,),
        in_specs=specs,
        out_specs=row(d),
        out_shape=jax.ShapeDtypeStruct((n, d), F32),
        compiler_params=_params(("parallel",)),
        name="post_gla" if gla else "post_attn",
    )(*ins)


def _rope_tables(n_tokens, head_dim):
    rows = n_tokens // GRID_W
    rowp = jnp.broadcast_to(jnp.arange(rows)[:, None], (rows, GRID_W)).reshape(-1)
    colp = jnp.broadcast_to(jnp.arange(GRID_W)[None, :], (rows, GRID_W)).reshape(-1)
    half = head_dim // 2
    quarter = head_dim // 4
    inv = ROPE_BASE ** (-jnp.arange(0, half, 2, dtype=F32) / half)

    def axis_angles(pos):
        a = pos.astype(F32)[:, None] * inv[None, :]
        return jnp.concatenate([a, a], axis=-1)

    ang = jnp.concatenate([axis_angles(rowp), axis_angles(colp)], axis=-1)
    reps = LANES // head_dim
    cos = jnp.tile(jnp.cos(ang), (1, reps))
    sin = jnp.tile(jnp.sin(ang), (1, reps))
    first = (jnp.arange(LANES) % (2 * quarter)) < quarter
    sa = jnp.where(first[None, :], -sin, 0.0)
    sb = jnp.where(first[None, :], 0.0, sin)
    return cos, sa, sb


def _seg_matrix(seg):
    idx = jnp.arange(LANES) // seg
    return jnp.where(idx[:, None] == idx[None, :], 1.0 / seg, 0.0).astype(BF16)


def kernel(x, c, ctx, c_ctx, ada_w, ada_b, norm1_g, norm2_g, ffn_w_gu, ffn_w_down, diff_w_in, diff_w_out, diff_q_norm, diff_k_norm, diff_lambda_q1, diff_lambda_k1, diff_lambda_q2, diff_lambda_k2, diff_subln, gqa_w_in, gqa_w_out, gqa_q_norm, gqa_k_norm, gla_w_in, gla_gate_w1_fwd, gla_gate_w2_fwd, gla_gate_b_fwd, gla_gate_w1_bwd, gla_gate_w2_bwd, gla_gate_b_bwd, gla_out_norm, gla_w_out):
    b, l, d = x.shape
    lc = ctx.shape[1]
    depth = ada_w.shape[0]
    nl, nc = b * l, b * lc

    cond_rows = ((b + 1 + 7) // 8) * 8
    cond = jnp.zeros((cond_rows, d), F32).at[:b].set(c).at[b].set(c_ctx)
    mod = _modulation(cond, ada_w, ada_b)

    xl = x.reshape(nl, d)
    xc = ctx.reshape(nc, d)
    ones_tab = (jnp.ones((lc, LANES), F32), jnp.zeros((lc, LANES), F32), jnp.zeros((lc, LANES), F32))
    dummy_lam = jnp.zeros((4, DIFF_DH), F32)
    dummy_sub = jnp.ones((1, LANES), F32)

    for i in range(depth):
        last = i == depth - 1
        kind, j = i % N_MIXERS, i // N_MIXERS
        modl = mod[i, :b].reshape(b, 1, 6 * d)
        modc = mod[i, b:b + 1].reshape(1, 1, 6 * d)
        g1 = norm1_g[i].reshape(1, d)
        g2 = norm2_g[i].reshape(1, d)
        wgu = ffn_w_gu[i].astype(BF16)
        wd = ffn_w_down[i].astype(BF16)

        if kind in (0, 1):
            if kind == 0:
                diff, dh = True, DIFF_DH
                w_in, w_out = diff_w_in[j].astype(BF16), diff_w_out[j].astype(BF16)
                gq = jnp.tile(diff_q_norm[j], LANES // dh).reshape(1, LANES) * (dh ** -0.5 * LOG2E)
                gk = jnp.tile(diff_k_norm[j], LANES // dh).reshape(1, LANES)
                nq = nk = nv = d // LANES
                lamp = jnp.stack([diff_lambda_q1[j], diff_lambda_k1[j], diff_lambda_q2[j], diff_lambda_k2[j]])
                subln = diff_subln[j].reshape(1, LANES)
                lambda_init = 0.8 - 0.6 * math.exp(-0.3 * i)
            else:
                diff, dh = False, GQA_DH
                w_in, w_out = gqa_w_in[j].astype(BF16), gqa_w_out[j].astype(BF16)
                gq = gqa_q_norm[j].reshape(1, LANES) * (dh ** -0.5 * LOG2E)
                gk = gqa_k_norm[j].reshape(1, LANES)
                nq = d // LANES
                nk = nv = (w_in.shape[1] - d) // (2 * LANES)
                lamp, subln, lambda_init = dummy_lam, dummy_sub, 0.0
            pmat = _seg_matrix(dh)
            cos, sa, sb = _rope_tables(l, dh)
            kw = dict(nq=nq, nk=nk, nv=nv, rot=dh // 4)
            ql, kl, vl = _pre_attn(xl, modl, l, g1, w_in, pmat, gq, gk, cos, sa, sb, l, **kw)
            qc, kc, vc = _pre_attn(xc, modc, nc, g1, w_in, pmat, gq, gk, *ones_tab, lc, **kw)
            r3 = lambda t, n: t.reshape(b, n, t.shape[1])
            kc3, vc3 = r3(kc, lc), r3(vc, lc)
            akw = dict(diff=diff, lambda_init=lambda_init)
            ol = _attention(r3(ql, l), kc3, vc3, r3(kl, l), r3(vl, l), lamp, subln, **akw).reshape(nl, d)
            xl = _post(xl, ol, modl, l, w_out, g2, wgu, wd)
            if not last:
                oc = _attention(r3(qc, lc), kc3, vc3, None, None, lamp, subln, **akw).reshape(nc, d)
                xc = _post(xc, oc, modc, nc, w_out, g2, wgu, wd)
        else:
            dk, dv = gla_gate_w2_fwd.shape[2], gla_w_out.shape[1]
            rank = gla_gate_w1_fwd.shape[2]
            w1 = jnp.zeros((d, MXU_N), F32).at[:, :rank].set(gla_gate_w1_fwd[j]).at[:, rank:2 * rank].set(gla_gate_w1_bwd[j])
            w_in = jnp.concatenate([gla_w_in[j], w1], axis=1).astype(BF16)
            w2 = (jnp.zeros((MXU_N, 2 * dk), F32).at[:rank, :dk].set(gla_gate_w2_fwd[j])
                  .at[rank:2 * rank, dk:].set(gla_gate_w2_bwd[j])).astype(BF16)
            gb = jnp.concatenate([gla_gate_b_fwd[j], gla_gate_b_bwd[j]]).reshape(1, 2 * dk)
            w_out = gla_w_out[j].astype(BF16)
            dkh, dvh = dk // GLA_HEADS, dv // GLA_HEADS
            onorm = gla_out_norm[j].reshape(1, dvh)
            kw = dict(dk=dk, dv=dv, qscale=dkh ** -0.5)
            ql, kl, vl, rl, gl = _pre_gla(xl, modl, l, g1, w_in, w2, gb, **kw)
            qc, kc, vc, rc, gc = _pre_gla(xc, modc, nc, g1, w_in, w2, gb, **kw)
            r3 = lambda t, n: t.reshape(b, n, t.shape[1])
            s0 = jnp.zeros((b, 2, GLA_HEADS, dvh, dkh), F32)
            ocf, ocb, sc = _gla_scan(r3(qc, lc), r3(kc, lc), r3(vc, lc), r3(gc, lc), s0)
            olf, olb, _ = _gla_scan(r3(ql, l), r3(kl, l), r3(vl, l), r3(gl, l), sc)
            xl = _post(xl, (olf.reshape(nl, dv), olb.reshape(nl, dv), rl), modl, l, w_out, g2, wgu, wd, gla_norm=onorm)
            if not last:
                xc = _post(xc, (ocf.reshape(nc, dv), ocb.reshape(nc, dv), rc), modc, nc, w_out, g2, wgu, wd, gla_norm=onorm)
    return xl.reshape(b, l, d)
```

```python
import functools
import math

import jax
import jax.numpy as jnp
from jax import lax
from jax.experimental import pallas as pl
from jax.experimental.pallas import tpu as pltpu

F32 = jnp.float32
BF16 = jnp.bfloat16

GRID_W = 64
N_MIXERS = 3
ROPE_BASE = 10000.0
NORM_EPS = 1e-6
DIFF_DH = 64
GQA_DH = 128
GQA_GROUP = 4
GLA_HEADS = 4
GLA_GATE_RANK = 16
GLA_TAU = 16.0
GLA_CHUNK = 64
GLA_SUB = 16
LOG2E = math.log2(math.e)
ATTN_MIN_DENOM = 1e-22

LANES = 128
MXU_N = 256
VMEM_LIMIT = 56 * 1024 * 1024

TM_PRE = 512
TM_POST = 512
FFN_CHUNK = 256
TQ_DIFF = 512
TQ_GQA = 256
TK_ATTN = 4096
GLA_R = 4


def _const_spec(shape):
    nd = len(shape)
    return pl.BlockSpec(shape, lambda *_: (0,) * nd, pipeline_mode=pl.Buffered(1))


def _params(sem, vmem=VMEM_LIMIT):
    return pltpu.CompilerParams(dimension_semantics=sem, vmem_limit_bytes=vmem)


def _sigmoid(x):
    return 1.0 / (1.0 + jnp.exp(-x))


def _norm_mod(x, g, shift, scale):
    ms = jnp.mean(x * x, axis=-1, keepdims=True)
    return (x * lax.rsqrt(ms + NORM_EPS) * g) * (1.0 + scale) + shift


def _dot(a, b):
    return jnp.dot(a, b, preferred_element_type=F32)


def _dot_nt(a, b):
    return lax.dot_general(a, b, (((1,), (1,)), ((), ())), preferred_element_type=F32)


def _dot_tn(a, b):
    return lax.dot_general(a, b, (((0,), (0,)), ((), ())), preferred_element_type=F32)


def _split3(x):
    hi = x.astype(BF16)
    r1 = x - hi.astype(F32)
    mid = r1.astype(BF16)
    lo = (r1 - mid.astype(F32)).astype(BF16)
    return hi, mid, lo


def _mod_kernel(cond_ref, w_ref, b_ref, o_ref):
    c = cond_ref[...]
    s = (c * _sigmoid(c)).astype(BF16)
    o_ref[0] = _dot(s, w_ref[0].astype(BF16)) + b_ref[0]


def _modulation(cond, ada_w, ada_b):
    depth, d, n = ada_w.shape
    rows = cond.shape[0]
    tn = 1536
    return pl.pallas_call(
        _mod_kernel,
        grid=(depth, n // tn),
        in_specs=[
            pl.BlockSpec((rows, d), lambda i, j: (0, 0)),
            pl.BlockSpec((1, d, tn), lambda i, j: (i, 0, j)),
            pl.BlockSpec((1, 1, tn), lambda i, j: (i, 0, j)),
        ],
        out_specs=pl.BlockSpec((1, rows, tn), lambda i, j: (i, 0, j)),
        out_shape=jax.ShapeDtypeStruct((depth, rows, n), F32),
        compiler_params=_params(("parallel", "parallel")),
        name="adaln_mod",
    )(cond, ada_w, ada_b.reshape(depth, 1, n))


def _seg_mean(x2, p):
    hi = x2.astype(BF16)
    lo = (x2 - hi.astype(F32)).astype(BF16)
    return _dot(hi, p) + _dot(lo, p)


def _pre_attn_kernel(x_ref, mod_ref, g1_ref, w_ref, p_ref, gq_ref, gk_ref, cos_ref, sa_ref, sb_ref,
                     q_ref, k_ref, v_ref, *, d, nq, nk, nv, rot):
    x = x_ref[...]
    m = mod_ref[0]
    h = _norm_mod(x, g1_ref[...], m[:, 0:d], m[:, d:2 * d]).astype(BF16)
    p = p_ref[...]
    cos, sa, sb = cos_ref[...], sa_ref[...], sb_ref[...]

    def norm_rope(xs, gain):
        ms = _seg_mean(xs * xs, p)
        xn = xs * lax.rsqrt(ms + NORM_EPS) * gain
        return xn * cos + pltpu.roll(xn, LANES - rot, 1) * sa + pltpu.roll(xn, rot, 1) * sb

    per = MXU_N // LANES
    for grp in range((nq + nk + nv) // per):
        y = _dot(h, w_ref[:, grp * MXU_N:(grp + 1) * MXU_N])
        for t in range(per):
            slab = grp * per + t
            ys = y[:, t * LANES:(t + 1) * LANES]
            if slab < nq:
                q_ref[:, slab * LANES:(slab + 1) * LANES] = norm_rope(ys, gq_ref[...]).astype(BF16)
            elif slab < nq + nk:
                j = slab - nq
                k_ref[:, j * LANES:(j + 1) * LANES] = norm_rope(ys, gk_ref[...]).astype(BF16)
            else:
                j = slab - nq - nk
                v_ref[:, j * LANES:(j + 1) * LANES] = ys.astype(BF16)


def _pre_attn(x2d, modx, rows_per_mod, g1, w, pmat, gq, gk, cos, sa, sb, tab_rows, *, nq, nk, nv, rot):
    n, d = x2d.shape
    tm = min(TM_PRE, tab_rows)
    tab_blocks = tab_rows // tm
    mod_blocks = rows_per_mod // tm
    row = lambda c: pl.BlockSpec((tm, c), lambda i: (i, 0))
    tab = pl.BlockSpec((tm, LANES), lambda i: (i % tab_blocks, 0))
    kern = functools.partial(_pre_attn_kernel, d=d, nq=nq, nk=nk, nv=nv, rot=rot)
    return pl.pallas_call(
        kern,
        grid=(n // tm,),
        in_specs=[
            row(d),
            pl.BlockSpec((1, 1, 6 * d), lambda i: (i // mod_blocks, 0, 0)),
            _const_spec((1, d)),
            _const_spec(w.shape),
            _const_spec((LANES, LANES)),
            _const_spec((1, LANES)),
            _const_spec((1, LANES)),
            tab, tab, tab,
        ],
        out_specs=[row(nq * LANES), row(nk * LANES), row(nv * LANES)],
        out_shape=[jax.ShapeDtypeStruct((n, c * LANES), BF16) for c in (nq, nk, nv)],
        compiler_params=_params(("parallel",)),
        name="pre_attn",
    )(x2d, modx, g1, w, pmat, gq, gk, cos, sa, sb)


def _pre_gla_kernel(x_ref, mod_ref, g1_ref, w_ref, w2_ref, gb_ref,
                    q_ref, k_ref, v_ref, r_ref, g_ref, *, d, dk, dv, qscale):
    x = x_ref[...]
    m = mod_ref[0]
    h = _norm_mod(x, g1_ref[...], m[:, 0:d], m[:, d:2 * d]).astype(BF16)
    bounds = (dk, 2 * dk, 2 * dk + dv, 2 * dk + 2 * dv)
    ngrp = w_ref.shape[1] // MXU_N
    for grp in range(ngrp):
        c0 = grp * MXU_N
        y = _dot(h, w_ref[:, c0:c0 + MXU_N])
        if c0 < bounds[0]:
            q_ref[:, c0:c0 + MXU_N] = (y * qscale).astype(BF16)
        elif c0 < bounds[1]:
            k_ref[:, c0 - bounds[0]:c0 - bounds[0] + MXU_N] = y.astype(BF16)
        elif c0 < bounds[2]:
            v_ref[:, c0 - bounds[1]:c0 - bounds[1] + MXU_N] = y.astype(BF16)
        elif c0 < bounds[3]:
            r_ref[:, c0 - bounds[2]:c0 - bounds[2] + MXU_N] = y.astype(BF16)
        else:
            z = _dot(y.astype(BF16), w2_ref[...]) + gb_ref[...]
            logsig = jnp.minimum(z, 0.0) - jnp.log(1.0 + jnp.exp(-jnp.abs(z)))
            g_ref[...] = logsig * (1.0 / GLA_TAU)


def _pre_gla(x2d, modx, rows_per_mod, g1, w, w2, gb, *, dk, dv, qscale):
    n, d = x2d.shape
    tm = min(TM_PRE, n)
    mod_blocks = rows_per_mod // tm
    row = lambda c: pl.BlockSpec((tm, c), lambda i: (i, 0))
    kern = functools.partial(_pre_gla_kernel, d=d, dk=dk, dv=dv, qscale=qscale)
    return pl.pallas_call(
        kern,
        grid=(n // tm,),
        in_specs=[
            row(d),
            pl.BlockSpec((1, 1, 6 * d), lambda i: (i // mod_blocks, 0, 0)),
            _const_spec((1, d)),
            _const_spec(w.shape),
            _const_spec(w2.shape),
            _const_spec((1, 2 * dk)),
        ],
        out_specs=[row(dk), row(dk), row(dv), row(dv), row(2 * dk)],
        out_shape=[jax.ShapeDtypeStruct((n, c), t) for c, t in
                   ((dk, BF16), (dk, BF16), (dv, BF16), (dv, BF16), (2 * dk, F32))],
        compiler_params=_params(("parallel",)),
        name="pre_gla",
    )(x2d, modx, g1, w, w2, gb)


def _attn_kernel(*refs, diff, tq, tk, n_lat, lambda_init):
    if n_lat:
        q_ref, kc_ref, vc_ref, k_ref, v_ref, pn_ref, lam_ref, sub_ref, o_ref, kn_sc, m_sc, l1_sc, l_sc, acc_sc = refs
    else:
        q_ref, kc_ref, vc_ref, pn_ref, lam_ref, sub_ref, o_ref, kn_sc, m_sc, l1_sc, l_sc, acc_sc = refs
        k_ref = v_ref = None
    lane = lax.broadcasted_iota(jnp.int32, (1, LANES), 1)
    lo_lane = lane < DIFF_DH

    @pl.when(pl.program_id(2) == 0)
    def _():
        pn = pn_ref[...]

        def seg_norm2_max(kb):
            kf = kb.astype(F32)
            return jnp.max(_dot((kf * kf).astype(BF16), pn), axis=0, keepdims=True)

        r = seg_norm2_max(kc_ref[0])
        if n_lat:
            def body(i, r):
                off = pl.multiple_of(i * tk, tk)
                return jnp.maximum(r, seg_norm2_max(k_ref[0, pl.ds(off, tk), :]))
            r = lax.fori_loop(0, n_lat, body, r)
        if diff:
            kn_sc[0:1, :] = jnp.broadcast_to(jnp.max(jnp.where(lo_lane, r, 0.0), axis=1, keepdims=True), (1, LANES))
            kn_sc[1:2, :] = jnp.broadcast_to(jnp.max(jnp.where(lo_lane, 0.0, r), axis=1, keepdims=True), (1, LANES))
        else:
            kn_sc[0:1, :] = r

    q = q_ref[0]
    if diff:
        lo_mask = jnp.where(lo_lane, 1.0, 0.0).astype(BF16)
        qs = jnp.concatenate([q * lo_mask, q * (1.0 - lo_mask)], axis=0)
        kn = jnp.concatenate([jnp.broadcast_to(kn_sc[0:1, :], (tq, LANES)),
                              jnp.broadcast_to(kn_sc[1:2, :], (tq, LANES))], axis=0)
    else:
        qs = jnp.concatenate([q[:, g * LANES:(g + 1) * LANES] for g in range(GQA_GROUP)], axis=0)
        kn = kn_sc[0:1, :]
    qf = qs.astype(F32)
    qn2 = _dot((qf * qf).astype(BF16), jnp.ones((LANES, LANES), BF16))
    shift = jnp.sqrt(qn2 * kn)

    def fast_step(kb, vb, first):
        s = _dot_nt(qs, kb)
        psum, ps = None, []
        for c in range(kb.shape[0] // LANES):
            pc = jnp.exp2(s[:, c * LANES:(c + 1) * LANES] - shift)
            psum = pc if psum is None else psum + pc
            ps.append(pc.astype(BF16))
        pv = _dot(jnp.concatenate(ps, axis=1), vb)
        if first:
            l_sc[...] = psum
            acc_sc[...] = pv
        else:
            l_sc[...] += psum
            acc_sc[...] += pv

    def exact_step(kb, vb, first):
        s = _dot_nt(qs, kb)
        m_cur = jnp.max(s, axis=-1, keepdims=True)
        if first:
            m_new = m_cur
            p = jnp.exp2(s - m_new)
            l1_sc[...] = jnp.sum(p, axis=-1, keepdims=True)
            acc_sc[...] = _dot(p.astype(BF16), vb)
        else:
            m_old = m_sc[...]
            m_new = jnp.maximum(m_old, m_cur)
            alpha = jnp.exp2(m_old - m_new)
            p = jnp.exp2(s - m_new)
            l1_sc[...] = alpha * l1_sc[...] + jnp.sum(p, axis=-1, keepdims=True)
            acc_sc[...] = alpha * acc_sc[...] + _dot(p.astype(BF16), vb)
        m_sc[...] = m_new

    def sweep(step):
        step(kc_ref[0], vc_ref[0], True)
        if n_lat:
            def body(i, carry):
                off = pl.multiple_of(i * tk, tk)
                step(k_ref[0, pl.ds(off, tk), :], v_ref[0, pl.ds(off, tk), :], False)
                return carry
            lax.fori_loop(0, n_lat, body, 0)

    def finish(o):
        if diff:
            lp = lam_ref[...]
            lam = (jnp.exp(jnp.sum(lp[0:1] * lp[1:2], axis=-1, keepdims=True))
                   - jnp.exp(jnp.sum(lp[2:3] * lp[3:4], axis=-1, keepdims=True)) + lambda_init)
            a = o[:tq] - lam * o[tq:]
            ms = jnp.mean(a * a, axis=-1, keepdims=True)
            a = a * lax.rsqrt(ms + NORM_EPS) * sub_ref[...] * (1.0 - lambda_init)
            o_ref[0] = a.astype(BF16)
        else:
            for g in range(GQA_GROUP):
                o_ref[0, :, g * LANES:(g + 1) * LANES] = o[g * tq:(g + 1) * tq].astype(BF16)

    sweep(fast_step)
    denom = jnp.sum(l_sc[...], axis=-1, keepdims=True)
    finish(acc_sc[...] / denom)

    @pl.when(jnp.logical_not(jnp.min(denom) >= ATTN_MIN_DENOM))
    def _():
        sweep(exact_step)
        finish(acc_sc[...] / l1_sc[...])


def _attention(q, kc, vc, k, v, lamp, subln, *, diff, lambda_init):
    b, t, _ = q.shape
    lc = kc.shape[1]
    tq = min(TQ_DIFF if diff else TQ_GQA, t)
    qw = LANES if diff else GQA_GROUP * LANES
    heads = kc.shape[2] // LANES
    stack = (2 if diff else GQA_GROUP) * tq
    n_lat = 0
    ins = [q, kc, vc]
    specs = [
        pl.BlockSpec((1, tq, qw), lambda bi, h, i: (bi, i, h)),
        pl.BlockSpec((1, lc, LANES), lambda bi, h, i: (bi, 0, h)),
        pl.BlockSpec((1, lc, LANES), lambda bi, h, i: (bi, 0, h)),
    ]
    tk = TK_ATTN
    if k is not None:
        l = k.shape[1]
        tk = min(TK_ATTN, l)
        n_lat = l // tk
        ins += [k, v]
        specs += [pl.BlockSpec((1, l, LANES), lambda bi, h, i: (bi, 0, h))] * 2
    seg = DIFF_DH if diff else LANES
    idx = jnp.arange(LANES) // seg
    pn = jnp.where(idx[:, None] == idx[None, :], 1.0, 0.0).astype(BF16)
    ins += [pn, lamp, subln]
    specs += [pl.BlockSpec(a.shape, lambda bi, h, i: (0, 0)) for a in (pn, lamp, subln)]
    kern = functools.partial(_attn_kernel, diff=diff, tq=tq, tk=tk, n_lat=n_lat, lambda_init=lambda_init)
    return pl.pallas_call(
        kern,
        grid=(b, heads, t // tq),
        in_specs=specs,
        out_specs=pl.BlockSpec((1, tq, qw), lambda bi, h, i: (bi, i, h)),
        out_shape=jax.ShapeDtypeStruct((b, t, heads * qw), BF16),
        scratch_shapes=[pltpu.VMEM((8, LANES), F32), pltpu.VMEM((stack, 1), F32), pltpu.VMEM((stack, 1), F32),
                        pltpu.VMEM((stack, LANES), F32), pltpu.VMEM((stack, LANES), F32)],
        compiler_params=_params(("parallel", "parallel", "arbitrary")),
        name="attn_diff" if diff else "attn_gqa",
    )(*ins)


def _gla_chunk(q, k, v, g, st_ref, rev):
    c = GLA_CHUNK
    row = lax.broadcasted_iota(jnp.int32, (c, c), 0)
    col = lax.broadcasted_iota(jnp.int32, (c, c), 1)
    keep = (col >= row) if rev else (col <= row)
    tri = jnp.where(keep, 1.0, 0.0).astype(BF16)
    hi, mid, lo = _split3(g)
    bsum = _dot(tri, hi) + _dot(tri, mid) + _dot(tri, lo)
    b_tot = bsum[0:1] if rev else bsum[c - 1:c]
    qf, kf = q.astype(F32), k.astype(F32)
    st = st_ref[...]
    o = _dot_nt((qf * jnp.exp(bsum)).astype(BF16), st.astype(BF16))

    krow = lax.broadcasted_iota(jnp.int32, bsum.shape, 0)
    nsub = c // GLA_SUB
    blocks = []
    for i in range(nsub):
        lo_r, hi_r = i * GLA_SUB, (i + 1) * GLA_SUB
        if rev:
            ref = bsum[hi_r:hi_r + 1] if i < nsub - 1 else jnp.zeros_like(b_tot)
            valid = krow >= lo_r
        else:
            ref = bsum[lo_r - 1:lo_r] if i > 0 else jnp.zeros_like(b_tot)
            valid = krow < hi_r
        ki = (kf * jnp.exp(jnp.where(valid, ref - bsum, 0.0))).astype(BF16)
        qi = (qf[lo_r:hi_r] * jnp.exp(bsum[lo_r:hi_r] - ref)).astype(BF16)
        blocks.append(_dot_nt(qi, ki))
    a = jnp.where(keep, jnp.concatenate(blocks, axis=0), 0.0)
    o = o + _dot(a.astype(BF16), v)

    kd = (kf * jnp.exp(b_tot - bsum)).astype(BF16)
    st_ref[...] = jnp.exp(b_tot) * st + _dot_tn(v, kd)
    return o


def _gla_kernel(qf_ref, kf_ref, vf_ref, gf_ref, qb_ref, kb_ref, vb_ref, gb_ref, s0_ref,
                of_ref, ob_ref, st_out_ref, st_sc, *, nchunk, dkh, dvh):
    j = pl.program_id(1)

    @pl.when(j == 0)
    def _():
        st_sc[...] = s0_ref[0]

    c = GLA_CHUNK
    for r in range(nchunk):
        rf, rb = r, nchunk - 1 - r
        for h in range(GLA_HEADS):
            kc, vc = slice(h * dkh, (h + 1) * dkh), slice(h * dvh, (h + 1) * dvh)
            rows = slice(rf * c, (rf + 1) * c)
            o = _gla_chunk(qf_ref[0, rows, kc], kf_ref[0, rows, kc], vf_ref[0, rows, vc], gf_ref[0, rows, kc],
                           st_sc.at[0, h], False)
            of_ref[0, rows, vc] = o.astype(BF16)
            rows = slice(rb * c, (rb + 1) * c)
            o = _gla_chunk(qb_ref[0, rows, kc], kb_ref[0, rows, kc], vb_ref[0, rows, vc], gb_ref[0, rows, kc],
                           st_sc.at[1, h], True)
            ob_ref[0, rows, vc] = o.astype(BF16)

    @pl.when(j == pl.num_programs(1) - 1)
    def _():
        st_out_ref[0] = st_sc[...]


def _gla_scan(q, k, v, g, s0):
    b, t, dk = q.shape
    dv = v.shape[2]
    dkh, dvh = dk // GLA_HEADS, dv // GLA_HEADS
    nchunk = min(GLA_R, t // GLA_CHUNK)
    rows = nchunk * GLA_CHUNK
    n = t // rows
    fwd = lambda w, cb=0: pl.BlockSpec((1, rows, w), lambda bi, j: (bi, j, cb))
    bwd = lambda w, cb=0: pl.BlockSpec((1, rows, w), lambda bi, j: (bi, n - 1 - j, cb))
    st_spec = pl.BlockSpec((1, 2, GLA_HEADS, dvh, dkh), lambda bi, j: (bi, 0, 0, 0, 0))
    kern = functools.partial(_gla_kernel, nchunk=nchunk, dkh=dkh, dvh=dvh)
    return pl.pallas_call(
        kern,
        grid=(b, n),
        in_specs=[fwd(dk), fwd(dk), fwd(dv), fwd(dk, 0), bwd(dk), bwd(dk), bwd(dv), bwd(dk, 1), st_spec],
        out_specs=[fwd(dv), bwd(dv), st_spec],
        out_shape=[jax.ShapeDtypeStruct((b, t, dv), BF16), jax.ShapeDtypeStruct((b, t, dv), BF16),
                   jax.ShapeDtypeStruct(s0.shape, F32)],
        scratch_shapes=[pltpu.VMEM((2, GLA_HEADS, dvh, dkh), F32)],
        compiler_params=_params(("parallel", "arbitrary")),
        name="gla_scan",
    )(q, k, v, g, q, k, v, g, s0)


def _post_kernel(*refs, d, f, gla, dvh):
    if gla:
        x_ref, of_ref, ob_ref, r_ref, on_ref, mod_ref, wo_ref, g2_ref, wgu_ref, wd_ref, o_ref = refs
        r = r_ref[...].astype(F32)
        gate = r * _sigmoid(r)
        parts = []
        for h in range(GLA_HEADS):
            cs = slice(h * dvh, (h + 1) * dvh)
            oh = of_ref[:, cs].astype(F32) + ob_ref[:, cs].astype(F32)
            ms = jnp.mean(oh * oh, axis=-1, keepdims=True)
            parts.append((oh * lax.rsqrt(ms + NORM_EPS) * on_ref[...] * gate[:, cs]).astype(BF16))
        a = jnp.concatenate(parts, axis=1)
    else:
        x_ref, a_ref, mod_ref, wo_ref, g2_ref, wgu_ref, wd_ref, o_ref = refs
        a = a_ref[...]
    m = mod_ref[0]
    gate1, shift2, scale2, gate2 = (m[:, 2 * d:3 * d], m[:, 3 * d:4 * d], m[:, 4 * d:5 * d], m[:, 5 * d:6 * d])
    x1 = x_ref[...] + gate1 * _dot(a, wo_ref[...])
    h2 = _norm_mod(x1, g2_ref[...], shift2, scale2).astype(BF16)
    acc = jnp.zeros(x1.shape, F32)
    for c0 in range(0, f, FFN_CHUNK):
        gt = _dot(h2, wgu_ref[:, c0:c0 + FFN_CHUNK])
        up = _dot(h2, wgu_ref[:, f + c0:f + c0 + FFN_CHUNK])
        act = (gt * _sigmoid(gt) * up).astype(BF16)
        acc = acc + _dot(act, wd_ref[c0:c0 + FFN_CHUNK, :])
    o_ref[...] = x1 + gate2 * acc


def _post(x2d, mix, modx, rows_per_mod, wo, g2, wgu, wd, *, gla_norm=None):
    n, d = x2d.shape
    f = wd.shape[0]
    tm = min(TM_POST, n)
    mod_blocks = rows_per_mod // tm
    row = lambda c: pl.BlockSpec((tm, c), lambda i: (i, 0))
    gla = gla_norm is not None
    if gla:
        ins = [x2d, *mix, gla_norm]
        specs = [row(d), row(d), row(d), row(d), _const_spec(gla_norm.shape)]
        dvh = gla_norm.shape[1]
    else:
        ins = [x2d, mix]
        specs = [row(d), row(d)]
        dvh = 0
    ins += [modx, wo, g2, wgu, wd]
    specs += [pl.BlockSpec((1, 1, 6 * d), lambda i: (i // mod_blocks, 0, 0)),
              _const_spec(wo.shape), _const_spec((1, d)), _const_spec(wgu.shape), _const_spec(wd.shape)]
    kern = functools.partial(_post_kernel, d=d, f=f, gla=gla, dvh=dvh)
    return pl.pallas_call(
        kern,
        grid=(n // tm,),
        in_specs=specs,
        out_specs=row(d),
        out_shape=jax.ShapeDtypeStruct((n, d), F32),
        compiler_params=_params(("parallel",)),
        name="post_gla" if gla else "post_attn",
    )(*ins)


def _rope_tables(n_tokens, head_dim):
    rows = n_tokens // GRID_W
    rowp = jnp.broadcast_to(jnp.arange(rows)[:, None], (rows, GRID_W)).reshape(-1)
    colp = jnp.broadcast_to(jnp.arange(GRID_W)[None, :], (rows, GRID_W)).reshape(-1)
    half = head_dim // 2
    quarter = head_dim // 4
    inv = ROPE_BASE ** (-jnp.arange(0, half, 2, dtype=F32) / half)

    def axis_angles(pos):
        a = pos.astype(F32)[:, None] * inv[None, :]
        return jnp.concatenate([a, a], axis=-1)

    ang = jnp.concatenate([axis_angles(rowp), axis_angles(colp)], axis=-1)
    reps = LANES // head_dim
    cos = jnp.tile(jnp.cos(ang), (1, reps))
    sin = jnp.tile(jnp.sin(ang), (1, reps))
    first = (jnp.arange(LANES) % (2 * quarter)) < quarter
    sa = jnp.where(first[None, :], -sin, 0.0)
    sb = jnp.where(first[None, :], 0.0, sin)
    return cos, sa, sb


def _seg_matrix(seg):
    idx = jnp.arange(LANES) // seg
    return jnp.where(idx[:, None] == idx[None, :], 1.0 / seg, 0.0).astype(BF16)


def kernel(x, c, ctx, c_ctx, ada_w, ada_b, norm1_g, norm2_g, ffn_w_gu, ffn_w_down, diff_w_in, diff_w_out, diff_q_norm, diff_k_norm, diff_lambda_q1, diff_lambda_k1, diff_lambda_q2, diff_lambda_k2, diff_subln, gqa_w_in, gqa_w_out, gqa_q_norm, gqa_k_norm, gla_w_in, gla_gate_w1_fwd, gla_gate_w2_fwd, gla_gate_b_fwd, gla_gate_w1_bwd, gla_gate_w2_bwd, gla_gate_b_bwd, gla_out_norm, gla_w_out):
    b, l, d = x.shape
    lc = ctx.shape[1]
    depth = ada_w.shape[0]
    nl, nc = b * l, b * lc

    cond_rows = ((b + 1 + 7) // 8) * 8
    cond = jnp.zeros((cond_rows, d), F32).at[:b].set(c).at[b].set(c_ctx)
    mod = _modulation(cond, ada_w, ada_b)

    xl = x.reshape(nl, d)
    xc = ctx.reshape(nc, d)
    ones_tab = (jnp.ones((lc, LANES), F32), jnp.zeros((lc, LANES), F32), jnp.zeros((lc, LANES), F32))
    dummy_lam = jnp.zeros((4, DIFF_DH), F32)
    dummy_sub = jnp.ones((1, LANES), F32)

    for i in range(depth):
        last = i == depth - 1
        kind, j = i % N_MIXERS, i // N_MIXERS
        modl = mod[i, :b].reshape(b, 1, 6 * d)
        modc = mod[i, b:b + 1].reshape(1, 1, 6 * d)
        g1 = norm1_g[i].reshape(1, d)
        g2 = norm2_g[i].reshape(1, d)
        wgu = ffn_w_gu[i].astype(BF16)
        wd = ffn_w_down[i].astype(BF16)

        if kind in (0, 1):
            if kind == 0:
                diff, dh = True, DIFF_DH
                w_in, w_out = diff_w_in[j].astype(BF16), diff_w_out[j].astype(BF16)
                gq = jnp.tile(diff_q_norm[j], LANES // dh).reshape(1, LANES) * (dh ** -0.5 * LOG2E)
                gk = jnp.tile(diff_k_norm[j], LANES // dh).reshape(1, LANES)
                nq = nk = nv = d // LANES
                lamp = jnp.stack([diff_lambda_q1[j], diff_lambda_k1[j], diff_lambda_q2[j], diff_lambda_k2[j]])
                subln = diff_subln[j].reshape(1, LANES)
                lambda_init = 0.8 - 0.6 * math.exp(-0.3 * i)
            else:
                diff, dh = False, GQA_DH
                w_in, w_out = gqa_w_in[j].astype(BF16), gqa_w_out[j].astype(BF16)
                gq = gqa_q_norm[j].reshape(1, LANES) * (dh ** -0.5 * LOG2E)
                gk = gqa_k_norm[j].reshape(1, LANES)
                nq = d // LANES
                nk = nv = (w_in.shape[1] - d) // (2 * LANES)
                lamp, subln, lambda_init = dummy_lam, dummy_sub, 0.0
            pmat = _seg_matrix(dh)
            cos, sa, sb = _rope_tables(l, dh)
            kw = dict(nq=nq, nk=nk, nv=nv, rot=dh // 4)
            ql, kl, vl = _pre_attn(xl, modl, l, g1, w_in, pmat, gq, gk, cos, sa, sb, l, **kw)
            qc, kc, vc = _pre_attn(xc, modc, nc, g1, w_in, pmat, gq, gk, *ones_tab, lc, **kw)
            r3 = lambda t, n: t.reshape(b, n, t.shape[1])
            kc3, vc3 = r3(kc, lc), r3(vc, lc)
            akw = dict(diff=diff, lambda_init=lambda_init)
            ol = _attention(r3(ql, l), kc3, vc3, r3(kl, l), r3(vl, l), lamp, subln, **akw).reshape(nl, d)
            xl = _post(xl, ol, modl, l, w_out, g2, wgu, wd)
            if not last:
                oc = _attention(r3(qc, lc), kc3, vc3, None, None, lamp, subln, **akw).reshape(nc, d)
                xc = _post(xc, oc, modc, nc, w_out, g2, wgu, wd)
        else:
            dk, dv = gla_gate_w2_fwd.shape[2], gla_w_out.shape[1]
            rank = gla_gate_w1_fwd.shape[2]
            w1 = jnp.zeros((d, MXU_N), F32).at[:, :rank].set(gla_gate_w1_fwd[j]).at[:, rank:2 * rank].set(gla_gate_w1_bwd[j])
            w_in = jnp.concatenate([gla_w_in[j], w1], axis=1).astype(BF16)
            w2 = (jnp.zeros((MXU_N, 2 * dk), F32).at[:rank, :dk].set(gla_gate_w2_fwd[j])
                  .at[rank:2 * rank, dk:].set(gla_gate_w2_bwd[j])).astype(BF16)
            gb = jnp.concatenate([gla_gate_b_fwd[j], gla_gate_b_bwd[j]]).reshape(1, 2 * dk)
            w_out = gla_w_out[j].astype(BF16)
            dkh, dvh = dk // GLA_HEADS, dv // GLA_HEADS
            onorm = gla_out_norm[j].reshape(1, dvh)
            kw = dict(dk=dk, dv=dv, qscale=dkh ** -0.5)
            ql, kl, vl, rl, gl = _pre_gla(xl, modl, l, g1, w_in, w2, gb, **kw)
            qc, kc, vc, rc, gc = _pre_gla(xc, modc, nc, g1, w_in, w2, gb, **kw)
            r3 = lambda t, n: t.reshape(b, n, t.shape[1])
            s0 = jnp.zeros((b, 2, GLA_HEADS, dvh, dkh), F32)
            ocf, ocb, sc = _gla_scan(r3(qc, lc), r3(kc, lc), r3(vc, lc), r3(gc, lc), s0)
            olf, olb, _ = _gla_scan(r3(ql, l), r3(kl, l), r3(vl, l), r3(gl, l), sc)
            xl = _post(xl, (olf.reshape(nl, dv), olb.reshape(nl, dv), rl), modl, l, w_out, g2, wgu, wd, gla_norm=onorm)
            if not last:
                xc = _post(xc, (ocf.reshape(nc, dv), ocb.reshape(nc, dv), rc), modc, nc, w_out, g2, wgu, wd, gla_norm=onorm)
    return xl.reshape(b, l, d)
```

```python
import functools
import math

import jax
import jax.numpy as jnp
from jax import lax
from jax.experimental import pallas as pl
from jax.experimental.pallas import tpu as pltpu

F32 = jnp.float32
BF16 = jnp.bfloat16

GRID_W = 64
N_MIXERS = 3
ROPE_BASE = 10000.0
NORM_EPS = 1e-6
DIFF_DH = 64
GQA_DH = 128
GQA_GROUP = 4
GLA_HEADS = 4
GLA_GATE_RANK = 16
GLA_TAU = 16.0
GLA_CHUNK = 64
GLA_SUB = 16
LOG2E = math.log2(math.e)
ATTN_MIN_DENOM = 1e-22

LANES = 128
MXU_N = 256
VMEM_LIMIT = 56 * 1024 * 1024

TM_PRE = 512
TM_POST = 512
FFN_CHUNK = 256
TQ_DIFF = 512
TQ_GQA = 256
TK_ATTN = 4096
GLA_R = 4


def _const_spec(shape):
    nd = len(shape)
    return pl.BlockSpec(shape, lambda *_: (0,) * nd, pipeline_mode=pl.Buffered(1))


def _params(sem, vmem=VMEM_LIMIT):
    return pltpu.CompilerParams(dimension_semantics=sem, vmem_limit_bytes=vmem)


def _sigmoid(x):
    return 1.0 / (1.0 + jnp.exp(-x))


def _norm_mod(x, g, shift, scale):
    ms = jnp.mean(x * x, axis=-1, keepdims=True)
    return (x * lax.rsqrt(ms + NORM_EPS) * g) * (1.0 + scale) + shift


def _dot(a, b):
    return jnp.dot(a, b, preferred_element_type=F32)


def _dot_nt(a, b):
    return lax.dot_general(a, b, (((1,), (1,)), ((), ())), preferred_element_type=F32)


def _dot_tn(a, b):
    return lax.dot_general(a, b, (((0,), (0,)), ((), ())), preferred_element_type=F32)


def _split3(x):
    hi = x.astype(BF16)
    r1 = x - hi.astype(F32)
    mid = r1.astype(BF16)
    lo = (r1 - mid.astype(F32)).astype(BF16)
    return hi, mid, lo


def _mod_kernel(cond_ref, w_ref, b_ref, o_ref):
    c = cond_ref[...]
    s = (c * _sigmoid(c)).astype(BF16)
    o_ref[0] = _dot(s, w_ref[0].astype(BF16)) + b_ref[0]


def _modulation(cond, ada_w, ada_b):
    depth, d, n = ada_w.shape
    rows = cond.shape[0]
    tn = 1536
    return pl.pallas_call(
        _mod_kernel,
        grid=(depth, n // tn),
        in_specs=[
            pl.BlockSpec((rows, d), lambda i, j: (0, 0)),
            pl.BlockSpec((1, d, tn), lambda i, j: (i, 0, j)),
            pl.BlockSpec((1, 1, tn), lambda i, j: (i, 0, j)),
        ],
        out_specs=pl.BlockSpec((1, rows, tn), lambda i, j: (i, 0, j)),
        out_shape=jax.ShapeDtypeStruct((depth, rows, n), F32),
        compiler_params=_params(("parallel", "parallel")),
        name="adaln_mod",
    )(cond, ada_w, ada_b.reshape(depth, 1, n))


def _pre_attn_kernel(x_ref, mod_ref, g1_ref, w_ref, p_ref, gq_ref, gk_ref, cos_ref, sa_ref, sb_ref,
                     q_ref, k_ref, v_ref, *, d, nq, nk, nv, rot):
    x = x_ref[...]
    m = mod_ref[0]
    h = _norm_mod(x, g1_ref[...], m[:, 0:d], m[:, d:2 * d]).astype(BF16)
    cos, sa, sb = cos_ref[...], sa_ref[...], sb_ref[...]
    per = MXU_N // LANES

    def project(grp):
        return _dot(h, w_ref[:, grp * MXU_N:(grp + 1) * MXU_N])

    def emit(grp, y):
        slab0 = grp * per
        if slab0 >= nq + nk:
            j = slab0 - nq - nk
            v_ref[:, j * LANES:(j + per) * LANES] = y.astype(BF16)
            return
        ms = _dot((y * y).astype(BF16), p_ref[...])
        yn = y * lax.rsqrt(ms + NORM_EPS)
        for t in range(per):
            slab = slab0 + t
            out_ref, j, gain = (q_ref, slab, gq_ref) if slab < nq else (k_ref, slab - nq, gk_ref)
            xn = yn[:, t * LANES:(t + 1) * LANES] * gain[...]
            r = xn * cos + pltpu.roll(xn, LANES - rot, 1) * sa + pltpu.roll(xn, rot, 1) * sb
            out_ref[:, j * LANES:(j + 1) * LANES] = r.astype(BF16)

    ngrp = (nq + nk + nv) // per
    y = project(0)
    for grp in range(ngrp):
        y_next = project(grp + 1) if grp + 1 < ngrp else None
        emit(grp, y)
        y = y_next


def _pre_attn(x2d, modx, rows_per_mod, g1, w, pmat, gq, gk, cos, sa, sb, tab_rows, *, nq, nk, nv, rot):
    n, d = x2d.shape
    tm = min(TM_PRE, tab_rows)
    tab_blocks = tab_rows // tm
    mod_blocks = rows_per_mod // tm
    row = lambda c: pl.BlockSpec((tm, c), lambda i: (i, 0))
    tab = pl.BlockSpec((tm, LANES), lambda i: (i % tab_blocks, 0))
    kern = functools.partial(_pre_attn_kernel, d=d, nq=nq, nk=nk, nv=nv, rot=rot)
    return pl.pallas_call(
        kern,
        grid=(n // tm,),
        in_specs=[
            row(d),
            pl.BlockSpec((1, 1, 6 * d), lambda i: (i // mod_blocks, 0, 0)),
            _const_spec((1, d)),
            _const_spec(w.shape),
            _const_spec(pmat.shape),
            _const_spec((1, LANES)),
            _const_spec((1, LANES)),
            tab, tab, tab,
        ],
        out_specs=[row(nq * LANES), row(nk * LANES), row(nv * LANES)],
        out_shape=[jax.ShapeDtypeStruct((n, c * LANES), BF16) for c in (nq, nk, nv)],
        compiler_params=_params(("parallel",)),
        name="pre_attn",
    )(x2d, modx, g1, w, pmat, gq, gk, cos, sa, sb)


def _pre_gla_kernel(x_ref, mod_ref, g1_ref, w_ref, w2_ref, gb_ref,
                    q_ref, k_ref, v_ref, r_ref, g_ref, *, d, dk, dv, qscale):
    x = x_ref[...]
    m = mod_ref[0]
    h = _norm_mod(x, g1_ref[...], m[:, 0:d], m[:, d:2 * d]).astype(BF16)
    bounds = (dk, 2 * dk, 2 * dk + dv, 2 * dk + 2 * dv)
    ngrp = w_ref.shape[1] // MXU_N
    for grp in range(ngrp):
        c0 = grp * MXU_N
        y = _dot(h, w_ref[:, c0:c0 + MXU_N])
        if c0 < bounds[0]:
            q_ref[:, c0:c0 + MXU_N] = (y * qscale).astype(BF16)
        elif c0 < bounds[1]:
            k_ref[:, c0 - bounds[0]:c0 - bounds[0] + MXU_N] = y.astype(BF16)
        elif c0 < bounds[2]:
            v_ref[:, c0 - bounds[1]:c0 - bounds[1] + MXU_N] = y.astype(BF16)
        elif c0 < bounds[3]:
            r_ref[:, c0 - bounds[2]:c0 - bounds[2] + MXU_N] = y.astype(BF16)
        else:
            z = _dot(y.astype(BF16), w2_ref[...]) + gb_ref[...]
            logsig = jnp.minimum(z, 0.0) - jnp.log(1.0 + jnp.exp(-jnp.abs(z)))
            g_ref[...] = logsig * (1.0 / GLA_TAU)


def _pre_gla(x2d, modx, rows_per_mod, g1, w, w2, gb, *, dk, dv, qscale):
    n, d = x2d.shape
    tm = min(TM_PRE, n)
    mod_blocks = rows_per_mod // tm
    row = lambda c: pl.BlockSpec((tm, c), lambda i: (i, 0))
    kern = functools.partial(_pre_gla_kernel, d=d, dk=dk, dv=dv, qscale=qscale)
    return pl.pallas_call(
        kern,
        grid=(n // tm,),
        in_specs=[
            row(d),
            pl.BlockSpec((1, 1, 6 * d), lambda i: (i // mod_blocks, 0, 0)),
            _const_spec((1, d)),
            _const_spec(w.shape),
            _const_spec(w2.shape),
            _const_spec((1, 2 * dk)),
        ],
        out_specs=[row(dk), row(dk), row(dv), row(dv), row(2 * dk)],
        out_shape=[jax.ShapeDtypeStruct((n, c), t) for c, t in
                   ((dk, BF16), (dk, BF16), (dv, BF16), (dv, BF16), (2 * dk, F32))],
        compiler_params=_params(("parallel",)),
        name="pre_gla",
    )(x2d, modx, g1, w, w2, gb)


def _attn_kernel(*refs, diff, tq, tk, n_lat, lambda_init):
    if n_lat:
        q_ref, kc_ref, vc_ref, k_ref, v_ref, pn_ref, lam_ref, sub_ref, o_ref, kn_sc, m_sc, l1_sc, l_sc, acc_sc = refs
    else:
        q_ref, kc_ref, vc_ref, pn_ref, lam_ref, sub_ref, o_ref, kn_sc, m_sc, l1_sc, l_sc, acc_sc = refs
        k_ref = v_ref = None
    lane = lax.broadcasted_iota(jnp.int32, (1, LANES), 1)
    lo_lane = lane < DIFF_DH

    @pl.when(pl.program_id(2) == 0)
    def _():
        pn = pn_ref[...]

        def seg_norm2_max(kb):
            kf = kb.astype(F32)
            return jnp.max(_dot((kf * kf).astype(BF16), pn), axis=0, keepdims=True)

        r = seg_norm2_max(kc_ref[0])
        if n_lat:
            def body(i, r):
                off = pl.multiple_of(i * tk, tk)
                return jnp.maximum(r, seg_norm2_max(k_ref[0, pl.ds(off, tk), :]))
            r = lax.fori_loop(0, n_lat, body, r)
        if diff:
            kn_sc[0:1, :] = jnp.broadcast_to(jnp.max(jnp.where(lo_lane, r, 0.0), axis=1, keepdims=True), (1, LANES))
            kn_sc[1:2, :] = jnp.broadcast_to(jnp.max(jnp.where(lo_lane, 0.0, r), axis=1, keepdims=True), (1, LANES))
        else:
            kn_sc[0:1, :] = r

    q = q_ref[0]
    if diff:
        lo_mask = jnp.where(lo_lane, 1.0, 0.0).astype(BF16)
        qs = jnp.concatenate([q * lo_mask, q * (1.0 - lo_mask)], axis=0)
        kn = jnp.concatenate([jnp.broadcast_to(kn_sc[0:1, :], (tq, LANES)),
                              jnp.broadcast_to(kn_sc[1:2, :], (tq, LANES))], axis=0)
    else:
        qs = jnp.concatenate([q[:, g * LANES:(g + 1) * LANES] for g in range(GQA_GROUP)], axis=0)
        kn = kn_sc[0:1, :]
    qf = qs.astype(F32)
    qn2 = _dot((qf * qf).astype(BF16), jnp.ones((LANES, LANES), BF16))
    shift = jnp.sqrt(qn2 * kn)

    def fast_step(kb, vb, first):
        s = _dot_nt(qs, kb)
        psum, ps = None, []
        for c in range(kb.shape[0] // LANES):
            pc = jnp.exp2(s[:, c * LANES:(c + 1) * LANES] - shift)
            psum = pc if psum is None else psum + pc
            ps.append(pc.astype(BF16))
        pv = _dot(jnp.concatenate(ps, axis=1), vb)
        if first:
            l_sc[...] = psum
            acc_sc[...] = pv
        else:
            l_sc[...] += psum
            acc_sc[...] += pv

    def exact_step(kb, vb, first):
        s = _dot_nt(qs, kb)
        m_cur = jnp.max(s, axis=-1, keepdims=True)
        if first:
            m_new = m_cur
            p = jnp.exp2(s - m_new)
            l1_sc[...] = jnp.sum(p, axis=-1, keepdims=True)
            acc_sc[...] = _dot(p.astype(BF16), vb)
        else:
            m_old = m_sc[...]
            m_new = jnp.maximum(m_old, m_cur)
            alpha = jnp.exp2(m_old - m_new)
            p = jnp.exp2(s - m_new)
            l1_sc[...] = alpha * l1_sc[...] + jnp.sum(p, axis=-1, keepdims=True)
            acc_sc[...] = alpha * acc_sc[...] + _dot(p.astype(BF16), vb)
        m_sc[...] = m_new

    def sweep(step):
        step(kc_ref[0], vc_ref[0], True)
        if n_lat:
            def body(i, carry):
                off = pl.multiple_of(i * tk, tk)
                step(k_ref[0, pl.ds(off, tk), :], v_ref[0, pl.ds(off, tk), :], False)
                return carry
            lax.fori_loop(0, n_lat, body, 0)

    def finish(o):
        if diff:
            lp = lam_ref[...]
            lam = (jnp.exp(jnp.sum(lp[0:1] * lp[1:2], axis=-1, keepdims=True))
                   - jnp.exp(jnp.sum(lp[2:3] * lp[3:4], axis=-1, keepdims=True)) + lambda_init)
            a = o[:tq] - lam * o[tq:]
            ms = jnp.mean(a * a, axis=-1, keepdims=True)
            a = a * lax.rsqrt(ms + NORM_EPS) * sub_ref[...] * (1.0 - lambda_init)
            o_ref[0] = a.astype(BF16)
        else:
            for g in range(GQA_GROUP):
                o_ref[0, :, g * LANES:(g + 1) * LANES] = o[g * tq:(g + 1) * tq].astype(BF16)

    sweep(fast_step)
    denom = jnp.sum(l_sc[...], axis=-1, keepdims=True)
    finish(acc_sc[...] / denom)

    @pl.when(jnp.logical_not(jnp.min(denom) >= ATTN_MIN_DENOM))
    def _():
        sweep(exact_step)
        finish(acc_sc[...] / l1_sc[...])


def _attention(q, kc, vc, k, v, lamp, subln, *, diff, lambda_init):
    b, t, _ = q.shape
    lc = kc.shape[1]
    tq = min(TQ_DIFF if diff else TQ_GQA, t)
    qw = LANES if diff else GQA_GROUP * LANES
    heads = kc.shape[2] // LANES
    stack = (2 if diff else GQA_GROUP) * tq
    n_lat = 0
    ins = [q, kc, vc]
    specs = [
        pl.BlockSpec((1, tq, qw), lambda bi, h, i: (bi, i, h)),
        pl.BlockSpec((1, lc, LANES), lambda bi, h, i: (bi, 0, h)),
        pl.BlockSpec((1, lc, LANES), lambda bi, h, i: (bi, 0, h)),
    ]
    tk = TK_ATTN
    if k is not None:
        l = k.shape[1]
        tk = min(TK_ATTN, l)
        n_lat = l // tk
        ins += [k, v]
        specs += [pl.BlockSpec((1, l, LANES), lambda bi, h, i: (bi, 0, h))] * 2
    seg = DIFF_DH if diff else LANES
    idx = jnp.arange(LANES) // seg
    pn = jnp.where(idx[:, None] == idx[None, :], 1.0, 0.0).astype(BF16)
    ins += [pn, lamp, subln]
    specs += [pl.BlockSpec(a.shape, lambda bi, h, i: (0, 0)) for a in (pn, lamp, subln)]
    kern = functools.partial(_attn_kernel, diff=diff, tq=tq, tk=tk, n_lat=n_lat, lambda_init=lambda_init)
    return pl.pallas_call(
        kern,
        grid=(b, heads, t // tq),
        in_specs=specs,
        out_specs=pl.BlockSpec((1, tq, qw), lambda bi, h, i: (bi, i, h)),
        out_shape=jax.ShapeDtypeStruct((b, t, heads * qw), BF16),
        scratch_shapes=[pltpu.VMEM((8, LANES), F32), pltpu.VMEM((stack, 1), F32), pltpu.VMEM((stack, 1), F32),
                        pltpu.VMEM((stack, LANES), F32), pltpu.VMEM((stack, LANES), F32)],
        compiler_params=_params(("parallel", "parallel", "arbitrary")),
        name="attn_diff" if diff else "attn_gqa",
    )(*ins)


def _gla_local(q_ref, k_ref, v_ref, g_ref, *, nchunk, dkh, dvh, rev):
    c = GLA_CHUNK
    rows = nchunk * c
    ri = lax.broadcasted_iota(jnp.int32, (rows, rows), 0)
    ci = lax.broadcasted_iota(jnp.int32, (rows, rows), 1)
    causal = (ci >= ri) if rev else (ci <= ri)
    tri = jnp.where(jnp.logical_and(ri // c == ci // c, causal), 1.0, 0.0).astype(BF16)
    hi, mid, lo = _split3(g_ref[0])
    bsum = _dot(tri, hi) + _dot(tri, mid) + _dot(tri, lo)
    qf, kf = q_ref[0].astype(F32), k_ref[0].astype(F32)
    qt = (qf * jnp.exp(bsum)).astype(BF16)

    krow = lax.broadcasted_iota(jnp.int32, (c, bsum.shape[1]), 0)
    r64 = lax.broadcasted_iota(jnp.int32, (c, c), 0)
    c64 = lax.broadcasted_iota(jnp.int32, (c, c), 1)
    keep = (c64 >= r64) if rev else (c64 <= r64)
    nsub = c // GLA_SUB
    scaled = []
    for n in range(nchunk):
        bs, kc, qc = bsum[n * c:(n + 1) * c], kf[n * c:(n + 1) * c], qf[n * c:(n + 1) * c]
        b_tot = bs[0:1] if rev else bs[c - 1:c]
        kd = (kc * jnp.exp(b_tot - bs)).astype(BF16)
        qis, kis = [], []
        for i in range(nsub):
            lo_r, hi_r = i * GLA_SUB, (i + 1) * GLA_SUB
            if rev:
                ref = bs[hi_r:hi_r + 1] if i < nsub - 1 else jnp.zeros_like(b_tot)
                valid = krow >= lo_r
            else:
                ref = bs[lo_r - 1:lo_r] if i > 0 else jnp.zeros_like(b_tot)
                valid = krow < hi_r
            kis.append((kc * jnp.exp(jnp.where(valid, ref - bs, 0.0))).astype(BF16))
            qis.append((qc[lo_r:hi_r] * jnp.exp(bs[lo_r:hi_r] - ref)).astype(BF16))
        scaled.append((jnp.exp(b_tot), kd, qis, kis))

    pairs = [(n, h) for n in range(nchunk) for h in range(GLA_HEADS)]
    hks = [slice(h * dkh, (h + 1) * dkh) for h in range(GLA_HEADS)]
    vals = {(n, h): v_ref[0, n * c:(n + 1) * c, h * dvh:(h + 1) * dvh] for n, h in pairs}
    blocks = {(n, h): [_dot_nt(scaled[n][2][i][:, hks[h]], scaled[n][3][i][:, hks[h]]) for i in range(nsub)]
              for n, h in pairs}
    inc = {(n, h): _dot_tn(vals[n, h], scaled[n][1][:, hks[h]]) for n, h in pairs}
    amat = {p: jnp.where(keep, jnp.concatenate(blocks[p], axis=0), 0.0).astype(BF16) for p in pairs}
    intra = {p: _dot(amat[p], vals[p]) for p in pairs}
    local = {(n, h): (scaled[n][0][:, hks[h]], intra[n, h], inc[n, h]) for n, h in pairs}
    return qt, local


def _gla_kernel(qf_ref, kf_ref, vf_ref, gf_ref, qb_ref, kb_ref, vb_ref, gb_ref, s0_ref,
                of_ref, ob_ref, st_out_ref, st_sc, *, nchunk, dkh, dvh):
    j = pl.program_id(1)

    @pl.when(j == 0)
    def _():
        st_sc[...] = s0_ref[0]

    c = GLA_CHUNK
    kw = dict(nchunk=nchunk, dkh=dkh, dvh=dvh)
    sides = ((_gla_local(qf_ref, kf_ref, vf_ref, gf_ref, rev=False, **kw), of_ref),
             (_gla_local(qb_ref, kb_ref, vb_ref, gb_ref, rev=True, **kw), ob_ref))
    for step in range(nchunk):
        for side, ((qt, local), o_ref) in enumerate(sides):
            n = step if side == 0 else nchunk - 1 - step
            for h in range(GLA_HEADS):
                hk, hv = slice(h * dkh, (h + 1) * dkh), slice(h * dvh, (h + 1) * dvh)
                dec, intra, inc = local[n, h]
                st = st_sc[side, h]
                o = _dot_nt(qt[n * c:(n + 1) * c, hk], st.astype(BF16)) + intra
                o_ref[0, n * c:(n + 1) * c, hv] = o.astype(BF16)
                st_sc[side, h] = dec * st + inc

    @pl.when(j == pl.num_programs(1) - 1)
    def _():
        st_out_ref[0] = st_sc[...]


def _gla_scan(q, k, v, g, s0):
    b, t, dk = q.shape
    dv = v.shape[2]
    dkh, dvh = dk // GLA_HEADS, dv // GLA_HEADS
    nchunk = min(GLA_R, t // GLA_CHUNK)
    rows = nchunk * GLA_CHUNK
    n = t // rows
    fwd = lambda w, cb=0: pl.BlockSpec((1, rows, w), lambda bi, j: (bi, j, cb))
    bwd = lambda w, cb=0: pl.BlockSpec((1, rows, w), lambda bi, j: (bi, n - 1 - j, cb))
    st_spec = pl.BlockSpec((1, 2, GLA_HEADS, dvh, dkh), lambda bi, j: (bi, 0, 0, 0, 0))
    kern = functools.partial(_gla_kernel, nchunk=nchunk, dkh=dkh, dvh=dvh)
    return pl.pallas_call(
        kern,
        grid=(b, n),
        in_specs=[fwd(dk), fwd(dk), fwd(dv), fwd(dk, 0), bwd(dk), bwd(dk), bwd(dv), bwd(dk, 1), st_spec],
        out_specs=[fwd(dv), bwd(dv), st_spec],
        out_shape=[jax.ShapeDtypeStruct((b, t, dv), BF16), jax.ShapeDtypeStruct((b, t, dv), BF16),
                   jax.ShapeDtypeStruct(s0.shape, F32)],
        scratch_shapes=[pltpu.VMEM((2, GLA_HEADS, dvh, dkh), F32)],
        compiler_params=_params(("parallel", "arbitrary")),
        name="gla_scan",
    )(q, k, v, g, q, k, v, g, s0)


def _post_kernel(*refs, d, f, gla, dvh):
    if gla:
        x_ref, of_ref, ob_ref, r_ref, on_ref, mod_ref, wo_ref, g2_ref, wgu_ref, wd_ref, o_ref = refs
        r = r_ref[...].astype(F32)
        gate = r * _sigmoid(r)
        parts = []
        for h in range(GLA_HEADS):
            cs = slice(h * dvh, (h + 1) * dvh)
            oh = of_ref[:, cs].astype(F32) + ob_ref[:, cs].astype(F32)
            ms = jnp.mean(oh * oh, axis=-1, keepdims=True)
            parts.append((oh * lax.rsqrt(ms + NORM_EPS) * on_ref[...] * gate[:, cs]).astype(BF16))
        a = jnp.concatenate(parts, axis=1)
    else:
        x_ref, a_ref, mod_ref, wo_ref, g2_ref, wgu_ref, wd_ref, o_ref = refs
        a = a_ref[...]
    m = mod_ref[0]
    gate1, shift2, scale2, gate2 = (m[:, 2 * d:3 * d], m[:, 3 * d:4 * d], m[:, 4 * d:5 * d], m[:, 5 * d:6 * d])
    x1 = x_ref[...] + gate1 * _dot(a, wo_ref[...])
    h2 = _norm_mod(x1, g2_ref[...], shift2, scale2).astype(BF16)
    acc = jnp.zeros(x1.shape, F32)
    for c0 in range(0, f, FFN_CHUNK):
        gt = _dot(h2, wgu_ref[:, c0:c0 + FFN_CHUNK])
        up = _dot(h2, wgu_ref[:, f + c0:f + c0 + FFN_CHUNK])
        act = (gt * _sigmoid(gt) * up).astype(BF16)
        acc = acc + _dot(act, wd_ref[c0:c0 + FFN_CHUNK, :])
    o_ref[...] = x1 + gate2 * acc


def _post(x2d, mix, modx, rows_per_mod, wo, g2, wgu, wd, *, gla_norm=None):
    n, d = x2d.shape
    f = wd.shape[0]
    tm = min(TM_POST, n)
    mod_blocks = rows_per_mod // tm
    row = lambda c: pl.BlockSpec((tm, c), lambda i: (i, 0))
    gla = gla_norm is not None
    if gla:
        ins = [x2d, *mix, gla_norm]
        specs = [row(d), row(d), row(d), row(d), _const_spec(gla_norm.shape)]
        dvh = gla_norm.shape[1]
    else:
        ins = [x2d, mix]
        specs = [row(d), row(d)]
        dvh = 0
    ins += [modx, wo, g2, wgu, wd]
    specs += [pl.BlockSpec((1, 1, 6 * d), lambda i: (i // mod_blocks, 0, 0)),
              _const_spec(wo.shape), _const_spec((1, d)), _const_spec(wgu.shape), _const_spec(wd.shape)]
    kern = functools.partial(_post_kernel, d=d, f=f, gla=gla, dvh=dvh)
    return pl.pallas_call(
        kern,
        grid=(n // tm,),
        in_specs=specs,
        out_specs=row(d),
        out_shape=jax.ShapeDtypeStruct((n, d), F32),
        compiler_params=_params(("parallel",)),
        name="post_gla" if gla else "post_attn",
    )(*ins)


def _rope_tables(n_tokens, head_dim):
    rows = n_tokens // GRID_W
    rowp = jnp.broadcast_to(jnp.arange(rows)[:, None], (rows, GRID_W)).reshape(-1)
    colp = jnp.broadcast_to(jnp.arange(GRID_W)[None, :], (rows, GRID_W)).reshape(-1)
    half = head_dim // 2
    quarter = head_dim // 4
    inv = ROPE_BASE ** (-jnp.arange(0, half, 2, dtype=F32) / half)

    def axis_angles(pos):
        a = pos.astype(F32)[:, None] * inv[None, :]
        return jnp.concatenate([a, a], axis=-1)

    ang = jnp.concatenate([axis_angles(rowp), axis_angles(colp)], axis=-1)
    reps = LANES // head_dim
    cos = jnp.tile(jnp.cos(ang), (1, reps))
    sin = jnp.tile(jnp.sin(ang), (1, reps))
    first = (jnp.arange(LANES) % (2 * quarter)) < quarter
    sa = jnp.where(first[None, :], -sin, 0.0)
    sb = jnp.where(first[None, :], 0.0, sin)
    return cos, sa, sb


def _seg_matrix(seg):
    idx = jnp.arange(MXU_N) // seg
    return jnp.where(idx[:, None] == idx[None, :], 1.0 / seg, 0.0).astype(BF16)


def kernel(x, c, ctx, c_ctx, ada_w, ada_b, norm1_g, norm2_g, ffn_w_gu, ffn_w_down, diff_w_in, diff_w_out, diff_q_norm, diff_k_norm, diff_lambda_q1, diff_lambda_k1, diff_lambda_q2, diff_lambda_k2, diff_subln, gqa_w_in, gqa_w_out, gqa_q_norm, gqa_k_norm, gla_w_in, gla_gate_w1_fwd, gla_gate_w2_fwd, gla_gate_b_fwd, gla_gate_w1_bwd, gla_gate_w2_bwd, gla_gate_b_bwd, gla_out_norm, gla_w_out):
    b, l, d = x.shape
    lc = ctx.shape[1]
    depth = ada_w.shape[0]
    nl, nc = b * l, b * lc

    cond_rows = ((b + 1 + 7) // 8) * 8
    cond = jnp.zeros((cond_rows, d), F32).at[:b].set(c).at[b].set(c_ctx)
    mod = _modulation(cond, ada_w, ada_b)

    xl = x.reshape(nl, d)
    xc = ctx.reshape(nc, d)
    ones_tab = (jnp.ones((lc, LANES), F32), jnp.zeros((lc, LANES), F32), jnp.zeros((lc, LANES), F32))
    dummy_lam = jnp.zeros((4, DIFF_DH), F32)
    dummy_sub = jnp.ones((1, LANES), F32)

    for i in range(depth):
        last = i == depth - 1
        kind, j = i % N_MIXERS, i // N_MIXERS
        modl = mod[i, :b].reshape(b, 1, 6 * d)
        modc = mod[i, b:b + 1].reshape(1, 1, 6 * d)
        g1 = norm1_g[i].reshape(1, d)
        g2 = norm2_g[i].reshape(1, d)
        wgu = ffn_w_gu[i].astype(BF16)
        wd = ffn_w_down[i].astype(BF16)

        if kind in (0, 1):
            if kind == 0:
                diff, dh = True, DIFF_DH
                w_in, w_out = diff_w_in[j].astype(BF16), diff_w_out[j].astype(BF16)
                gq = jnp.tile(diff_q_norm[j], LANES // dh).reshape(1, LANES) * (dh ** -0.5 * LOG2E)
                gk = jnp.tile(diff_k_norm[j], LANES // dh).reshape(1, LANES)
                nq = nk = nv = d // LANES
                lamp = jnp.stack([diff_lambda_q1[j], diff_lambda_k1[j], diff_lambda_q2[j], diff_lambda_k2[j]])
                subln = diff_subln[j].reshape(1, LANES)
                lambda_init = 0.8 - 0.6 * math.exp(-0.3 * i)
            else:
                diff, dh = False, GQA_DH
                w_in, w_out = gqa_w_in[j].astype(BF16), gqa_w_out[j].astype(BF16)
                gq = gqa_q_norm[j].reshape(1, LANES) * (dh ** -0.5 * LOG2E)
                gk = gqa_k_norm[j].reshape(1, LANES)
                nq = d // LANES
                nk = nv = (w_in.shape[1] - d) // (2 * LANES)
                lamp, subln, lambda_init = dummy_lam, dummy_sub, 0.0
            pmat = _seg_matrix(dh)
            cos, sa, sb = _rope_tables(l, dh)
            kw = dict(nq=nq, nk=nk, nv=nv, rot=dh // 4)
            ql, kl, vl = _pre_attn(xl, modl, l, g1, w_in, pmat, gq, gk, cos, sa, sb, l, **kw)
            qc, kc, vc = _pre_attn(xc, modc, nc, g1, w_in, pmat, gq, gk, *ones_tab, lc, **kw)
            r3 = lambda t, n: t.reshape(b, n, t.shape[1])
            kc3, vc3 = r3(kc, lc), r3(vc, lc)
            akw = dict(diff=diff, lambda_init=lambda_init)
            ol = _attention(r3(ql, l), kc3, vc3, r3(kl, l), r3(vl, l), lamp, subln, **akw).reshape(nl, d)
            xl = _post(xl, ol, modl, l, w_out, g2, wgu, wd)
            if not last:
                oc = _attention(r3(qc, lc), kc3, vc3, None, None, lamp, subln, **akw).reshape(nc, d)
                xc = _post(xc, oc, modc, nc, w_out, g2, wgu, wd)
        else:
            dk, dv = gla_gate_w2_fwd.shape[2], gla_w_out.shape[1]
            rank = gla_gate_w1_fwd.shape[2]
            w1 = jnp.zeros((d, MXU_N), F32).at[:, :rank].set(gla_gate_w1_fwd[j]).at[:, rank:2 * rank].set(gla_gate_w1_bwd[j])
            w_in = jnp.concatenate([gla_w_in[j], w1], axis=1).astype(BF16)
            w2 = (jnp.zeros((MXU_N, 2 * dk), F32).at[:rank, :dk].set(gla_gate_w2_fwd[j])
                  .at[rank:2 * rank, dk:].set(gla_gate_w2_bwd[j])).astype(BF16)
            gb = jnp.concatenate([gla_gate_b_fwd[j], gla_gate_b_bwd[j]]).reshape(1, 2 * dk)
            w_out = gla_w_out[j].astype(BF16)
            dkh, dvh = dk // GLA_HEADS, dv // GLA_HEADS
            onorm = gla_out_norm[j].reshape(1, dvh)
            kw = dict(dk=dk, dv=dv, qscale=dkh ** -0.5)
            ql, kl, vl, rl, gl = _pre_gla(xl, modl, l, g1, w_in, w2, gb, **kw)
            qc, kc, vc, rc, gc = _pre_gla(xc, modc, nc, g1, w_in, w2, gb, **kw)
            r3 = lambda t, n: t.reshape(b, n, t.shape[1])
            s0 = jnp.zeros((b, 2, GLA_HEADS, dvh, dkh), F32)
            ocf, ocb, sc = _gla_scan(r3(qc, lc), r3(kc, lc), r3(vc, lc), r3(gc, lc), s0)
            olf, olb, _ = _gla_scan(r3(ql, l), r3(kl, l), r3(vl, l), r3(gl, l), sc)
            xl = _post(xl, (olf.reshape(nl, dv), olb.reshape(nl, dv), rl), modl, l, w_out, g2, wgu, wd, gla_norm=onorm)
            if not last:
                xc = _post(xc, (ocf.reshape(nc, dv), ocb.reshape(nc, dv), rc), modc, nc, w_out, g2, wgu, wd, gla_norm=onorm)
    return xl.reshape(b, l, d)
```

```python
import functools
import math

import jax
import jax.numpy as jnp
from jax import lax
from jax.experimental import pallas as pl
from jax.experimental.pallas import tpu as pltpu

F32 = jnp.float32
BF16 = jnp.bfloat16

GRID_W = 64
N_MIXERS = 3
ROPE_BASE = 10000.0
NORM_EPS = 1e-6
DIFF_DH = 64
GQA_DH = 128
GQA_GROUP = 4
GLA_HEADS = 4
GLA_GATE_RANK = 16
GLA_TAU = 16.0
GLA_CHUNK = 64
GLA_SUB = 16
LOG2E = math.log2(math.e)
ATTN_MIN_DENOM = 1e-22

LANES = 128
MXU_N = 256
VMEM_LIMIT = 56 * 1024 * 1024

TM_PRE = 512
TM_POST = 512
FFN_CHUNK = 256
TQ_DIFF = 512
TQ_GQA = 256
TK_ATTN = 4608
GLA_R = 4


def _const_spec(shape):
    nd = len(shape)
    return pl.BlockSpec(shape, lambda *_: (0,) * nd, pipeline_mode=pl.Buffered(1))


def _params(sem, vmem=VMEM_LIMIT):
    return pltpu.CompilerParams(dimension_semantics=sem, vmem_limit_bytes=vmem)


def _sigmoid(x):
    return 1.0 / (1.0 + jnp.exp(-x))


def _norm_mod(x, g, shift, scale):
    ms = jnp.mean(x * x, axis=-1, keepdims=True)
    return (x * lax.rsqrt(ms + NORM_EPS) * g) * (1.0 + scale) + shift


def _dot(a, b):
    return jnp.dot(a, b, preferred_element_type=F32)


def _dot_nt(a, b):
    return lax.dot_general(a, b, (((1,), (1,)), ((), ())), preferred_element_type=F32)


def _dot_tn(a, b):
    return lax.dot_general(a, b, (((0,), (0,)), ((), ())), preferred_element_type=F32)


def _split3(x):
    hi = x.astype(BF16)
    r1 = x - hi.astype(F32)
    mid = r1.astype(BF16)
    lo = (r1 - mid.astype(F32)).astype(BF16)
    return hi, mid, lo


def _mod_kernel(cond_ref, w_ref, b_ref, o_ref):
    c = cond_ref[...]
    s = (c * _sigmoid(c)).astype(BF16)
    o_ref[0] = _dot(s, w_ref[0].astype(BF16)) + b_ref[0]


def _modulation(cond, ada_w, ada_b):
    depth, d, n = ada_w.shape
    rows = cond.shape[0]
    tn = 1536
    return pl.pallas_call(
        _mod_kernel,
        grid=(depth, n // tn),
        in_specs=[
            pl.BlockSpec((rows, d), lambda i, j: (0, 0)),
            pl.BlockSpec((1, d, tn), lambda i, j: (i, 0, j)),
            pl.BlockSpec((1, 1, tn), lambda i, j: (i, 0, j)),
        ],
        out_specs=pl.BlockSpec((1, rows, tn), lambda i, j: (i, 0, j)),
        out_shape=jax.ShapeDtypeStruct((depth, rows, n), F32),
        compiler_params=_params(("parallel", "parallel")),
        name="adaln_mod",
    )(cond, ada_w, ada_b.reshape(depth, 1, n))


def _pre_attn_kernel(x_ref, mod_ref, g1_ref, w_ref, p_ref, gq_ref, gk_ref, cos_ref, sa_ref, sb_ref, *rest,
                     d, nq, nk, nv, rot):
    q_ref, k_ref, v_ref = rest[-3], rest[-2].at[0], rest[-1].at[0]
    x = x_ref[...]
    m = mod_ref[0]
    h = _norm_mod(x, g1_ref[...], m[:, 0:d], m[:, d:2 * d]).astype(BF16)
    cos, sa, sb = cos_ref[...], sa_ref[...], sb_ref[...]
    per = MXU_N // LANES

    def project(grp):
        return _dot(h, w_ref[:, grp * MXU_N:(grp + 1) * MXU_N])

    def emit(grp, y):
        slab0 = grp * per
        if slab0 >= nq + nk:
            j = slab0 - nq - nk
            v_ref[:, j * LANES:(j + per) * LANES] = y.astype(BF16)
            return
        ms = _dot((y * y).astype(BF16), p_ref[...])
        yn = y * lax.rsqrt(ms + NORM_EPS)
        for t in range(per):
            slab = slab0 + t
            out_ref, j, gain = (q_ref, slab, gq_ref) if slab < nq else (k_ref, slab - nq, gk_ref)
            xn = yn[:, t * LANES:(t + 1) * LANES] * gain[...]
            r = xn * cos + pltpu.roll(xn, LANES - rot, 1) * sa + pltpu.roll(xn, rot, 1) * sb
            out_ref[:, j * LANES:(j + 1) * LANES] = r.astype(BF16)

    ngrp = (nq + nk + nv) // per
    y = project(0)
    for grp in range(ngrp):
        y_next = project(grp + 1) if grp + 1 < ngrp else None
        emit(grp, y)
        y = y_next


def _pre_attn(x2d, modx, rows_per_mod, g1, w, pmat, gq, gk, cos, sa, sb, seq, kv_rows, kv_row0, kv_prev,
              *, nq, nk, nv, rot):
    n, d = x2d.shape
    tm = min(TM_PRE, seq)
    seq_blocks = seq // tm
    mod_blocks = rows_per_mod // tm
    kv_block0 = kv_row0 // tm
    row = lambda c: pl.BlockSpec((tm, c), lambda i: (i, 0))
    tab = pl.BlockSpec((tm, LANES), lambda i: (i % seq_blocks, 0))
    kv = lambda c: pl.BlockSpec((1, tm, c), lambda i: (i // seq_blocks, kv_block0 + i % seq_blocks, 0))
    ins = [x2d, modx, g1, w, pmat, gq, gk, cos, sa, sb]
    specs = [
        row(d),
        pl.BlockSpec((1, 1, 6 * d), lambda i: (i // mod_blocks, 0, 0)),
        _const_spec((1, d)),
        _const_spec(w.shape),
        _const_spec(pmat.shape),
        _const_spec((1, LANES)),
        _const_spec((1, LANES)),
        tab, tab, tab,
    ]
    aliases = {}
    if kv_prev is not None:
        aliases = {len(ins): 1, len(ins) + 1: 2}
        ins += list(kv_prev)
        specs += [pl.BlockSpec(memory_space=pl.ANY)] * 2
    batches = n // seq
    kern = functools.partial(_pre_attn_kernel, d=d, nq=nq, nk=nk, nv=nv, rot=rot)
    return pl.pallas_call(
        kern,
        grid=(n // tm,),
        in_specs=specs,
        out_specs=[row(nq * LANES), kv(nk * LANES), kv(nv * LANES)],
        out_shape=[jax.ShapeDtypeStruct((n, nq * LANES), BF16),
                   jax.ShapeDtypeStruct((batches, kv_rows, nk * LANES), BF16),
                   jax.ShapeDtypeStruct((batches, kv_rows, nv * LANES), BF16)],
        input_output_aliases=aliases,
        compiler_params=_params(("parallel",)),
        name="pre_attn",
    )(*ins)


def _pre_gla_kernel(x_ref, mod_ref, g1_ref, w_ref, w2_ref, gb_ref,
                    q_ref, k_ref, v_ref, r_ref, g_ref, *, d, dk, dv, qscale):
    x = x_ref[...]
    m = mod_ref[0]
    h = _norm_mod(x, g1_ref[...], m[:, 0:d], m[:, d:2 * d]).astype(BF16)
    bounds = (dk, 2 * dk, 2 * dk + dv, 2 * dk + 2 * dv)
    ngrp = w_ref.shape[1] // MXU_N
    for grp in range(ngrp):
        c0 = grp * MXU_N
        y = _dot(h, w_ref[:, c0:c0 + MXU_N])
        if c0 < bounds[0]:
            q_ref[:, c0:c0 + MXU_N] = (y * qscale).astype(BF16)
        elif c0 < bounds[1]:
            k_ref[:, c0 - bounds[0]:c0 - bounds[0] + MXU_N] = y.astype(BF16)
        elif c0 < bounds[2]:
            v_ref[:, c0 - bounds[1]:c0 - bounds[1] + MXU_N] = y.astype(BF16)
        elif c0 < bounds[3]:
            r_ref[:, c0 - bounds[2]:c0 - bounds[2] + MXU_N] = y.astype(BF16)
        else:
            z = _dot(y.astype(BF16), w2_ref[...]) + gb_ref[...]
            logsig = jnp.minimum(z, 0.0) - jnp.log(1.0 + jnp.exp(-jnp.abs(z)))
            g_ref[...] = logsig * (1.0 / GLA_TAU)


def _pre_gla(x2d, modx, rows_per_mod, g1, w, w2, gb, *, dk, dv, qscale):
    n, d = x2d.shape
    tm = min(TM_PRE, n)
    mod_blocks = rows_per_mod // tm
    row = lambda c: pl.BlockSpec((tm, c), lambda i: (i, 0))
    kern = functools.partial(_pre_gla_kernel, d=d, dk=dk, dv=dv, qscale=qscale)
    return pl.pallas_call(
        kern,
        grid=(n // tm,),
        in_specs=[
            row(d),
            pl.BlockSpec((1, 1, 6 * d), lambda i: (i // mod_blocks, 0, 0)),
            _const_spec((1, d)),
            _const_spec(w.shape),
            _const_spec(w2.shape),
            _const_spec((1, 2 * dk)),
        ],
        out_specs=[row(dk), row(dk), row(dv), row(dv), row(2 * dk)],
        out_shape=[jax.ShapeDtypeStruct((n, c), t) for c, t in
                   ((dk, BF16), (dk, BF16), (dv, BF16), (dv, BF16), (2 * dk, F32))],
        compiler_params=_params(("parallel",)),
        name="pre_gla",
    )(x2d, modx, g1, w, w2, gb)


def _attn_kernel(q_ref, k_ref, v_ref, pn_ref, lam_ref, sub_ref, o_ref, kn_sc, m_sc, l1_sc, l_sc, acc_sc,
                 *, diff, tq, tk, n_steps, lambda_init):
    lane = lax.broadcasted_iota(jnp.int32, (1, LANES), 1)
    lo_lane = lane < DIFF_DH

    @pl.when(pl.program_id(2) == 0)
    def _():
        pn = pn_ref[...]

        def seg_norm2_max(kb):
            kf = kb.astype(F32)
            return jnp.max(_dot((kf * kf).astype(BF16), pn), axis=0, keepdims=True)

        r = seg_norm2_max(k_ref[0, 0:tk, :])
        for i in range(1, n_steps):
            r = jnp.maximum(r, seg_norm2_max(k_ref[0, i * tk:(i + 1) * tk, :]))
        if diff:
            kn_sc[0:1, :] = jnp.broadcast_to(jnp.max(jnp.where(lo_lane, r, 0.0), axis=1, keepdims=True), (1, LANES))
            kn_sc[1:2, :] = jnp.broadcast_to(jnp.max(jnp.where(lo_lane, 0.0, r), axis=1, keepdims=True), (1, LANES))
        else:
            kn_sc[0:1, :] = r

    q = q_ref[0]
    if diff:
        lo_mask = jnp.where(lo_lane, 1.0, 0.0).astype(BF16)
        qs = jnp.concatenate([q * lo_mask, q * (1.0 - lo_mask)], axis=0)
        kn = jnp.concatenate([jnp.broadcast_to(kn_sc[0:1, :], (tq, LANES)),
                              jnp.broadcast_to(kn_sc[1:2, :], (tq, LANES))], axis=0)
    else:
        qs = jnp.concatenate([q[:, g * LANES:(g + 1) * LANES] for g in range(GQA_GROUP)], axis=0)
        kn = kn_sc[0:1, :]
    qf = qs.astype(F32)
    qn2 = _dot((qf * qf).astype(BF16), jnp.ones((LANES, LANES), BF16))
    shift = jnp.sqrt(qn2 * kn)

    def fast_step(kb, vb, first):
        s = _dot_nt(qs, kb)
        psum, ps = None, []
        for c in range(kb.shape[0] // LANES):
            pc = jnp.exp2(s[:, c * LANES:(c + 1) * LANES] - shift)
            psum = pc if psum is None else psum + pc
            ps.append(pc.astype(BF16))
        pv = _dot(jnp.concatenate(ps, axis=1), vb)
        if first:
            l_sc[...] = psum
            acc_sc[...] = pv
        else:
            l_sc[...] += psum
            acc_sc[...] += pv

    def exact_step(kb, vb, first):
        s = _dot_nt(qs, kb)
        m_cur = jnp.max(s, axis=-1, keepdims=True)
        if first:
            m_new = m_cur
            p = jnp.exp2(s - m_new)
            l1_sc[...] = jnp.sum(p, axis=-1, keepdims=True)
            acc_sc[...] = _dot(p.astype(BF16), vb)
        else:
            m_old = m_sc[...]
            m_new = jnp.maximum(m_old, m_cur)
            alpha = jnp.exp2(m_old - m_new)
            p = jnp.exp2(s - m_new)
            l1_sc[...] = alpha * l1_sc[...] + jnp.sum(p, axis=-1, keepdims=True)
            acc_sc[...] = alpha * acc_sc[...] + _dot(p.astype(BF16), vb)
        m_sc[...] = m_new

    def sweep(step):
        for i in range(n_steps):
            step(k_ref[0, i * tk:(i + 1) * tk, :], v_ref[0, i * tk:(i + 1) * tk, :], i == 0)

    def finish(o):
        if diff:
            lp = lam_ref[...]
            lam = (jnp.exp(jnp.sum(lp[0:1] * lp[1:2], axis=-1, keepdims=True))
                   - jnp.exp(jnp.sum(lp[2:3] * lp[3:4], axis=-1, keepdims=True)) + lambda_init)
            a = o[:tq] - lam * o[tq:]
            ms = jnp.mean(a * a, axis=-1, keepdims=True)
            a = a * lax.rsqrt(ms + NORM_EPS) * sub_ref[...] * (1.0 - lambda_init)
            o_ref[0] = a.astype(BF16)
        else:
            for g in range(GQA_GROUP):
                o_ref[0, :, g * LANES:(g + 1) * LANES] = o[g * tq:(g + 1) * tq].astype(BF16)

    sweep(fast_step)
    denom = jnp.sum(l_sc[...], axis=-1, keepdims=True)
    finish(acc_sc[...] / denom)

    @pl.when(jnp.logical_not(jnp.min(denom) >= ATTN_MIN_DENOM))
    def _():
        sweep(exact_step)
        finish(acc_sc[...] / l1_sc[...])


def _attention(q, k, v, lamp, subln, *, diff, lambda_init, kv_rows, kv_block):
    b, t, _ = q.shape
    tq = min(TQ_DIFF if diff else TQ_GQA, t)
    qw = LANES if diff else GQA_GROUP * LANES
    heads = k.shape[2] // LANES
    stack = (2 if diff else GQA_GROUP) * tq
    tk = max(c for c in range(LANES, min(TK_ATTN, kv_rows) + 1, LANES) if kv_rows % c == 0)
    kv_spec = pl.BlockSpec((1, kv_rows, LANES), lambda bi, h, i: (bi, kv_block, h))
    seg = DIFF_DH if diff else LANES
    idx = jnp.arange(LANES) // seg
    pn = jnp.where(idx[:, None] == idx[None, :], 1.0, 0.0).astype(BF16)
    ins = [q, k, v, pn, lamp, subln]
    specs = [pl.BlockSpec((1, tq, qw), lambda bi, h, i: (bi, i, h)), kv_spec, kv_spec]
    specs += [pl.BlockSpec(a.shape, lambda bi, h, i: (0, 0)) for a in (pn, lamp, subln)]
    kern = functools.partial(_attn_kernel, diff=diff, tq=tq, tk=tk, n_steps=kv_rows // tk, lambda_init=lambda_init)
    return pl.pallas_call(
        kern,
        grid=(b, heads, t // tq),
        in_specs=specs,
        out_specs=pl.BlockSpec((1, tq, qw), lambda bi, h, i: (bi, i, h)),
        out_shape=jax.ShapeDtypeStruct((b, t, heads * qw), BF16),
        scratch_shapes=[pltpu.VMEM((8, LANES), F32), pltpu.VMEM((stack, 1), F32), pltpu.VMEM((stack, 1), F32),
                        pltpu.VMEM((stack, LANES), F32), pltpu.VMEM((stack, LANES), F32)],
        compiler_params=_params(("parallel", "parallel", "arbitrary")),
        name="attn_diff" if diff else "attn_gqa",
    )(*ins)


def _gla_local(q_ref, k_ref, v_ref, g_ref, *, nchunk, dkh, dvh, rev):
    c = GLA_CHUNK
    rows = nchunk * c
    ri = lax.broadcasted_iota(jnp.int32, (rows, rows), 0)
    ci = lax.broadcasted_iota(jnp.int32, (rows, rows), 1)
    causal = (ci >= ri) if rev else (ci <= ri)
    tri = jnp.where(jnp.logical_and(ri // c == ci // c, causal), 1.0, 0.0).astype(BF16)
    hi, mid, lo = _split3(g_ref[0])
    bsum = _dot(tri, hi) + _dot(tri, mid) + _dot(tri, lo)
    qf, kf = q_ref[0].astype(F32), k_ref[0].astype(F32)
    qt = (qf * jnp.exp(bsum)).astype(BF16)

    krow = lax.broadcasted_iota(jnp.int32, (c, bsum.shape[1]), 0)
    r64 = lax.broadcasted_iota(jnp.int32, (c, c), 0)
    c64 = lax.broadcasted_iota(jnp.int32, (c, c), 1)
    keep = (c64 >= r64) if rev else (c64 <= r64)
    nsub = c // GLA_SUB
    scaled = []
    for n in range(nchunk):
        bs, kc, qc = bsum[n * c:(n + 1) * c], kf[n * c:(n + 1) * c], qf[n * c:(n + 1) * c]
        b_tot = bs[0:1] if rev else bs[c - 1:c]
        kd = (kc * jnp.exp(b_tot - bs)).astype(BF16)
        qis, kis = [], []
        for i in range(nsub):
            lo_r, hi_r = i * GLA_SUB, (i + 1) * GLA_SUB
            if rev:
                ref = bs[hi_r:hi_r + 1] if i < nsub - 1 else jnp.zeros_like(b_tot)
                valid = krow >= lo_r
            else:
                ref = bs[lo_r - 1:lo_r] if i > 0 else jnp.zeros_like(b_tot)
                valid = krow < hi_r
            kis.append((kc * jnp.exp(jnp.where(valid, ref - bs, 0.0))).astype(BF16))
            qis.append((qc[lo_r:hi_r] * jnp.exp(bs[lo_r:hi_r] - ref)).astype(BF16))
        scaled.append((jnp.exp(b_tot), kd, qis, kis))

    pairs = [(n, h) for n in range(nchunk) for h in range(GLA_HEADS)]
    hks = [slice(h * dkh, (h + 1) * dkh) for h in range(GLA_HEADS)]
    vals = {(n, h): v_ref[0, n * c:(n + 1) * c, h * dvh:(h + 1) * dvh] for n, h in pairs}
    blocks = {(n, h): [_dot_nt(scaled[n][2][i][:, hks[h]], scaled[n][3][i][:, hks[h]]) for i in range(nsub)]
              for n, h in pairs}
    inc = {(n, h): _dot_tn(vals[n, h], scaled[n][1][:, hks[h]]) for n, h in pairs}
    amat = {p: jnp.where(keep, jnp.concatenate(blocks[p], axis=0), 0.0).astype(BF16) for p in pairs}
    intra = {p: _dot(amat[p], vals[p]) for p in pairs}
    local = {(n, h): (scaled[n][0][:, hks[h]], intra[n, h], inc[n, h]) for n, h in pairs}
    return qt, local


def _gla_kernel(qf_ref, kf_ref, vf_ref, gf_ref, qb_ref, kb_ref, vb_ref, gb_ref, s0_ref,
                of_ref, ob_ref, st_out_ref, st_sc, *, nchunk, dkh, dvh):
    j = pl.program_id(1)

    @pl.when(j == 0)
    def _():
        st_sc[...] = s0_ref[0]

    c = GLA_CHUNK
    kw = dict(nchunk=nchunk, dkh=dkh, dvh=dvh)
    sides = ((_gla_local(qf_ref, kf_ref, vf_ref, gf_ref, rev=False, **kw), of_ref),
             (_gla_local(qb_ref, kb_ref, vb_ref, gb_ref, rev=True, **kw), ob_ref))
    for step in range(nchunk):
        for side, ((qt, local), o_ref) in enumerate(sides):
            n = step if side == 0 else nchunk - 1 - step
            for h in range(GLA_HEADS):
                hk, hv = slice(h * dkh, (h + 1) * dkh), slice(h * dvh, (h + 1) * dvh)
                dec, intra, inc = local[n, h]
                st = st_sc[side, h]
                o = _dot_nt(qt[n * c:(n + 1) * c, hk], st.astype(BF16)) + intra
                o_ref[0, n * c:(n + 1) * c, hv] = o.astype(BF16)
                st_sc[side, h] = dec * st + inc

    @pl.when(j == pl.num_programs(1) - 1)
    def _():
        st_out_ref[0] = st_sc[...]


def _gla_scan(q, k, v, g, s0):
    b, t, dk = q.shape
    dv = v.shape[2]
    dkh, dvh = dk // GLA_HEADS, dv // GLA_HEADS
    nchunk = min(GLA_R, t // GLA_CHUNK)
    rows = nchunk * GLA_CHUNK
    n = t // rows
    fwd = lambda w, cb=0: pl.BlockSpec((1, rows, w), lambda bi, j: (bi, j, cb))
    bwd = lambda w, cb=0: pl.BlockSpec((1, rows, w), lambda bi, j: (bi, n - 1 - j, cb))
    st_spec = pl.BlockSpec((1, 2, GLA_HEADS, dvh, dkh), lambda bi, j: (bi, 0, 0, 0, 0))
    kern = functools.partial(_gla_kernel, nchunk=nchunk, dkh=dkh, dvh=dvh)
    return pl.pallas_call(
        kern,
        grid=(b, n),
        in_specs=[fwd(dk), fwd(dk), fwd(dv), fwd(dk, 0), bwd(dk), bwd(dk), bwd(dv), bwd(dk, 1), st_spec],
        out_specs=[fwd(dv), bwd(dv), st_spec],
        out_shape=[jax.ShapeDtypeStruct((b, t, dv), BF16), jax.ShapeDtypeStruct((b, t, dv), BF16),
                   jax.ShapeDtypeStruct(s0.shape, F32)],
        scratch_shapes=[pltpu.VMEM((2, GLA_HEADS, dvh, dkh), F32)],
        compiler_params=_params(("parallel", "arbitrary")),
        name="gla_scan",
    )(q, k, v, g, q, k, v, g, s0)


def _post_kernel(*refs, d, f, gla, dvh):
    if gla:
        x_ref, of_ref, ob_ref, r_ref, on_ref, mod_ref, wo_ref, g2_ref, wgu_ref, wd_ref, o_ref = refs
        r = r_ref[...].astype(F32)
        gate = r * _sigmoid(r)
        parts = []
        for h in range(GLA_HEADS):
            cs = slice(h * dvh, (h + 1) * dvh)
            oh = of_ref[:, cs].astype(F32) + ob_ref[:, cs].astype(F32)
            ms = jnp.mean(oh * oh, axis=-1, keepdims=True)
            parts.append((oh * lax.rsqrt(ms + NORM_EPS) * on_ref[...] * gate[:, cs]).astype(BF16))
        a = jnp.concatenate(parts, axis=1)
    else:
        x_ref, a_ref, mod_ref, wo_ref, g2_ref, wgu_ref, wd_ref, o_ref = refs
        a = a_ref[...]
    m = mod_ref[0]
    gate1, shift2, scale2, gate2 = (m[:, 2 * d:3 * d], m[:, 3 * d:4 * d], m[:, 4 * d:5 * d], m[:, 5 * d:6 * d])
    x1 = x_ref[...] + gate1 * _dot(a, wo_ref[...])
    h2 = _norm_mod(x1, g2_ref[...], shift2, scale2).astype(BF16)
    acc = jnp.zeros(x1.shape, F32)
    for c0 in range(0, f, FFN_CHUNK):
        gt = _dot(h2, wgu_ref[:, c0:c0 + FFN_CHUNK])
        up = _dot(h2, wgu_ref[:, f + c0:f + c0 + FFN_CHUNK])
        act = (gt * _sigmoid(gt) * up).astype(BF16)
        acc = acc + _dot(act, wd_ref[c0:c0 + FFN_CHUNK, :])
    o_ref[...] = x1 + gate2 * acc


def _post(x2d, mix, modx, rows_per_mod, wo, g2, wgu, wd, *, gla_norm=None):
    n, d = x2d.shape
    f = wd.shape[0]
    tm = min(TM_POST, n)
    mod_blocks = rows_per_mod // tm
    row = lambda c: pl.BlockSpec((tm, c), lambda i: (i, 0))
    gla = gla_norm is not None
    if gla:
        ins = [x2d, *mix, gla_norm]
        specs = [row(d), row(d), row(d), row(d), _const_spec(gla_norm.shape)]
        dvh = gla_norm.shape[1]
    else:
        ins = [x2d, mix]
        specs = [row(d), row(d)]
        dvh = 0
    ins += [modx, wo, g2, wgu, wd]
    specs += [pl.BlockSpec((1, 1, 6 * d), lambda i: (i // mod_blocks, 0, 0)),
              _const_spec(wo.shape), _const_spec((1, d)), _const_spec(wgu.shape), _const_spec(wd.shape)]
    kern = functools.partial(_post_kernel, d=d, f=f, gla=gla, dvh=dvh)
    return pl.pallas_call(
        kern,
        grid=(n // tm,),
        in_specs=specs,
        out_specs=row(d),
        out_shape=jax.ShapeDtypeStruct((n, d), F32),
        compiler_params=_params(("parallel",)),
        name="post_gla" if gla else "post_attn",
    )(*ins)


def _rope_tables(n_tokens, head_dim):
    rows = n_tokens // GRID_W
    rowp = jnp.broadcast_to(jnp.arange(rows)[:, None], (rows, GRID_W)).reshape(-1)
    colp = jnp.broadcast_to(jnp.arange(GRID_W)[None, :], (rows, GRID_W)).reshape(-1)
    half = head_dim // 2
    quarter = head_dim // 4
    inv = ROPE_BASE ** (-jnp.arange(0, half, 2, dtype=F32) / half)

    def axis_angles(pos):
        a = pos.astype(F32)[:, None] * inv[None, :]
        return jnp.concatenate([a, a], axis=-1)

    ang = jnp.concatenate([axis_angles(rowp), axis_angles(colp)], axis=-1)
    reps = LANES // head_dim
    cos = jnp.tile(jnp.cos(ang), (1, reps))
    sin = jnp.tile(jnp.sin(ang), (1, reps))
    first = (jnp.arange(LANES) % (2 * quarter)) < quarter
    sa = jnp.where(first[None, :], -sin, 0.0)
    sb = jnp.where(first[None, :], 0.0, sin)
    return cos, sa, sb


def _seg_matrix(seg):
    idx = jnp.arange(MXU_N) // seg
    return jnp.where(idx[:, None] == idx[None, :], 1.0 / seg, 0.0).astype(BF16)


def kernel(x, c, ctx, c_ctx, ada_w, ada_b, norm1_g, norm2_g, ffn_w_gu, ffn_w_down, diff_w_in, diff_w_out, diff_q_norm, diff_k_norm, diff_lambda_q1, diff_lambda_k1, diff_lambda_q2, diff_lambda_k2, diff_subln, gqa_w_in, gqa_w_out, gqa_q_norm, gqa_k_norm, gla_w_in, gla_gate_w1_fwd, gla_gate_w2_fwd, gla_gate_b_fwd, gla_gate_w1_bwd, gla_gate_w2_bwd, gla_gate_b_bwd, gla_out_norm, gla_w_out):
    b, l, d = x.shape
    lc = ctx.shape[1]
    depth = ada_w.shape[0]
    nl, nc = b * l, b * lc

    cond_rows = ((b + 1 + 7) // 8) * 8
    cond = jnp.zeros((cond_rows, d), F32).at[:b].set(c).at[b].set(c_ctx)
    mod = _modulation(cond, ada_w, ada_b)

    xl = x.reshape(nl, d)
    xc = ctx.reshape(nc, d)
    ones_tab = (jnp.ones((lc, LANES), F32), jnp.zeros((lc, LANES), F32), jnp.zeros((lc, LANES), F32))
    dummy_lam = jnp.zeros((4, DIFF_DH), F32)
    dummy_sub = jnp.ones((1, LANES), F32)

    for i in range(depth):
        last = i == depth - 1
        kind, j = i % N_MIXERS, i // N_MIXERS
        modl = mod[i, :b].reshape(b, 1, 6 * d)
        modc = mod[i, b:b + 1].reshape(1, 1, 6 * d)
        g1 = norm1_g[i].reshape(1, d)
        g2 = norm2_g[i].reshape(1, d)
        wgu = ffn_w_gu[i].astype(BF16)
        wd = ffn_w_down[i].astype(BF16)

        if kind in (0, 1):
            if kind == 0:
                diff, dh = True, DIFF_DH
                w_in, w_out = diff_w_in[j].astype(BF16), diff_w_out[j].astype(BF16)
                gq = jnp.tile(diff_q_norm[j], LANES // dh).reshape(1, LANES) * (dh ** -0.5 * LOG2E)
                gk = jnp.tile(diff_k_norm[j], LANES // dh).reshape(1, LANES)
                nq = nk = nv = d // LANES
                lamp = jnp.stack([diff_lambda_q1[j], diff_lambda_k1[j], diff_lambda_q2[j], diff_lambda_k2[j]])
                subln = diff_subln[j].reshape(1, LANES)
                lambda_init = 0.8 - 0.6 * math.exp(-0.3 * i)
            else:
                diff, dh = False, GQA_DH
                w_in, w_out = gqa_w_in[j].astype(BF16), gqa_w_out[j].astype(BF16)
                gq = gqa_q_norm[j].reshape(1, LANES) * (dh ** -0.5 * LOG2E)
                gk = gqa_k_norm[j].reshape(1, LANES)
                nq = d // LANES
                nk = nv = (w_in.shape[1] - d) // (2 * LANES)
                lamp, subln, lambda_init = dummy_lam, dummy_sub, 0.0
            pmat = _seg_matrix(dh)
            cos, sa, sb = _rope_tables(l, dh)
            kw = dict(nq=nq, nk=nk, nv=nv, rot=dh // 4)
            ql, k_all, v_all = _pre_attn(xl, modl, l, g1, w_in, pmat, gq, gk, cos, sa, sb, l, l + lc, 0, None, **kw)
            qc, k_all, v_all = _pre_attn(xc, modc, nc, g1, w_in, pmat, gq, gk, *ones_tab, lc, l + lc, l,
                                         (k_all, v_all), **kw)
            r3 = lambda t, n: t.reshape(b, n, t.shape[1])
            akw = dict(diff=diff, lambda_init=lambda_init)
            ol = _attention(r3(ql, l), k_all, v_all, lamp, subln, kv_rows=l + lc, kv_block=0, **akw).reshape(nl, d)
            xl = _post(xl, ol, modl, l, w_out, g2, wgu, wd)
            if not last:
                oc = _attention(r3(qc, lc), k_all, v_all, lamp, subln, kv_rows=lc, kv_block=l // lc, **akw)
                xc = _post(xc, oc.reshape(nc, d), modc, nc, w_out, g2, wgu, wd)
        else:
            dk, dv = gla_gate_w2_fwd.shape[2], gla_w_out.shape[1]
            rank = gla_gate_w1_fwd.shape[2]
            w1 = jnp.zeros((d, MXU_N), F32).at[:, :rank].set(gla_gate_w1_fwd[j]).at[:, rank:2 * rank].set(gla_gate_w1_bwd[j])
            w_in = jnp.concatenate([gla_w_in[j], w1], axis=1).astype(BF16)
            w2 = (jnp.zeros((MXU_N, 2 * dk), F32).at[:rank, :dk].set(gla_gate_w2_fwd[j])
                  .at[rank:2 * rank, dk:].set(gla_gate_w2_bwd[j])).astype(BF16)
            gb = jnp.concatenate([gla_gate_b_fwd[j], gla_gate_b_bwd[j]]).reshape(1, 2 * dk)
            w_out = gla_w_out[j].astype(BF16)
            dkh, dvh = dk // GLA_HEADS, dv // GLA_HEADS
            onorm = gla_out_norm[j].reshape(1, dvh)
            kw = dict(dk=dk, dv=dv, qscale=dkh ** -0.5)
            ql, kl, vl, rl, gl = _pre_gla(xl, modl, l, g1, w_in, w2, gb, **kw)
            qc, kc, vc, rc, gc = _pre_gla(xc, modc, nc, g1, w_in, w2, gb, **kw)
            r3 = lambda t, n: t.reshape(b, n, t.shape[1])
            s0 = jnp.zeros((b, 2, GLA_HEADS, dvh, dkh), F32)
            ocf, ocb, sc = _gla_scan(r3(qc, lc), r3(kc, lc), r3(vc, lc), r3(gc, lc), s0)
            olf, olb, _ = _gla_scan(r3(ql, l), r3(kl, l), r3(vl, l), r3(gl, l), sc)
            xl = _post(xl, (olf.reshape(nl, dv), olb.reshape(nl, dv), rl), modl, l, w_out, g2, wgu, wd, gla_norm=onorm)
            if not last:
                xc = _post(xc, (ocf.reshape(nc, dv), ocb.reshape(nc, dv), rc), modc, nc, w_out, g2, wgu, wd, gla_norm=onorm)
    return xl.reshape(b, l, d)
```

```python
import functools
import math

import jax
import jax.numpy as jnp
from jax import lax
from jax.experimental import pallas as pl
from jax.experimental.pallas import tpu as pltpu

F32 = jnp.float32
BF16 = jnp.bfloat16

GRID_W = 64
N_MIXERS = 3
ROPE_BASE = 10000.0
NORM_EPS = 1e-6
DIFF_DH = 64
GQA_DH = 128
GQA_GROUP = 4
GLA_HEADS = 4
GLA_GATE_RANK = 16
GLA_TAU = 16.0
GLA_CHUNK = 64
GLA_SUB = 16
LOG2E = math.log2(math.e)
ATTN_MIN_DENOM = 1e-22

LANES = 128
MXU_N = 256
VMEM_LIMIT = 56 * 1024 * 1024

TM_PRE = 512
TM_POST = 512
FFN_CHUNK = 256
TQ_DIFF = 512
TQ_GQA = 256
TK_ATTN = 4608
GLA_R = 4


def _const_spec(shape):
    nd = len(shape)
    return pl.BlockSpec(shape, lambda *_: (0,) * nd, pipeline_mode=pl.Buffered(1))


def _params(sem, vmem=VMEM_LIMIT):
    return pltpu.CompilerParams(dimension_semantics=sem, vmem_limit_bytes=vmem)


def _sigmoid(x):
    return 1.0 / (1.0 + jnp.exp(-x))


def _norm_mod(x, g, shift, scale):
    ms = jnp.mean(x * x, axis=-1, keepdims=True)
    return (x * lax.rsqrt(ms + NORM_EPS) * g) * (1.0 + scale) + shift


def _dot(a, b):
    return jnp.dot(a, b, preferred_element_type=F32)


def _dot_nt(a, b):
    return lax.dot_general(a, b, (((1,), (1,)), ((), ())), preferred_element_type=F32)


def _dot_tn(a, b):
    return lax.dot_general(a, b, (((0,), (0,)), ((), ())), preferred_element_type=F32)


def _split3(x):
    hi = x.astype(BF16)
    r1 = x - hi.astype(F32)
    mid = r1.astype(BF16)
    lo = (r1 - mid.astype(F32)).astype(BF16)
    return hi, mid, lo


def _mod_kernel(cond_ref, w_ref, b_ref, o_ref):
    c = cond_ref[...]
    s = (c * _sigmoid(c)).astype(BF16)
    o_ref[0] = _dot(s, w_ref[0].astype(BF16)) + b_ref[0]


def _modulation(cond, ada_w, ada_b):
    depth, d, n = ada_w.shape
    rows = cond.shape[0]
    tn = 1536
    return pl.pallas_call(
        _mod_kernel,
        grid=(depth, n // tn),
        in_specs=[
            pl.BlockSpec((rows, d), lambda i, j: (0, 0)),
            pl.BlockSpec((1, d, tn), lambda i, j: (i, 0, j)),
            pl.BlockSpec((1, 1, tn), lambda i, j: (i, 0, j)),
        ],
        out_specs=pl.BlockSpec((1, rows, tn), lambda i, j: (i, 0, j)),
        out_shape=jax.ShapeDtypeStruct((depth, rows, n), F32),
        compiler_params=_params(("parallel", "parallel")),
        name="adaln_mod",
    )(cond, ada_w, ada_b.reshape(depth, 1, n))


def _pre_attn_kernel(x_ref, mod_ref, g1_ref, w_ref, p_ref, gq_ref, gk_ref, cos_ref, sa_ref, sb_ref, *rest,
                     d, nq, nk, nv, rot):
    q_ref, k_ref, v_ref = rest[-3], rest[-2].at[0], rest[-1].at[0]
    x = x_ref[...]
    m = mod_ref[0]
    h = _norm_mod(x, g1_ref[...], m[:, 0:d], m[:, d:2 * d]).astype(BF16)
    cos, sa, sb = cos_ref[...], sa_ref[...], sb_ref[...]
    per = MXU_N // LANES

    def project(grp):
        return _dot(h, w_ref[:, grp * MXU_N:(grp + 1) * MXU_N])

    def emit(grp, y):
        slab0 = grp * per
        if slab0 >= nq + nk:
            j = slab0 - nq - nk
            v_ref[:, j * LANES:(j + per) * LANES] = y.astype(BF16)
            return
        ms = _dot((y * y).astype(BF16), p_ref[...])
        yn = y * lax.rsqrt(ms + NORM_EPS)
        for t in range(per):
            slab = slab0 + t
            out_ref, j, gain = (q_ref, slab, gq_ref) if slab < nq else (k_ref, slab - nq, gk_ref)
            xn = yn[:, t * LANES:(t + 1) * LANES] * gain[...]
            r = xn * cos + pltpu.roll(xn, LANES - rot, 1) * sa + pltpu.roll(xn, rot, 1) * sb
            out_ref[:, j * LANES:(j + 1) * LANES] = r.astype(BF16)

    ngrp = (nq + nk + nv) // per
    y = project(0)
    for grp in range(ngrp):
        y_next = project(grp + 1) if grp + 1 < ngrp else None
        emit(grp, y)
        y = y_next


def _pre_attn(x2d, modx, rows_per_mod, g1, w, pmat, gq, gk, cos, sa, sb, seq, kv_rows, kv_row0, kv_prev,
              *, nq, nk, nv, rot):
    n, d = x2d.shape
    tm = min(TM_PRE, seq)
    seq_blocks = seq // tm
    mod_blocks = rows_per_mod // tm
    kv_block0 = kv_row0 // tm
    row = lambda c: pl.BlockSpec((tm, c), lambda i: (i, 0))
    tab = pl.BlockSpec((tm, LANES), lambda i: (i % seq_blocks, 0))
    kv = lambda c: pl.BlockSpec((1, tm, c), lambda i: (i // seq_blocks, kv_block0 + i % seq_blocks, 0))
    ins = [x2d, modx, g1, w, pmat, gq, gk, cos, sa, sb]
    specs = [
        row(d),
        pl.BlockSpec((1, 1, 6 * d), lambda i: (i // mod_blocks, 0, 0)),
        _const_spec((1, d)),
        _const_spec(w.shape),
        _const_spec(pmat.shape),
        _const_spec((1, LANES)),
        _const_spec((1, LANES)),
        tab, tab, tab,
    ]
    aliases = {}
    if kv_prev is not None:
        aliases = {len(ins): 1, len(ins) + 1: 2}
        ins += list(kv_prev)
        specs += [pl.BlockSpec(memory_space=pl.ANY)] * 2
    batches = n // seq
    kern = functools.partial(_pre_attn_kernel, d=d, nq=nq, nk=nk, nv=nv, rot=rot)
    return pl.pallas_call(
        kern,
        grid=(n // tm,),
        in_specs=specs,
        out_specs=[row(nq * LANES), kv(nk * LANES), kv(nv * LANES)],
        out_shape=[jax.ShapeDtypeStruct((n, nq * LANES), BF16),
                   jax.ShapeDtypeStruct((batches, kv_rows, nk * LANES), BF16),
                   jax.ShapeDtypeStruct((batches, kv_rows, nv * LANES), BF16)],
        input_output_aliases=aliases,
        compiler_params=_params(("parallel",)),
        name="pre_attn",
    )(*ins)


def _pre_gla_kernel(x_ref, mod_ref, g1_ref, w_ref, w2_ref, gb_ref,
                    q_ref, k_ref, v_ref, r_ref, g_ref, *, d, dk, dv, qscale):
    x = x_ref[...]
    m = mod_ref[0]
    h = _norm_mod(x, g1_ref[...], m[:, 0:d], m[:, d:2 * d]).astype(BF16)
    bounds = (dk, 2 * dk, 2 * dk + dv, 2 * dk + 2 * dv)
    ngrp = w_ref.shape[1] // MXU_N
    for grp in range(ngrp):
        c0 = grp * MXU_N
        y = _dot(h, w_ref[:, c0:c0 + MXU_N])
        if c0 < bounds[0]:
            q_ref[:, c0:c0 + MXU_N] = (y * qscale).astype(BF16)
        elif c0 < bounds[1]:
            k_ref[:, c0 - bounds[0]:c0 - bounds[0] + MXU_N] = y.astype(BF16)
        elif c0 < bounds[2]:
            v_ref[:, c0 - bounds[1]:c0 - bounds[1] + MXU_N] = y.astype(BF16)
        elif c0 < bounds[3]:
            r_ref[:, c0 - bounds[2]:c0 - bounds[2] + MXU_N] = y.astype(BF16)
        else:
            z = _dot(y.astype(BF16), w2_ref[...]) + gb_ref[...]
            logsig = jnp.minimum(z, 0.0) - jnp.log(1.0 + jnp.exp(-jnp.abs(z)))
            g_ref[...] = logsig * (1.0 / GLA_TAU)


def _pre_gla(x2d, modx, rows_per_mod, g1, w, w2, gb, *, dk, dv, qscale):
    n, d = x2d.shape
    tm = min(TM_PRE, n)
    mod_blocks = rows_per_mod // tm
    row = lambda c: pl.BlockSpec((tm, c), lambda i: (i, 0))
    kern = functools.partial(_pre_gla_kernel, d=d, dk=dk, dv=dv, qscale=qscale)
    return pl.pallas_call(
        kern,
        grid=(n // tm,),
        in_specs=[
            row(d),
            pl.BlockSpec((1, 1, 6 * d), lambda i: (i // mod_blocks, 0, 0)),
            _const_spec((1, d)),
            _const_spec(w.shape),
            _const_spec(w2.shape),
            _const_spec((1, 2 * dk)),
        ],
        out_specs=[row(dk), row(dk), row(dv), row(dv), row(2 * dk)],
        out_shape=[jax.ShapeDtypeStruct((n, c), t) for c, t in
                   ((dk, BF16), (dk, BF16), (dv, BF16), (dv, BF16), (2 * dk, F32))],
        compiler_params=_params(("parallel",)),
        name="pre_gla",
    )(x2d, modx, g1, w, w2, gb)


def _attn_kernel(q_ref, k_ref, v_ref, pn_ref, lam_ref, sub_ref, o_ref, kn_sc, m_sc, l1_sc, l_sc, acc_sc,
                 *maybe_p_sc, diff, tq, tk, n_steps, lambda_init):
    p_sc = maybe_p_sc[0] if diff else None
    lane = lax.broadcasted_iota(jnp.int32, (1, LANES), 1)
    lo_lane = lane < DIFF_DH

    @pl.when(pl.program_id(2) == 0)
    def _():
        pn = pn_ref[...]

        def seg_norm2_max(kb):
            kf = kb.astype(F32)
            return jnp.max(_dot((kf * kf).astype(BF16), pn), axis=0, keepdims=True)

        r = seg_norm2_max(k_ref[0, 0:tk, :])
        for i in range(1, n_steps):
            r = jnp.maximum(r, seg_norm2_max(k_ref[0, i * tk:(i + 1) * tk, :]))
        if diff:
            kn_sc[0:1, :] = jnp.broadcast_to(jnp.max(jnp.where(lo_lane, r, 0.0), axis=1, keepdims=True), (1, LANES))
            kn_sc[1:2, :] = jnp.broadcast_to(jnp.max(jnp.where(lo_lane, 0.0, r), axis=1, keepdims=True), (1, LANES))
        else:
            kn_sc[0:1, :] = r

    q = q_ref[0]
    if diff:
        lo_mask = jnp.where(lo_lane, 1.0, 0.0).astype(BF16)
        qs = jnp.concatenate([q * lo_mask, q * (1.0 - lo_mask)], axis=0)
        kn = jnp.concatenate([jnp.broadcast_to(kn_sc[0:1, :], (tq, LANES)),
                              jnp.broadcast_to(kn_sc[1:2, :], (tq, LANES))], axis=0)
    else:
        qs = jnp.concatenate([q[:, g * LANES:(g + 1) * LANES] for g in range(GQA_GROUP)], axis=0)
        kn = kn_sc[0:1, :]
    qf = qs.astype(F32)
    qn2 = _dot((qf * qf).astype(BF16), jnp.ones((LANES, LANES), BF16))
    shift = jnp.sqrt(qn2 * kn)

    def fast_step(kb, vb, first, col0=0):
        s = _dot_nt(qs, kb)
        psum, ps = None, []
        for c in range(kb.shape[0] // LANES):
            pc = jnp.exp2(s[:, c * LANES:(c + 1) * LANES] - shift)
            psum = pc if psum is None else psum + pc
            ps.append(pc.astype(BF16))
        if diff:
            p_sc[:, col0:col0 + kb.shape[0]] = jnp.concatenate(ps, axis=1)
        else:
            pv = _dot(jnp.concatenate(ps, axis=1), vb)
        if first:
            l_sc[...] = psum
            if not diff:
                acc_sc[...] = pv
        else:
            l_sc[...] += psum
            if not diff:
                acc_sc[...] += pv

    def exact_step(kb, vb, first):
        s = _dot_nt(qs, kb)
        m_cur = jnp.max(s, axis=-1, keepdims=True)
        if first:
            m_new = m_cur
            p = jnp.exp2(s - m_new)
            l1_sc[...] = jnp.sum(p, axis=-1, keepdims=True)
            acc_sc[...] = _dot(p.astype(BF16), vb)
        else:
            m_old = m_sc[...]
            m_new = jnp.maximum(m_old, m_cur)
            alpha = jnp.exp2(m_old - m_new)
            p = jnp.exp2(s - m_new)
            l1_sc[...] = alpha * l1_sc[...] + jnp.sum(p, axis=-1, keepdims=True)
            acc_sc[...] = alpha * acc_sc[...] + _dot(p.astype(BF16), vb)
        m_sc[...] = m_new

    def kv_chunk(i):
        return k_ref[0, i * tk:(i + 1) * tk, :], v_ref[0, i * tk:(i + 1) * tk, :]

    def finish_diff(a):
        ms = jnp.mean(a * a, axis=-1, keepdims=True)
        a = a * lax.rsqrt(ms + NORM_EPS) * sub_ref[...] * (1.0 - lambda_init)
        o_ref[0] = a.astype(BF16)

    def finish_gqa(o):
        for g in range(GQA_GROUP):
            o_ref[0, :, g * LANES:(g + 1) * LANES] = o[g * tq:(g + 1) * tq].astype(BF16)

    if diff:
        lp = lam_ref[...]
        lam = (jnp.exp(jnp.sum(lp[0:1] * lp[1:2], axis=-1, keepdims=True))
               - jnp.exp(jnp.sum(lp[2:3] * lp[3:4], axis=-1, keepdims=True)) + lambda_init)

    for i in range(n_steps):
        fast_step(*kv_chunk(i), i == 0, i * tk)
    denom = jnp.sum(l_sc[...], axis=-1, keepdims=True)
    ok = jnp.min(denom) >= ATTN_MIN_DENOM

    if diff:
        @pl.when(ok)
        def _():
            l1, l2 = denom[:tq], denom[tq:]
            rho = jnp.broadcast_to(lam * l1 / l2, (tq, LANES)).astype(BF16)
            acc = None
            for i in range(n_steps):
                mix = [p_sc[0:tq, i * tk + c * LANES:i * tk + (c + 1) * LANES]
                       - rho * p_sc[tq:2 * tq, i * tk + c * LANES:i * tk + (c + 1) * LANES]
                       for c in range(tk // LANES)]
                pv = _dot(jnp.concatenate(mix, axis=1), kv_chunk(i)[1])
                acc = pv if acc is None else acc + pv
            finish_diff(acc / l1)
    else:
        finish_gqa(acc_sc[...] / denom)

    @pl.when(jnp.logical_not(ok))
    def _():
        for i in range(n_steps):
            exact_step(*kv_chunk(i), i == 0)
        o = acc_sc[...] / l1_sc[...]
        if diff:
            finish_diff(o[:tq] - lam * o[tq:])
        else:
            finish_gqa(o)


def _attention(q, k, v, lamp, subln, *, diff, lambda_init, kv_rows, kv_block):
    b, t, _ = q.shape
    tq = min(TQ_DIFF if diff else TQ_GQA, t)
    qw = LANES if diff else GQA_GROUP * LANES
    heads = k.shape[2] // LANES
    stack = (2 if diff else GQA_GROUP) * tq
    tk = max(c for c in range(LANES, min(TK_ATTN, kv_rows) + 1, LANES) if kv_rows % c == 0)
    kv_spec = pl.BlockSpec((1, kv_rows, LANES), lambda bi, h, i: (bi, kv_block, h))
    seg = DIFF_DH if diff else LANES
    idx = jnp.arange(LANES) // seg
    pn = jnp.where(idx[:, None] == idx[None, :], 1.0, 0.0).astype(BF16)
    ins = [q, k, v, pn, lamp, subln]
    specs = [pl.BlockSpec((1, tq, qw), lambda bi, h, i: (bi, i, h)), kv_spec, kv_spec]
    specs += [pl.BlockSpec(a.shape, lambda bi, h, i: (0, 0)) for a in (pn, lamp, subln)]
    kern = functools.partial(_attn_kernel, diff=diff, tq=tq, tk=tk, n_steps=kv_rows // tk, lambda_init=lambda_init)
    scratch = [pltpu.VMEM((8, LANES), F32), pltpu.VMEM((stack, 1), F32), pltpu.VMEM((stack, 1), F32),
               pltpu.VMEM((stack, LANES), F32), pltpu.VMEM((stack, LANES), F32)]
    if diff:
        scratch.append(pltpu.VMEM((stack, kv_rows), BF16))
    return pl.pallas_call(
        kern,
        grid=(b, heads, t // tq),
        in_specs=specs,
        out_specs=pl.BlockSpec((1, tq, qw), lambda bi, h, i: (bi, i, h)),
        out_shape=jax.ShapeDtypeStruct((b, t, heads * qw), BF16),
        scratch_shapes=scratch,
        compiler_params=_params(("parallel", "parallel", "arbitrary")),
        name="attn_diff" if diff else "attn_gqa",
    )(*ins)


def _gla_local(q_ref, k_ref, v_ref, g_ref, *, nchunk, dkh, dvh, rev):
    c = GLA_CHUNK
    rows = nchunk * c
    ri = lax.broadcasted_iota(jnp.int32, (rows, rows), 0)
    ci = lax.broadcasted_iota(jnp.int32, (rows, rows), 1)
    causal = (ci >= ri) if rev else (ci <= ri)
    tri = jnp.where(jnp.logical_and(ri // c == ci // c, causal), 1.0, 0.0).astype(BF16)
    hi, mid, lo = _split3(g_ref[0])
    bsum = _dot(tri, hi) + _dot(tri, mid) + _dot(tri, lo)
    qf, kf = q_ref[0].astype(F32), k_ref[0].astype(F32)
    qt = (qf * jnp.exp(bsum)).astype(BF16)

    krow = lax.broadcasted_iota(jnp.int32, (c, bsum.shape[1]), 0)
    r64 = lax.broadcasted_iota(jnp.int32, (c, c), 0)
    c64 = lax.broadcasted_iota(jnp.int32, (c, c), 1)
    keep = (c64 >= r64) if rev else (c64 <= r64)
    nsub = c // GLA_SUB
    scaled = []
    for n in range(nchunk):
        bs, kc, qc = bsum[n * c:(n + 1) * c], kf[n * c:(n + 1) * c], qf[n * c:(n + 1) * c]
        b_tot = bs[0:1] if rev else bs[c - 1:c]
        kd = (kc * jnp.exp(b_tot - bs)).astype(BF16)
        qis, kis = [], []
        for i in range(nsub):
            lo_r, hi_r = i * GLA_SUB, (i + 1) * GLA_SUB
            if rev:
                ref = bs[hi_r:hi_r + 1] if i < nsub - 1 else jnp.zeros_like(b_tot)
                valid = krow >= lo_r
            else:
                ref = bs[lo_r - 1:lo_r] if i > 0 else jnp.zeros_like(b_tot)
                valid = krow < hi_r
            kis.append((kc * jnp.exp(jnp.where(valid, ref - bs, 0.0))).astype(BF16))
            qis.append((qc[lo_r:hi_r] * jnp.exp(bs[lo_r:hi_r] - ref)).astype(BF16))
        scaled.append((jnp.exp(b_tot), kd, qis, kis))

    pairs = [(n, h) for n in range(nchunk) for h in range(GLA_HEADS)]
    hks = [slice(h * dkh, (h + 1) * dkh) for h in range(GLA_HEADS)]
    vals = {(n, h): v_ref[0, n * c:(n + 1) * c, h * dvh:(h + 1) * dvh] for n, h in pairs}
    blocks = {(n, h): [_dot_nt(scaled[n][2][i][:, hks[h]], scaled[n][3][i][:, hks[h]]) for i in range(nsub)]
              for n, h in pairs}
    inc = {(n, h): _dot_tn(vals[n, h], scaled[n][1][:, hks[h]]) for n, h in pairs}
    amat = {p: jnp.where(keep, jnp.concatenate(blocks[p], axis=0), 0.0).astype(BF16) for p in pairs}
    intra = {p: _dot(amat[p], vals[p]) for p in pairs}
    local = {(n, h): (scaled[n][0][:, hks[h]], intra[n, h], inc[n, h]) for n, h in pairs}
    return qt, local


def _gla_kernel(qf_ref, kf_ref, vf_ref, gf_ref, qb_ref, kb_ref, vb_ref, gb_ref, s0_ref,
                of_ref, ob_ref, st_out_ref, st_sc, *, nchunk, dkh, dvh):
    j = pl.program_id(1)

    @pl.when(j == 0)
    def _():
        st_sc[...] = s0_ref[0]

    c = GLA_CHUNK
    kw = dict(nchunk=nchunk, dkh=dkh, dvh=dvh)
    sides = ((_gla_local(qf_ref, kf_ref, vf_ref, gf_ref, rev=False, **kw), of_ref),
             (_gla_local(qb_ref, kb_ref, vb_ref, gb_ref, rev=True, **kw), ob_ref))
    for step in range(nchunk):
        for side, ((qt, local), o_ref) in enumerate(sides):
            n = step if side == 0 else nchunk - 1 - step
            for h in range(GLA_HEADS):
                hk, hv = slice(h * dkh, (h + 1) * dkh), slice(h * dvh, (h + 1) * dvh)
                dec, intra, inc = local[n, h]
                st = st_sc[side, h]
                o = _dot_nt(qt[n * c:(n + 1) * c, hk], st.astype(BF16)) + intra
                o_ref[0, n * c:(n + 1) * c, hv] = o.astype(BF16)
                st_sc[side, h] = dec * st + inc

    @pl.when(j == pl.num_programs(1) - 1)
    def _():
        st_out_ref[0] = st_sc[...]


def _gla_scan(q, k, v, g, s0):
    b, t, dk = q.shape
    dv = v.shape[2]
    dkh, dvh = dk // GLA_HEADS, dv // GLA_HEADS
    nchunk = min(GLA_R, t // GLA_CHUNK)
    rows = nchunk * GLA_CHUNK
    n = t // rows
    fwd = lambda w, cb=0: pl.BlockSpec((1, rows, w), lambda bi, j: (bi, j, cb))
    bwd = lambda w, cb=0: pl.BlockSpec((1, rows, w), lambda bi, j: (bi, n - 1 - j, cb))
    st_spec = pl.BlockSpec((1, 2, GLA_HEADS, dvh, dkh), lambda bi, j: (bi, 0, 0, 0, 0))
    kern = functools.partial(_gla_kernel, nchunk=nchunk, dkh=dkh, dvh=dvh)
    return pl.pallas_call(
        kern,
        grid=(b, n),
        in_specs=[fwd(dk), fwd(dk), fwd(dv), fwd(dk, 0), bwd(dk), bwd(dk), bwd(dv), bwd(dk, 1), st_spec],
        out_specs=[fwd(dv), bwd(dv), st_spec],
        out_shape=[jax.ShapeDtypeStruct((b, t, dv), BF16), jax.ShapeDtypeStruct((b, t, dv), BF16),
                   jax.ShapeDtypeStruct(s0.shape, F32)],
        scratch_shapes=[pltpu.VMEM((2, GLA_HEADS, dvh, dkh), F32)],
        compiler_params=_params(("parallel", "arbitrary")),
        name="gla_scan",
    )(q, k, v, g, q, k, v, g, s0)


def _post_kernel(*refs, d, f, gla, dvh):
    if gla:
        x_ref, of_ref, ob_ref, r_ref, on_ref, mod_ref, wo_ref, g2_ref, wgu_ref, wd_ref, o_ref = refs
        r = r_ref[...].astype(F32)
        gate = r * _sigmoid(r)
        parts = []
        for h in range(GLA_HEADS):
            cs = slice(h * dvh, (h + 1) * dvh)
            oh = of_ref[:, cs].astype(F32) + ob_ref[:, cs].astype(F32)
            ms = jnp.mean(oh * oh, axis=-1, keepdims=True)
            parts.append((oh * lax.rsqrt(ms + NORM_EPS) * on_ref[...] * gate[:, cs]).astype(BF16))
        a = jnp.concatenate(parts, axis=1)
    else:
        x_ref, a_ref, mod_ref, wo_ref, g2_ref, wgu_ref, wd_ref, o_ref = refs
        a = a_ref[...]
    m = mod_ref[0]
    gate1, shift2, scale2, gate2 = (m[:, 2 * d:3 * d], m[:, 3 * d:4 * d], m[:, 4 * d:5 * d], m[:, 5 * d:6 * d])
    x1 = x_ref[...] + gate1 * _dot(a, wo_ref[...])
    h2 = _norm_mod(x1, g2_ref[...], shift2, scale2).astype(BF16)
    acc = jnp.zeros(x1.shape, F32)
    for c0 in range(0, f, FFN_CHUNK):
        gt = _dot(h2, wgu_ref[:, c0:c0 + FFN_CHUNK])
        up = _dot(h2, wgu_ref[:, f + c0:f + c0 + FFN_CHUNK])
        act = (gt * _sigmoid(gt) * up).astype(BF16)
        acc = acc + _dot(act, wd_ref[c0:c0 + FFN_CHUNK, :])
    o_ref[...] = x1 + gate2 * acc


def _post(x2d, mix, modx, rows_per_mod, wo, g2, wgu, wd, *, gla_norm=None):
    n, d = x2d.shape
    f = wd.shape[0]
    tm = min(TM_POST, n)
    mod_blocks = rows_per_mod // tm
    row = lambda c: pl.BlockSpec((tm, c), lambda i: (i, 0))
    gla = gla_norm is not None
    if gla:
        ins = [x2d, *mix, gla_norm]
        specs = [row(d), row(d), row(d), row(d), _const_spec(gla_norm.shape)]
        dvh = gla_norm.shape[1]
    else:
        ins = [x2d, mix]
        specs = [row(d), row(d)]
        dvh = 0
    ins += [modx, wo, g2, wgu, wd]
    specs += [pl.BlockSpec((1, 1, 6 * d), lambda i: (i // mod_blocks, 0, 0)),
              _const_spec(wo.shape), _const_spec((1, d)), _const_spec(wgu.shape), _const_spec(wd.shape)]
    kern = functools.partial(_post_kernel, d=d, f=f, gla=gla, dvh=dvh)
    return pl.pallas_call(
        kern,
        grid=(n // tm,),
        in_specs=specs,
        out_specs=row(d),
        out_shape=jax.ShapeDtypeStruct((n, d), F32),
        compiler_params=_params(("parallel",)),
        name="post_gla" if gla else "post_attn",
    )(*ins)


def _rope_tables(n_tokens, head_dim):
    rows = n_tokens // GRID_W
    rowp = jnp.broadcast_to(jnp.arange(rows)[:, None], (rows, GRID_W)).reshape(-1)
    colp = jnp.broadcast_to(jnp.arange(GRID_W)[None, :], (rows, GRID_W)).reshape(-1)
    half = head_dim // 2
    quarter = head_dim // 4
    inv = ROPE_BASE ** (-jnp.arange(0, half, 2, dtype=F32) / half)

    def axis_angles(pos):
        a = pos.astype(F32)[:, None] * inv[None, :]
        return jnp.concatenate([a, a], axis=-1)

    ang = jnp.concatenate([axis_angles(rowp), axis_angles(colp)], axis=-1)
    reps = LANES // head_dim
    cos = jnp.tile(jnp.cos(ang), (1, reps))
    sin = jnp.tile(jnp.sin(ang), (1, reps))
    first = (jnp.arange(LANES) % (2 * quarter)) < quarter
    sa = jnp.where(first[None, :], -sin, 0.0)
    sb = jnp.where(first[None, :], 0.0, sin)
    return cos, sa, sb


def _seg_matrix(seg):
    idx = jnp.arange(MXU_N) // seg
    return jnp.where(idx[:, None] == idx[None, :], 1.0 / seg, 0.0).astype(BF16)


def kernel(x, c, ctx, c_ctx, ada_w, ada_b, norm1_g, norm2_g, ffn_w_gu, ffn_w_down, diff_w_in, diff_w_out, diff_q_norm, diff_k_norm, diff_lambda_q1, diff_lambda_k1, diff_lambda_q2, diff_lambda_k2, diff_subln, gqa_w_in, gqa_w_out, gqa_q_norm, gqa_k_norm, gla_w_in, gla_gate_w1_fwd, gla_gate_w2_fwd, gla_gate_b_fwd, gla_gate_w1_bwd, gla_gate_w2_bwd, gla_gate_b_bwd, gla_out_norm, gla_w_out):
    b, l, d = x.shape
    lc = ctx.shape[1]
    depth = ada_w.shape[0]
    nl, nc = b * l, b * lc

    cond_rows = ((b + 1 + 7) // 8) * 8
    cond = jnp.zeros((cond_rows, d), F32).at[:b].set(c).at[b].set(c_ctx)
    mod = _modulation(cond, ada_w, ada_b)

    xl = x.reshape(nl, d)
    xc = ctx.reshape(nc, d)
    ones_tab = (jnp.ones((lc, LANES), F32), jnp.zeros((lc, LANES), F32), jnp.zeros((lc, LANES), F32))
    dummy_lam = jnp.zeros((4, DIFF_DH), F32)
    dummy_sub = jnp.ones((1, LANES), F32)

    for i in range(depth):
        last = i == depth - 1
        kind, j = i % N_MIXERS, i // N_MIXERS
        modl = mod[i, :b].reshape(b, 1, 6 * d)
        modc = mod[i, b:b + 1].reshape(1, 1, 6 * d)
        g1 = norm1_g[i].reshape(1, d)
        g2 = norm2_g[i].reshape(1, d)
        wgu = ffn_w_gu[i].astype(BF16)
        wd = ffn_w_down[i].astype(BF16)

        if kind in (0, 1):
            if kind == 0:
                diff, dh = True, DIFF_DH
                w_in, w_out = diff_w_in[j].astype(BF16), diff_w_out[j].astype(BF16)
                gq = jnp.tile(diff_q_norm[j], LANES // dh).reshape(1, LANES) * (dh ** -0.5 * LOG2E)
                gk = jnp.tile(diff_k_norm[j], LANES // dh).reshape(1, LANES)
                nq = nk = nv = d // LANES
                lamp = jnp.stack([diff_lambda_q1[j], diff_lambda_k1[j], diff_lambda_q2[j], diff_lambda_k2[j]])
                subln = diff_subln[j].reshape(1, LANES)
                lambda_init = 0.8 - 0.6 * math.exp(-0.3 * i)
            else:
                diff, dh = False, GQA_DH
                w_in, w_out = gqa_w_in[j].astype(BF16), gqa_w_out[j].astype(BF16)
                gq = gqa_q_norm[j].reshape(1, LANES) * (dh ** -0.5 * LOG2E)
                gk = gqa_k_norm[j].reshape(1, LANES)
                nq = d // LANES
                nk = nv = (w_in.shape[1] - d) // (2 * LANES)
                lamp, subln, lambda_init = dummy_lam, dummy_sub, 0.0
            pmat = _seg_matrix(dh)
            cos, sa, sb = _rope_tables(l, dh)
            kw = dict(nq=nq, nk=nk, nv=nv, rot=dh // 4)
            ql, k_all, v_all = _pre_attn(xl, modl, l, g1, w_in, pmat, gq, gk, cos, sa, sb, l, l + lc, 0, None, **kw)
            qc, k_all, v_all = _pre_attn(xc, modc, nc, g1, w_in, pmat, gq, gk, *ones_tab, lc, l + lc, l,
                                         (k_all, v_all), **kw)
            r3 = lambda t, n: t.reshape(b, n, t.shape[1])
            akw = dict(diff=diff, lambda_init=lambda_init)
            ol = _attention(r3(ql, l), k_all, v_all, lamp, subln, kv_rows=l + lc, kv_block=0, **akw).reshape(nl, d)
            xl = _post(xl, ol, modl, l, w_out, g2, wgu, wd)
            if not last:
                oc = _attention(r3(qc, lc), k_all, v_all, lamp, subln, kv_rows=lc, kv_block=l // lc, **akw)
                xc = _post(xc, oc.reshape(nc, d), modc, nc, w_out, g2, wgu, wd)
        else:
            dk, dv = gla_gate_w2_fwd.shape[2], gla_w_out.shape[1]
            rank = gla_gate_w1_fwd.shape[2]
            w1 = jnp.zeros((d, MXU_N), F32).at[:, :rank].set(gla_gate_w1_fwd[j]).at[:, rank:2 * rank].set(gla_gate_w1_bwd[j])
            w_in = jnp.concatenate([gla_w_in[j], w1], axis=1).astype(BF16)
            w2 = (jnp.zeros((MXU_N, 2 * dk), F32).at[:rank, :dk].set(gla_gate_w2_fwd[j])
                  .at[rank:2 * rank, dk:].set(gla_gate_w2_bwd[j])).astype(BF16)
            gb = jnp.concatenate([gla_gate_b_fwd[j], gla_gate_b_bwd[j]]).reshape(1, 2 * dk)
            w_out = gla_w_out[j].astype(BF16)
            dkh, dvh = dk // GLA_HEADS, dv // GLA_HEADS
            onorm = gla_out_norm[j].reshape(1, dvh)
            kw = dict(dk=dk, dv=dv, qscale=dkh ** -0.5)
            ql, kl, vl, rl, gl = _pre_gla(xl, modl, l, g1, w_in, w2, gb, **kw)
            qc, kc, vc, rc, gc = _pre_gla(xc, modc, nc, g1, w_in, w2, gb, **kw)
            r3 = lambda t, n: t.reshape(b, n, t.shape[1])
            s0 = jnp.zeros((b, 2, GLA_HEADS, dvh, dkh), F32)
            ocf, ocb, sc = _gla_scan(r3(qc, lc), r3(kc, lc), r3(vc, lc), r3(gc, lc), s0)
            olf, olb, _ = _gla_scan(r3(ql, l), r3(kl, l), r3(vl, l), r3(gl, l), sc)
            xl = _post(xl, (olf.reshape(nl, dv), olb.reshape(nl, dv), rl), modl, l, w_out, g2, wgu, wd, gla_norm=onorm)
            if not last:
                xc = _post(xc, (ocf.reshape(nc, dv), ocb.reshape(nc, dv), rc), modc, nc, w_out, g2, wgu, wd, gla_norm=onorm)
    return xl.reshape(b, l, d)
```

```python
import functools
import math

import jax
import jax.numpy as jnp
from jax import lax
from jax.experimental import pallas as pl
from jax.experimental.pallas import tpu as pltpu

F32 = jnp.float32
BF16 = jnp.bfloat16

GRID_W = 64
N_MIXERS = 3
ROPE_BASE = 10000.0
NORM_EPS = 1e-6
DIFF_DH = 64
GQA_DH = 128
GQA_GROUP = 4
GLA_HEADS = 4
GLA_GATE_RANK = 16
GLA_TAU = 16.0
GLA_CHUNK = 64
GLA_SUB = 16
LOG2E = math.log2(math.e)
ATTN_MIN_DENOM = 1e-22

LANES = 128
MXU_N = 256
VMEM_LIMIT = 56 * 1024 * 1024

TM_PRE = 512
TM_POST = 512
FFN_CHUNK = 256
TQ_DIFF = 512
TQ_GQA = 256
TK_ATTN = 3072
GLA_R = 4


def _const_spec(shape):
    nd = len(shape)
    return pl.BlockSpec(shape, lambda *_: (0,) * nd, pipeline_mode=pl.Buffered(1))


def _params(sem, vmem=VMEM_LIMIT):
    return pltpu.CompilerParams(dimension_semantics=sem, vmem_limit_bytes=vmem)


def _sigmoid(x):
    return 1.0 / (1.0 + jnp.exp(-x))


def _norm_mod(x, g, shift, scale):
    ms = jnp.mean(x * x, axis=-1, keepdims=True)
    return (x * lax.rsqrt(ms + NORM_EPS) * g) * (1.0 + scale) + shift


def _dot(a, b):
    return jnp.dot(a, b, preferred_element_type=F32)


def _dot_nt(a, b):
    return lax.dot_general(a, b, (((1,), (1,)), ((), ())), preferred_element_type=F32)


def _dot_tn(a, b):
    return lax.dot_general(a, b, (((0,), (0,)), ((), ())), preferred_element_type=F32)


def _split3(x):
    hi = x.astype(BF16)
    r1 = x - hi.astype(F32)
    mid = r1.astype(BF16)
    lo = (r1 - mid.astype(F32)).astype(BF16)
    return hi, mid, lo


def _mod_kernel(cond_ref, w_ref, b_ref, o_ref):
    c = cond_ref[...]
    s = (c * _sigmoid(c)).astype(BF16)
    o_ref[0] = _dot(s, w_ref[0].astype(BF16)) + b_ref[0]


def _modulation(cond, ada_w, ada_b):
    depth, d, n = ada_w.shape
    rows = cond.shape[0]
    tn = 1536
    return pl.pallas_call(
        _mod_kernel,
        grid=(depth, n // tn),
        in_specs=[
            pl.BlockSpec((rows, d), lambda i, j: (0, 0)),
            pl.BlockSpec((1, d, tn), lambda i, j: (i, 0, j)),
            pl.BlockSpec((1, 1, tn), lambda i, j: (i, 0, j)),
        ],
        out_specs=pl.BlockSpec((1, rows, tn), lambda i, j: (i, 0, j)),
        out_shape=jax.ShapeDtypeStruct((depth, rows, n), F32),
        compiler_params=_params(("parallel", "parallel")),
        name="adaln_mod",
    )(cond, ada_w, ada_b.reshape(depth, 1, n))


def _pre_attn_kernel(x_ref, mod_ref, g1_ref, w_ref, p_ref, gq_ref, gk_ref, cos_ref, sa_ref, sb_ref, *rest,
                     d, nq, nk, nv, rot):
    q_ref, k_ref, v_ref = rest[-3], rest[-2].at[0], rest[-1].at[0]
    x = x_ref[...]
    m = mod_ref[0]
    h = _norm_mod(x, g1_ref[...], m[:, 0:d], m[:, d:2 * d]).astype(BF16)
    cos, sa, sb = cos_ref[...], sa_ref[...], sb_ref[...]
    per = MXU_N // LANES

    def project(grp):
        return _dot(h, w_ref[:, grp * MXU_N:(grp + 1) * MXU_N])

    def emit(grp, y):
        slab0 = grp * per
        if slab0 >= nq + nk:
            j = slab0 - nq - nk
            v_ref[:, j * LANES:(j + per) * LANES] = y.astype(BF16)
            return
        ms = _dot((y * y).astype(BF16), p_ref[...])
        yn = y * lax.rsqrt(ms + NORM_EPS)
        for t in range(per):
            slab = slab0 + t
            out_ref, j, gain = (q_ref, slab, gq_ref) if slab < nq else (k_ref, slab - nq, gk_ref)
            xn = yn[:, t * LANES:(t + 1) * LANES] * gain[...]
            r = xn * cos + pltpu.roll(xn, LANES - rot, 1) * sa + pltpu.roll(xn, rot, 1) * sb
            out_ref[:, j * LANES:(j + 1) * LANES] = r.astype(BF16)

    ngrp = (nq + nk + nv) // per
    y = project(0)
    for grp in range(ngrp):
        y_next = project(grp + 1) if grp + 1 < ngrp else None
        emit(grp, y)
        y = y_next


def _pre_attn(x2d, modx, rows_per_mod, g1, w, pmat, gq, gk, cos, sa, sb, seq, kv_rows, kv_row0, kv_prev,
              *, nq, nk, nv, rot):
    n, d = x2d.shape
    tm = min(TM_PRE, seq)
    seq_blocks = seq // tm
    mod_blocks = rows_per_mod // tm
    kv_block0 = kv_row0 // tm
    row = lambda c: pl.BlockSpec((tm, c), lambda i: (i, 0))
    tab = pl.BlockSpec((tm, LANES), lambda i: (i % seq_blocks, 0))
    kv = lambda c: pl.BlockSpec((1, tm, c), lambda i: (i // seq_blocks, kv_block0 + i % seq_blocks, 0))
    ins = [x2d, modx, g1, w, pmat, gq, gk, cos, sa, sb]
    specs = [
        row(d),
        pl.BlockSpec((1, 1, 6 * d), lambda i: (i // mod_blocks, 0, 0)),
        _const_spec((1, d)),
        _const_spec(w.shape),
        _const_spec(pmat.shape),
        _const_spec((1, LANES)),
        _const_spec((1, LANES)),
        tab, tab, tab,
    ]
    aliases = {}
    if kv_prev is not None:
        aliases = {len(ins): 1, len(ins) + 1: 2}
        ins += list(kv_prev)
        specs += [pl.BlockSpec(memory_space=pl.ANY)] * 2
    batches = n // seq
    kern = functools.partial(_pre_attn_kernel, d=d, nq=nq, nk=nk, nv=nv, rot=rot)
    return pl.pallas_call(
        kern,
        grid=(n // tm,),
        in_specs=specs,
        out_specs=[row(nq * LANES), kv(nk * LANES), kv(nv * LANES)],
        out_shape=[jax.ShapeDtypeStruct((n, nq * LANES), BF16),
                   jax.ShapeDtypeStruct((batches, kv_rows, nk * LANES), BF16),
                   jax.ShapeDtypeStruct((batches, kv_rows, nv * LANES), BF16)],
        input_output_aliases=aliases,
        compiler_params=_params(("parallel",)),
        name="pre_attn",
    )(*ins)


def _pre_gla_kernel(x_ref, mod_ref, g1_ref, w_ref, w2_ref, gb_ref,
                    q_ref, k_ref, v_ref, r_ref, g_ref, *, d, dk, dv, qscale):
    x = x_ref[...]
    m = mod_ref[0]
    h = _norm_mod(x, g1_ref[...], m[:, 0:d], m[:, d:2 * d]).astype(BF16)
    bounds = (dk, 2 * dk, 2 * dk + dv, 2 * dk + 2 * dv)
    def project(c0):
        return _dot(h, w_ref[:, c0:c0 + MXU_N])

    z = _dot(project(bounds[3]).astype(BF16), w2_ref[...]) + gb_ref[...]
    logsig = jnp.minimum(z, 0.0) - jnp.log(1.0 + jnp.exp(-jnp.abs(z)))
    g_ref[...] = logsig * (1.0 / GLA_TAU)
    for c0 in range(0, bounds[3], MXU_N):
        y = project(c0)
        if c0 < bounds[0]:
            q_ref[:, c0:c0 + MXU_N] = (y * qscale).astype(BF16)
        elif c0 < bounds[1]:
            k_ref[:, c0 - bounds[0]:c0 - bounds[0] + MXU_N] = y.astype(BF16)
        elif c0 < bounds[2]:
            v_ref[:, c0 - bounds[1]:c0 - bounds[1] + MXU_N] = y.astype(BF16)
        else:
            r_ref[:, c0 - bounds[2]:c0 - bounds[2] + MXU_N] = y.astype(BF16)


def _pre_gla(x2d, modx, rows_per_mod, g1, w, w2, gb, *, dk, dv, qscale):
    n, d = x2d.shape
    tm = min(TM_PRE, n)
    mod_blocks = rows_per_mod // tm
    row = lambda c: pl.BlockSpec((tm, c), lambda i: (i, 0))
    kern = functools.partial(_pre_gla_kernel, d=d, dk=dk, dv=dv, qscale=qscale)
    return pl.pallas_call(
        kern,
        grid=(n // tm,),
        in_specs=[
            row(d),
            pl.BlockSpec((1, 1, 6 * d), lambda i: (i // mod_blocks, 0, 0)),
            _const_spec((1, d)),
            _const_spec(w.shape),
            _const_spec(w2.shape),
            _const_spec((1, 2 * dk)),
        ],
        out_specs=[row(dk), row(dk), row(dv), row(dv), row(2 * dk)],
        out_shape=[jax.ShapeDtypeStruct((n, c), t) for c, t in
                   ((dk, BF16), (dk, BF16), (dv, BF16), (dv, BF16), (2 * dk, F32))],
        compiler_params=_params(("parallel",)),
        name="pre_gla",
    )(x2d, modx, g1, w, w2, gb)


def _attn_kernel(q_ref, k_ref, v_ref, pn_ref, lam_ref, sub_ref, o_ref, kn_sc, m_sc, l1_sc, l_sc, acc_sc,
                 *maybe_p_sc, diff, tq, tk, n_steps, lambda_init):
    p_sc = maybe_p_sc[0] if diff else None
    lane = lax.broadcasted_iota(jnp.int32, (1, LANES), 1)
    lo_lane = lane < DIFF_DH

    @pl.when(pl.program_id(2) == 0)
    def _():
        pn = pn_ref[...]

        def seg_norm2_max(kb):
            kf = kb.astype(F32)
            return jnp.max(_dot((kf * kf).astype(BF16), pn), axis=0, keepdims=True)

        r = seg_norm2_max(k_ref[0, 0:tk, :])
        for i in range(1, n_steps):
            r = jnp.maximum(r, seg_norm2_max(k_ref[0, i * tk:(i + 1) * tk, :]))
        if diff:
            kn_sc[0:1, :] = jnp.broadcast_to(jnp.max(jnp.where(lo_lane, r, 0.0), axis=1, keepdims=True), (1, LANES))
            kn_sc[1:2, :] = jnp.broadcast_to(jnp.max(jnp.where(lo_lane, 0.0, r), axis=1, keepdims=True), (1, LANES))
        else:
            kn_sc[0:1, :] = r

    q = q_ref[0]
    if diff:
        lo_mask = jnp.where(lo_lane, 1.0, 0.0).astype(BF16)
        qs = jnp.concatenate([q * lo_mask, q * (1.0 - lo_mask)], axis=0)
        kn = jnp.concatenate([jnp.broadcast_to(kn_sc[0:1, :], (tq, LANES)),
                              jnp.broadcast_to(kn_sc[1:2, :], (tq, LANES))], axis=0)
    else:
        qs = jnp.concatenate([q[:, g * LANES:(g + 1) * LANES] for g in range(GQA_GROUP)], axis=0)
        kn = kn_sc[0:1, :]
    qf = qs.astype(F32)
    qn2 = _dot((qf * qf).astype(BF16), jnp.ones((LANES, LANES), BF16))
    shift = jnp.sqrt(qn2 * kn)

    def fast_step(kb, vb, first, col0=0):
        s = _dot_nt(qs, kb)
        psum, ps = None, []
        for c in range(kb.shape[0] // LANES):
            pc = jnp.exp2(s[:, c * LANES:(c + 1) * LANES] - shift)
            psum = pc if psum is None else psum + pc
            ps.append(pc.astype(BF16))
        if diff:
            p_sc[:, col0:col0 + kb.shape[0]] = jnp.concatenate(ps, axis=1)
        else:
            pv = _dot(jnp.concatenate(ps, axis=1), vb)
        if first:
            l_sc[...] = psum
            if not diff:
                acc_sc[...] = pv
        else:
            l_sc[...] += psum
            if not diff:
                acc_sc[...] += pv

    def exact_step(kb, vb, first):
        s = _dot_nt(qs, kb)
        m_cur = jnp.max(s, axis=-1, keepdims=True)
        if first:
            m_new = m_cur
            p = jnp.exp2(s - m_new)
            l1_sc[...] = jnp.sum(p, axis=-1, keepdims=True)
            acc_sc[...] = _dot(p.astype(BF16), vb)
        else:
            m_old = m_sc[...]
            m_new = jnp.maximum(m_old, m_cur)
            alpha = jnp.exp2(m_old - m_new)
            p = jnp.exp2(s - m_new)
            l1_sc[...] = alpha * l1_sc[...] + jnp.sum(p, axis=-1, keepdims=True)
            acc_sc[...] = alpha * acc_sc[...] + _dot(p.astype(BF16), vb)
        m_sc[...] = m_new

    def kv_chunk(i):
        return k_ref[0, i * tk:(i + 1) * tk, :], v_ref[0, i * tk:(i + 1) * tk, :]

    def finish_diff(a):
        ms = jnp.mean(a * a, axis=-1, keepdims=True)
        a = a * lax.rsqrt(ms + NORM_EPS) * sub_ref[...] * (1.0 - lambda_init)
        o_ref[0] = a.astype(BF16)

    def finish_gqa(o):
        for g in range(GQA_GROUP):
            o_ref[0, :, g * LANES:(g + 1) * LANES] = o[g * tq:(g + 1) * tq].astype(BF16)

    if diff:
        lp = lam_ref[...]
        lam = (jnp.exp(jnp.sum(lp[0:1] * lp[1:2], axis=-1, keepdims=True))
               - jnp.exp(jnp.sum(lp[2:3] * lp[3:4], axis=-1, keepdims=True)) + lambda_init)

    for i in range(n_steps):
        fast_step(*kv_chunk(i), i == 0, i * tk)
    denom = jnp.sum(l_sc[...], axis=-1, keepdims=True)
    ok = jnp.min(denom) >= ATTN_MIN_DENOM

    if diff:
        @pl.when(ok)
        def _():
            l1, l2 = denom[:tq], denom[tq:]
            rho = jnp.broadcast_to(lam * l1 / l2, (tq, LANES)).astype(BF16)
            acc = None
            for i in range(n_steps):
                mix = [p_sc[0:tq, i * tk + c * LANES:i * tk + (c + 1) * LANES]
                       - rho * p_sc[tq:2 * tq, i * tk + c * LANES:i * tk + (c + 1) * LANES]
                       for c in range(tk // LANES)]
                pv = _dot(jnp.concatenate(mix, axis=1), kv_chunk(i)[1])
                acc = pv if acc is None else acc + pv
            finish_diff(acc / l1)
    else:
        finish_gqa(acc_sc[...] / denom)

    @pl.when(jnp.logical_not(ok))
    def _():
        for i in range(n_steps):
            exact_step(*kv_chunk(i), i == 0)
        o = acc_sc[...] / l1_sc[...]
        if diff:
            finish_diff(o[:tq] - lam * o[tq:])
        else:
            finish_gqa(o)


def _attention(q, k, v, lamp, subln, *, diff, lambda_init, kv_rows, kv_block):
    b, t, _ = q.shape
    tq = min(TQ_DIFF if diff else TQ_GQA, t)
    qw = LANES if diff else GQA_GROUP * LANES
    heads = k.shape[2] // LANES
    stack = (2 if diff else GQA_GROUP) * tq
    steps = [c for c in range(LANES, min(TK_ATTN, kv_rows) + 1, LANES) if kv_rows % c == 0]
    tk = max([c for c in steps if c % MXU_N == 0] or steps)
    kv_spec = pl.BlockSpec((1, kv_rows, LANES), lambda bi, h, i: (bi, kv_block, h))
    seg = DIFF_DH if diff else LANES
    idx = jnp.arange(LANES) // seg
    pn = jnp.where(idx[:, None] == idx[None, :], 1.0, 0.0).astype(BF16)
    ins = [q, k, v, pn, lamp, subln]
    specs = [pl.BlockSpec((1, tq, qw), lambda bi, h, i: (bi, i, h)), kv_spec, kv_spec]
    specs += [pl.BlockSpec(a.shape, lambda bi, h, i: (0, 0)) for a in (pn, lamp, subln)]
    kern = functools.partial(_attn_kernel, diff=diff, tq=tq, tk=tk, n_steps=kv_rows // tk, lambda_init=lambda_init)
    scratch = [pltpu.VMEM((8, LANES), F32), pltpu.VMEM((stack, 1), F32), pltpu.VMEM((stack, 1), F32),
               pltpu.VMEM((stack, LANES), F32), pltpu.VMEM((stack, LANES), F32)]
    if diff:
        scratch.append(pltpu.VMEM((stack, kv_rows), BF16))
    return pl.pallas_call(
        kern,
        grid=(b, heads, t // tq),
        in_specs=specs,
        out_specs=pl.BlockSpec((1, tq, qw), lambda bi, h, i: (bi, i, h)),
        out_shape=jax.ShapeDtypeStruct((b, t, heads * qw), BF16),
        scratch_shapes=scratch,
        compiler_params=_params(("parallel", "parallel", "arbitrary")),
        name="attn_diff" if diff else "attn_gqa",
    )(*ins)


def _gla_local(q_ref, k_ref, v_ref, g_ref, *, nchunk, dkh, dvh, rev):
    c = GLA_CHUNK
    rows = nchunk * c
    ri = lax.broadcasted_iota(jnp.int32, (rows, rows), 0)
    ci = lax.broadcasted_iota(jnp.int32, (rows, rows), 1)
    causal = (ci >= ri) if rev else (ci <= ri)
    tri = jnp.where(jnp.logical_and(ri // c == ci // c, causal), 1.0, 0.0).astype(BF16)
    hi, mid, lo = _split3(g_ref[0])
    bsum = _dot(tri, hi) + _dot(tri, mid) + _dot(tri, lo)
    qf, kf = q_ref[0].astype(F32), k_ref[0].astype(F32)
    qt = (qf * jnp.exp(bsum)).astype(BF16)

    krow = lax.broadcasted_iota(jnp.int32, (c, bsum.shape[1]), 0)
    r64 = lax.broadcasted_iota(jnp.int32, (c, c), 0)
    c64 = lax.broadcasted_iota(jnp.int32, (c, c), 1)
    keep = (c64 >= r64) if rev else (c64 <= r64)
    nsub = c // GLA_SUB
    scaled = []
    for n in range(nchunk):
        bs, kc, qc = bsum[n * c:(n + 1) * c], kf[n * c:(n + 1) * c], qf[n * c:(n + 1) * c]
        b_tot = bs[0:1] if rev else bs[c - 1:c]
        kd = (kc * jnp.exp(b_tot - bs)).astype(BF16)
        qis, kis = [], []
        for i in range(nsub):
            lo_r, hi_r = i * GLA_SUB, (i + 1) * GLA_SUB
            if rev:
                ref = bs[hi_r:hi_r + 1] if i < nsub - 1 else jnp.zeros_like(b_tot)
                valid = krow >= lo_r
            else:
                ref = bs[lo_r - 1:lo_r] if i > 0 else jnp.zeros_like(b_tot)
                valid = krow < hi_r
            kis.append((kc * jnp.exp(jnp.where(valid, ref - bs, 0.0))).astype(BF16))
            qis.append((qc[lo_r:hi_r] * jnp.exp(bs[lo_r:hi_r] - ref)).astype(BF16))
        scaled.append((jnp.exp(b_tot), kd, qis, kis))

    pairs = [(n, h) for n in range(nchunk) for h in range(GLA_HEADS)]
    hks = [slice(h * dkh, (h + 1) * dkh) for h in range(GLA_HEADS)]
    vals = {(n, h): v_ref[0, n * c:(n + 1) * c, h * dvh:(h + 1) * dvh] for n, h in pairs}
    blocks = {(n, h): [_dot_nt(scaled[n][2][i][:, hks[h]], scaled[n][3][i][:, hks[h]]) for i in range(nsub)]
              for n, h in pairs}
    inc = {(n, h): _dot_tn(vals[n, h], scaled[n][1][:, hks[h]]) for n, h in pairs}
    amat = {p: jnp.where(keep, jnp.concatenate(blocks[p], axis=0), 0.0).astype(BF16) for p in pairs}
    intra = {p: _dot(amat[p], vals[p]) for p in pairs}
    local = {(n, h): (scaled[n][0][:, hks[h]], intra[n, h], inc[n, h]) for n, h in pairs}
    return qt, local


def _gla_kernel(qf_ref, kf_ref, vf_ref, gf_ref, qb_ref, kb_ref, vb_ref, gb_ref, s0_ref,
                of_ref, ob_ref, st_out_ref, st_sc, *, nchunk, dkh, dvh):
    j = pl.program_id(1)

    @pl.when(j == 0)
    def _():
        st_sc[...] = s0_ref[0]

    c = GLA_CHUNK
    kw = dict(nchunk=nchunk, dkh=dkh, dvh=dvh)
    sides = ((_gla_local(qf_ref, kf_ref, vf_ref, gf_ref, rev=False, **kw), of_ref),
             (_gla_local(qb_ref, kb_ref, vb_ref, gb_ref, rev=True, **kw), ob_ref))
    for step in range(nchunk):
        for side, ((qt, local), o_ref) in enumerate(sides):
            n = step if side == 0 else nchunk - 1 - step
            for h in range(GLA_HEADS):
                hk, hv = slice(h * dkh, (h + 1) * dkh), slice(h * dvh, (h + 1) * dvh)
                dec, intra, inc = local[n, h]
                st = st_sc[side, h]
                o = _dot_nt(qt[n * c:(n + 1) * c, hk], st.astype(BF16)) + intra
                o_ref[0, n * c:(n + 1) * c, hv] = o.astype(BF16)
                st_sc[side, h] = dec * st + inc

    @pl.when(j == pl.num_programs(1) - 1)
    def _():
        st_out_ref[0] = st_sc[...]


def _gla_scan(q, k, v, g, s0):
    b, t, dk = q.shape
    dv = v.shape[2]
    dkh, dvh = dk // GLA_HEADS, dv // GLA_HEADS
    nchunk = min(GLA_R, t // GLA_CHUNK)
    rows = nchunk * GLA_CHUNK
    n = t // rows
    fwd = lambda w, cb=0: pl.BlockSpec((1, rows, w), lambda bi, j: (bi, j, cb))
    bwd = lambda w, cb=0: pl.BlockSpec((1, rows, w), lambda bi, j: (bi, n - 1 - j, cb))
    st_spec = pl.BlockSpec((1, 2, GLA_HEADS, dvh, dkh), lambda bi, j: (bi, 0, 0, 0, 0))
    kern = functools.partial(_gla_kernel, nchunk=nchunk, dkh=dkh, dvh=dvh)
    return pl.pallas_call(
        kern,
        grid=(b, n),
        in_specs=[fwd(dk), fwd(dk), fwd(dv), fwd(dk, 0), bwd(dk), bwd(dk), bwd(dv), bwd(dk, 1), st_spec],
        out_specs=[fwd(dv), bwd(dv), st_spec],
        out_shape=[jax.ShapeDtypeStruct((b, t, dv), BF16), jax.ShapeDtypeStruct((b, t, dv), BF16),
                   jax.ShapeDtypeStruct(s0.shape, F32)],
        scratch_shapes=[pltpu.VMEM((2, GLA_HEADS, dvh, dkh), F32)],
        compiler_params=_params(("parallel", "arbitrary")),
        name="gla_scan",
    )(q, k, v, g, q, k, v, g, s0)


def _post_kernel(*refs, d, f, gla, dvh):
    if gla:
        x_ref, of_ref, ob_ref, r_ref, on_ref, mod_ref, wo_ref, g2_ref, wgu_ref, wd_ref, o_ref = refs
        r = r_ref[...].astype(F32)
        gate = r * _sigmoid(r)
        parts = []
        for h in range(GLA_HEADS):
            cs = slice(h * dvh, (h + 1) * dvh)
            oh = of_ref[:, cs].astype(F32) + ob_ref[:, cs].astype(F32)
            ms = jnp.mean(oh * oh, axis=-1, keepdims=True)
            parts.append((oh * lax.rsqrt(ms + NORM_EPS) * on_ref[...] * gate[:, cs]).astype(BF16))
        a = jnp.concatenate(parts, axis=1)
    else:
        x_ref, a_ref, mod_ref, wo_ref, g2_ref, wgu_ref, wd_ref, o_ref = refs
        a = a_ref[...]
    m = mod_ref[0]
    gate1, shift2, scale2, gate2 = (m[:, 2 * d:3 * d], m[:, 3 * d:4 * d], m[:, 4 * d:5 * d], m[:, 5 * d:6 * d])
    x1 = x_ref[...] + gate1 * _dot(a, wo_ref[...])
    h2 = _norm_mod(x1, g2_ref[...], shift2, scale2).astype(BF16)
    acc = jnp.zeros(x1.shape, F32)
    for c0 in range(0, f, FFN_CHUNK):
        gt = _dot(h2, wgu_ref[:, c0:c0 + FFN_CHUNK])
        up = _dot(h2, wgu_ref[:, f + c0:f + c0 + FFN_CHUNK])
        act = (gt * _sigmoid(gt) * up).astype(BF16)
        acc = acc + _dot(act, wd_ref[c0:c0 + FFN_CHUNK, :])
    o_ref[...] = x1 + gate2 * acc


def _post(x2d, mix, modx, rows_per_mod, wo, g2, wgu, wd, *, gla_norm=None):
    n, d = x2d.shape
    f = wd.shape[0]
    tm = min(TM_POST, n)
    mod_blocks = rows_per_mod // tm
    row = lambda c: pl.BlockSpec((tm, c), lambda i: (i, 0))
    gla = gla_norm is not None
    if gla:
        ins = [x2d, *mix, gla_norm]
        specs = [row(d), row(d), row(d), row(d), _const_spec(gla_norm.shape)]
        dvh = gla_norm.shape[1]
    else:
        ins = [x2d, mix]
        specs = [row(d), row(d)]
        dvh = 0
    ins += [modx, wo, g2, wgu, wd]
    specs += [pl.BlockSpec((1, 1, 6 * d), lambda i: (i // mod_blocks, 0, 0)),
              _const_spec(wo.shape), _const_spec((1, d)), _const_spec(wgu.shape), _const_spec(wd.shape)]
    kern = functools.partial(_post_kernel, d=d, f=f, gla=gla, dvh=dvh)
    return pl.pallas_call(
        kern,
        grid=(n // tm,),
        in_specs=specs,
        out_specs=row(d),
        out_shape=jax.ShapeDtypeStruct((n, d), F32),
        compiler_params=_params(("parallel",)),
        name="post_gla" if gla else "post_attn",
    )(*ins)


def _rope_tables(n_tokens, head_dim):
    rows = n_tokens // GRID_W
    rowp = jnp.broadcast_to(jnp.arange(rows)[:, None], (rows, GRID_W)).reshape(-1)
    colp = jnp.broadcast_to(jnp.arange(GRID_W)[None, :], (rows, GRID_W)).reshape(-1)
    half = head_dim // 2
    quarter = head_dim // 4
    inv = ROPE_BASE ** (-jnp.arange(0, half, 2, dtype=F32) / half)

    def axis_angles(pos):
        a = pos.astype(F32)[:, None] * inv[None, :]
        return jnp.concatenate([a, a], axis=-1)

    ang = jnp.concatenate([axis_angles(rowp), axis_angles(colp)], axis=-1)
    reps = LANES // head_dim
    cos = jnp.tile(jnp.cos(ang), (1, reps))
    sin = jnp.tile(jnp.sin(ang), (1, reps))
    first = (jnp.arange(LANES) % (2 * quarter)) < quarter
    sa = jnp.where(first[None, :], -sin, 0.0)
    sb = jnp.where(first[None, :], 0.0, sin)
    return cos, sa, sb


def _seg_matrix(seg):
    idx = jnp.arange(MXU_N) // seg
    return jnp.where(idx[:, None] == idx[None, :], 1.0 / seg, 0.0).astype(BF16)


def kernel(x, c, ctx, c_ctx, ada_w, ada_b, norm1_g, norm2_g, ffn_w_gu, ffn_w_down, diff_w_in, diff_w_out, diff_q_norm, diff_k_norm, diff_lambda_q1, diff_lambda_k1, diff_lambda_q2, diff_lambda_k2, diff_subln, gqa_w_in, gqa_w_out, gqa_q_norm, gqa_k_norm, gla_w_in, gla_gate_w1_fwd, gla_gate_w2_fwd, gla_gate_b_fwd, gla_gate_w1_bwd, gla_gate_w2_bwd, gla_gate_b_bwd, gla_out_norm, gla_w_out):
    b, l, d = x.shape
    lc = ctx.shape[1]
    depth = ada_w.shape[0]
    nl, nc = b * l, b * lc

    cond_rows = ((b + 1 + 7) // 8) * 8
    cond = jnp.zeros((cond_rows, d), F32).at[:b].set(c).at[b].set(c_ctx)
    mod = _modulation(cond, ada_w, ada_b)

    xl = x.reshape(nl, d)
    xc = ctx.reshape(nc, d)
    ones_tab = (jnp.ones((lc, LANES), F32), jnp.zeros((lc, LANES), F32), jnp.zeros((lc, LANES), F32))
    dummy_lam = jnp.zeros((4, DIFF_DH), F32)
    dummy_sub = jnp.ones((1, LANES), F32)

    for i in range(depth):
        last = i == depth - 1
        kind, j = i % N_MIXERS, i // N_MIXERS
        modl = mod[i, :b].reshape(b, 1, 6 * d)
        modc = mod[i, b:b + 1].reshape(1, 1, 6 * d)
        g1 = norm1_g[i].reshape(1, d)
        g2 = norm2_g[i].reshape(1, d)
        wgu = ffn_w_gu[i].astype(BF16)
        wd = ffn_w_down[i].astype(BF16)

        if kind in (0, 1):
            if kind == 0:
                diff, dh = True, DIFF_DH
                w_in, w_out = diff_w_in[j].astype(BF16), diff_w_out[j].astype(BF16)
                gq = jnp.tile(diff_q_norm[j], LANES // dh).reshape(1, LANES) * (dh ** -0.5 * LOG2E)
                gk = jnp.tile(diff_k_norm[j], LANES // dh).reshape(1, LANES)
                nq = nk = nv = d // LANES
                lamp = jnp.stack([diff_lambda_q1[j], diff_lambda_k1[j], diff_lambda_q2[j], diff_lambda_k2[j]])
                subln = diff_subln[j].reshape(1, LANES)
                lambda_init = 0.8 - 0.6 * math.exp(-0.3 * i)
            else:
                diff, dh = False, GQA_DH
                w_in, w_out = gqa_w_in[j].astype(BF16), gqa_w_out[j].astype(BF16)
                gq = gqa_q_norm[j].reshape(1, LANES) * (dh ** -0.5 * LOG2E)
                gk = gqa_k_norm[j].reshape(1, LANES)
                nq = d // LANES
                nk = nv = (w_in.shape[1] - d) // (2 * LANES)
                lamp, subln, lambda_init = dummy_lam, dummy_sub, 0.0
            pmat = _seg_matrix(dh)
            cos, sa, sb = _rope_tables(l, dh)
            kw = dict(nq=nq, nk=nk, nv=nv, rot=dh // 4)
            ql, k_all, v_all = _pre_attn(xl, modl, l, g1, w_in, pmat, gq, gk, cos, sa, sb, l, l + lc, 0, None, **kw)
            qc, k_all, v_all = _pre_attn(xc, modc, nc, g1, w_in, pmat, gq, gk, *ones_tab, lc, l + lc, l,
                                         (k_all, v_all), **kw)
            r3 = lambda t, n: t.reshape(b, n, t.shape[1])
            akw = dict(diff=diff, lambda_init=lambda_init)
            ol = _attention(r3(ql, l), k_all, v_all, lamp, subln, kv_rows=l + lc, kv_block=0, **akw).reshape(nl, d)
            xl = _post(xl, ol, modl, l, w_out, g2, wgu, wd)
            if not last:
                oc = _attention(r3(qc, lc), k_all, v_all, lamp, subln, kv_rows=lc, kv_block=l // lc, **akw)
                xc = _post(xc, oc.reshape(nc, d), modc, nc, w_out, g2, wgu, wd)
        else:
            dk, dv = gla_gate_w2_fwd.shape[2], gla_w_out.shape[1]
            rank = gla_gate_w1_fwd.shape[2]
            w1 = jnp.zeros((d, MXU_N), F32).at[:, :rank].set(gla_gate_w1_fwd[j]).at[:, rank:2 * rank].set(gla_gate_w1_bwd[j])
            w_in = jnp.concatenate([gla_w_in[j], w1], axis=1).astype(BF16)
            w2 = (jnp.zeros((MXU_N, 2 * dk), F32).at[:rank, :dk].set(gla_gate_w2_fwd[j])
                  .at[rank:2 * rank, dk:].set(gla_gate_w2_bwd[j])).astype(BF16)
            gb = jnp.concatenate([gla_gate_b_fwd[j], gla_gate_b_bwd[j]]).reshape(1, 2 * dk)
            w_out = gla_w_out[j].astype(BF16)
            dkh, dvh = dk // GLA_HEADS, dv // GLA_HEADS
            onorm = gla_out_norm[j].reshape(1, dvh)
            kw = dict(dk=dk, dv=dv, qscale=dkh ** -0.5)
            ql, kl, vl, rl, gl = _pre_gla(xl, modl, l, g1, w_in, w2, gb, **kw)
            qc, kc, vc, rc, gc = _pre_gla(xc, modc, nc, g1, w_in, w2, gb, **kw)
            r3 = lambda t, n: t.reshape(b, n, t.shape[1])
            s0 = jnp.zeros((b, 2, GLA_HEADS, dvh, dkh), F32)
            ocf, ocb, sc = _gla_scan(r3(qc, lc), r3(kc, lc), r3(vc, lc), r3(gc, lc), s0)
            olf, olb, _ = _gla_scan(r3(ql, l), r3(kl, l), r3(vl, l), r3(gl, l), sc)
            xl = _post(xl, (olf.reshape(nl, dv), olb.reshape(nl, dv), rl), modl, l, w_out, g2, wgu, wd, gla_norm=onorm)
            if not last:
                xc = _post(xc, (ocf.reshape(nc, dv), ocb.reshape(nc, dv), rc), modc, nc, w_out, g2, wgu, wd, gla_norm=onorm)
    return xl.reshape(b, l, d)
```

```python
import functools
import math

import jax
import jax.numpy as jnp
from jax import lax
from jax.experimental import pallas as pl
from jax.experimental.pallas import tpu as pltpu

F32 = jnp.float32
BF16 = jnp.bfloat16

GRID_W = 64
N_MIXERS = 3
ROPE_BASE = 10000.0
NORM_EPS = 1e-6
DIFF_DH = 64
GQA_DH = 128
GQA_GROUP = 4
GLA_HEADS = 4
GLA_GATE_RANK = 16
GLA_TAU = 16.0
GLA_CHUNK = 64
GLA_SUB = 16
LOG2E = math.log2(math.e)
ATTN_MIN_DENOM = 1e-22

LANES = 128
MXU_N = 256
VMEM_LIMIT = 56 * 1024 * 1024

TM_PRE = 512
TM_POST = 512
FFN_CHUNK = 256
TQ_DIFF = 512
TQ_GQA = 256
TK_ATTN = 3072
TK_EXACT = 768
GLA_R = 4


def _const_spec(shape):
    nd = len(shape)
    return pl.BlockSpec(shape, lambda *_: (0,) * nd, pipeline_mode=pl.Buffered(1))


def _params(sem, vmem=VMEM_LIMIT):
    return pltpu.CompilerParams(dimension_semantics=sem, vmem_limit_bytes=vmem)


def _sigmoid(x):
    return 1.0 / (1.0 + jnp.exp(-x))


def _norm_mod(x, g, shift, scale):
    ms = jnp.mean(x * x, axis=-1, keepdims=True)
    return (x * lax.rsqrt(ms + NORM_EPS) * g) * (1.0 + scale) + shift


def _dot(a, b):
    return jnp.dot(a, b, preferred_element_type=F32)


def _dot_nt(a, b):
    return lax.dot_general(a, b, (((1,), (1,)), ((), ())), preferred_element_type=F32)


def _dot_tn(a, b):
    return lax.dot_general(a, b, (((0,), (0,)), ((), ())), preferred_element_type=F32)


def _split3(x):
    hi = x.astype(BF16)
    r1 = x - hi.astype(F32)
    mid = r1.astype(BF16)
    lo = (r1 - mid.astype(F32)).astype(BF16)
    return hi, mid, lo


def _mod_kernel(cond_ref, w_ref, b_ref, o_ref):
    c = cond_ref[...]
    s = (c * _sigmoid(c)).astype(BF16)
    o_ref[0] = _dot(s, w_ref[0].astype(BF16)) + b_ref[0]


def _modulation(cond, ada_w, ada_b):
    depth, d, n = ada_w.shape
    rows = cond.shape[0]
    tn = 1536
    return pl.pallas_call(
        _mod_kernel,
        grid=(depth, n // tn),
        in_specs=[
            pl.BlockSpec((rows, d), lambda i, j: (0, 0)),
            pl.BlockSpec((1, d, tn), lambda i, j: (i, 0, j)),
            pl.BlockSpec((1, 1, tn), lambda i, j: (i, 0, j)),
        ],
        out_specs=pl.BlockSpec((1, rows, tn), lambda i, j: (i, 0, j)),
        out_shape=jax.ShapeDtypeStruct((depth, rows, n), F32),
        compiler_params=_params(("parallel", "parallel")),
        name="adaln_mod",
    )(cond, ada_w, ada_b.reshape(depth, 1, n))


def _pre_attn_kernel(x_ref, mod_ref, g1_ref, w_ref, p_ref, gq_ref, gk_ref, cos_ref, sa_ref, sb_ref, *rest,
                     d, nq, nk, nv, rot):
    q_ref, k_ref, v_ref = rest[-3], rest[-2].at[0], rest[-1].at[0]
    x = x_ref[...]
    m = mod_ref[0]
    h = _norm_mod(x, g1_ref[...], m[:, 0:d], m[:, d:2 * d]).astype(BF16)
    cos, sa, sb = cos_ref[...], sa_ref[...], sb_ref[...]
    per = MXU_N // LANES

    def project(grp):
        return _dot(h, w_ref[:, grp * MXU_N:(grp + 1) * MXU_N])

    def emit(grp, y):
        slab0 = grp * per
        if slab0 >= nq + nk:
            j = slab0 - nq - nk
            v_ref[:, j * LANES:(j + per) * LANES] = y.astype(BF16)
            return
        ms = _dot((y * y).astype(BF16), p_ref[...])
        yn = y * lax.rsqrt(ms + NORM_EPS)
        for t in range(per):
            slab = slab0 + t
            out_ref, j, gain = (q_ref, slab, gq_ref) if slab < nq else (k_ref, slab - nq, gk_ref)
            xn = yn[:, t * LANES:(t + 1) * LANES] * gain[...]
            r = xn * cos + pltpu.roll(xn, LANES - rot, 1) * sa + pltpu.roll(xn, rot, 1) * sb
            out_ref[:, j * LANES:(j + 1) * LANES] = r.astype(BF16)

    ngrp = (nq + nk + nv) // per
    y = project(0)
    for grp in range(ngrp):
        y_next = project(grp + 1) if grp + 1 < ngrp else None
        emit(grp, y)
        y = y_next


def _pre_attn(x2d, modx, rows_per_mod, g1, w, pmat, gq, gk, cos, sa, sb, seq, kv_rows, kv_row0, kv_prev,
              *, nq, nk, nv, rot):
    n, d = x2d.shape
    tm = min(TM_PRE, seq)
    seq_blocks = seq // tm
    mod_blocks = rows_per_mod // tm
    kv_block0 = kv_row0 // tm
    row = lambda c: pl.BlockSpec((tm, c), lambda i: (i, 0))
    tab = pl.BlockSpec((tm, LANES), lambda i: (i % seq_blocks, 0))
    kv = lambda c: pl.BlockSpec((1, tm, c), lambda i: (i // seq_blocks, kv_block0 + i % seq_blocks, 0))
    ins = [x2d, modx, g1, w, pmat, gq, gk, cos, sa, sb]
    specs = [
        row(d),
        pl.BlockSpec((1, 1, 6 * d), lambda i: (i // mod_blocks, 0, 0)),
        _const_spec((1, d)),
        _const_spec(w.shape),
        _const_spec(pmat.shape),
        _const_spec((1, LANES)),
        _const_spec((1, LANES)),
        tab, tab, tab,
    ]
    aliases = {}
    if kv_prev is not None:
        aliases = {len(ins): 1, len(ins) + 1: 2}
        ins += list(kv_prev)
        specs += [pl.BlockSpec(memory_space=pl.ANY)] * 2
    batches = n // seq
    kern = functools.partial(_pre_attn_kernel, d=d, nq=nq, nk=nk, nv=nv, rot=rot)
    return pl.pallas_call(
        kern,
        grid=(n // tm,),
        in_specs=specs,
        out_specs=[row(nq * LANES), kv(nk * LANES), kv(nv * LANES)],
        out_shape=[jax.ShapeDtypeStruct((n, nq * LANES), BF16),
                   jax.ShapeDtypeStruct((batches, kv_rows, nk * LANES), BF16),
                   jax.ShapeDtypeStruct((batches, kv_rows, nv * LANES), BF16)],
        input_output_aliases=aliases,
        compiler_params=_params(("parallel",)),
        name="pre_attn",
    )(*ins)


def _pre_gla_kernel(x_ref, mod_ref, g1_ref, w_ref, w2_ref, gb_ref,
                    q_ref, k_ref, v_ref, r_ref, g_ref, *, d, dk, dv, qscale):
    x = x_ref[...]
    m = mod_ref[0]
    h = _norm_mod(x, g1_ref[...], m[:, 0:d], m[:, d:2 * d]).astype(BF16)
    bounds = (dk, 2 * dk, 2 * dk + dv, 2 * dk + 2 * dv)
    def project(c0):
        return _dot(h, w_ref[:, c0:c0 + MXU_N])

    z = _dot(project(bounds[3]).astype(BF16), w2_ref[...]) + gb_ref[...]
    logsig = jnp.minimum(z, 0.0) - jnp.log(1.0 + jnp.exp(-jnp.abs(z)))
    g_ref[...] = logsig * (1.0 / GLA_TAU)
    for c0 in range(0, bounds[3], MXU_N):
        y = project(c0)
        if c0 < bounds[0]:
            q_ref[:, c0:c0 + MXU_N] = (y * qscale).astype(BF16)
        elif c0 < bounds[1]:
            k_ref[:, c0 - bounds[0]:c0 - bounds[0] + MXU_N] = y.astype(BF16)
        elif c0 < bounds[2]:
            v_ref[:, c0 - bounds[1]:c0 - bounds[1] + MXU_N] = y.astype(BF16)
        else:
            r_ref[:, c0 - bounds[2]:c0 - bounds[2] + MXU_N] = y.astype(BF16)


def _pre_gla(x2d, modx, rows_per_mod, g1, w, w2, gb, *, dk, dv, qscale):
    n, d = x2d.shape
    tm = min(TM_PRE, n)
    mod_blocks = rows_per_mod // tm
    row = lambda c: pl.BlockSpec((tm, c), lambda i: (i, 0))
    kern = functools.partial(_pre_gla_kernel, d=d, dk=dk, dv=dv, qscale=qscale)
    return pl.pallas_call(
        kern,
        grid=(n // tm,),
        in_specs=[
            row(d),
            pl.BlockSpec((1, 1, 6 * d), lambda i: (i // mod_blocks, 0, 0)),
            _const_spec((1, d)),
            _const_spec(w.shape),
            _const_spec(w2.shape),
            _const_spec((1, 2 * dk)),
        ],
        out_specs=[row(dk), row(dk), row(dv), row(dv), row(2 * dk)],
        out_shape=[jax.ShapeDtypeStruct((n, c), t) for c, t in
                   ((dk, BF16), (dk, BF16), (dv, BF16), (dv, BF16), (2 * dk, F32))],
        compiler_params=_params(("parallel",)),
        name="pre_gla",
    )(x2d, modx, g1, w, w2, gb)


def _attn_kernel(q_ref, k_ref, v_ref, shift_ref, lam_ref, sub_ref, o_ref, m_sc, l1_sc, l_sc, acc_sc,
                 *maybe_p_sc, diff, tq, tk, tk_exact, lambda_init):
    p_sc = maybe_p_sc[0] if diff else None
    kv_rows = k_ref.shape[1]
    n_steps = kv_rows // tk
    lane = lax.broadcasted_iota(jnp.int32, (1, LANES), 1)
    lo_mask = jnp.where(lane < DIFF_DH, 1.0, 0.0).astype(BF16)
    shift = shift_ref[...]

    def stacked_q(r0, n):
        q = q_ref[0, r0:r0 + n, :]
        if diff:
            return jnp.concatenate([q * lo_mask, q * (1.0 - lo_mask)], axis=0)
        return jnp.concatenate([q[:, g * LANES:(g + 1) * LANES] for g in range(GQA_GROUP)], axis=0)

    def probabilities(qs, i):
        s = _dot_nt(qs, k_ref[0, i * tk:(i + 1) * tk, :])
        psum, ps = None, []
        for c in range(tk // LANES):
            pc = jnp.exp2(s[:, c * LANES:(c + 1) * LANES] - shift)
            psum = pc if psum is None else psum + pc
            ps.append(pc.astype(BF16))
        return jnp.concatenate(ps, axis=1), psum

    def finish_diff(a, r0, n):
        ms = jnp.mean(a * a, axis=-1, keepdims=True)
        a = a * lax.rsqrt(ms + NORM_EPS) * sub_ref[...] * (1.0 - lambda_init)
        o_ref[0, r0:r0 + n, :] = a.astype(BF16)

    def finish_gqa(o):
        for g in range(GQA_GROUP):
            o_ref[0, :, g * LANES:(g + 1) * LANES] = o[g * tq:(g + 1) * tq].astype(BF16)

    if diff:
        lp = lam_ref[...]
        lam = (jnp.exp(jnp.sum(lp[0:1] * lp[1:2], axis=-1, keepdims=True))
               - jnp.exp(jnp.sum(lp[2:3] * lp[3:4], axis=-1, keepdims=True)) + lambda_init)
        qs = stacked_q(0, tq)
        for i in range(n_steps):
            p, psum = probabilities(qs, i)
            p_sc[:, i * tk:(i + 1) * tk] = p
            if i == 0:
                l_sc[...] = psum
            else:
                l_sc[...] += psum
        denom = jnp.sum(l_sc[...], axis=-1, keepdims=True)
        ok = jnp.min(denom) >= ATTN_MIN_DENOM

        @pl.when(ok)
        def _():
            l1, l2 = denom[:tq], denom[tq:]
            rho = jnp.broadcast_to(lam * l1 / l2, (tq, LANES)).astype(BF16)
            acc = None
            for i in range(n_steps):
                mix = [p_sc[0:tq, i * tk + c * LANES:i * tk + (c + 1) * LANES]
                       - rho * p_sc[tq:2 * tq, i * tk + c * LANES:i * tk + (c + 1) * LANES]
                       for c in range(tk // LANES)]
                pv = _dot(jnp.concatenate(mix, axis=1), v_ref[0, i * tk:(i + 1) * tk, :])
                acc = pv if acc is None else acc + pv
            finish_diff(acc / l1, 0, tq)
    else:
        qs = stacked_q(0, tq)
        for i in range(n_steps):
            p, psum = probabilities(qs, i)
            pv = _dot(p, v_ref[0, i * tk:(i + 1) * tk, :])
            if i == 0:
                l_sc[...] = psum
                acc_sc[...] = pv
            else:
                l_sc[...] += psum
                acc_sc[...] += pv
        denom = jnp.sum(l_sc[...], axis=-1, keepdims=True)
        finish_gqa(acc_sc[...] / denom)
        ok = jnp.min(denom) >= ATTN_MIN_DENOM

    @pl.when(jnp.logical_not(ok))
    def _():
        qs = stacked_q(0, tq)
        m_sc[...] = jnp.full(m_sc.shape, -jnp.inf, F32)
        l1_sc[...] = jnp.zeros(l1_sc.shape, F32)
        acc_sc[...] = jnp.zeros(acc_sc.shape, F32)

        def body(i, carry):
            off = pl.multiple_of(i * tk_exact, tk_exact)
            s = _dot_nt(qs, k_ref[0, pl.ds(off, tk_exact), :])
            m_old = m_sc[...]
            m_new = jnp.maximum(m_old, jnp.max(s, axis=-1, keepdims=True))
            alpha = jnp.exp2(m_old - m_new)
            p = jnp.exp2(s - m_new)
            l1_sc[...] = alpha * l1_sc[...] + jnp.sum(p, axis=-1, keepdims=True)
            acc_sc[...] = alpha * acc_sc[...] + _dot(p.astype(BF16), v_ref[0, pl.ds(off, tk_exact), :])
            m_sc[...] = m_new
            return carry

        lax.fori_loop(0, kv_rows // tk_exact, body, 0)
        o = acc_sc[...] / l1_sc[...]
        if diff:
            finish_diff(o[:tq] - lam * o[tq:], 0, tq)
        else:
            finish_gqa(o)


def _attention(q, k, v, shift, lamp, subln, *, diff, lambda_init, kv_rows, kv_block):
    b, t, _ = q.shape
    tq = min(TQ_DIFF if diff else TQ_GQA, t)
    qw = LANES if diff else GQA_GROUP * LANES
    heads = k.shape[2] // LANES
    stack = (2 if diff else GQA_GROUP) * tq
    steps = [c for c in range(LANES, min(TK_ATTN, kv_rows) + 1, LANES) if kv_rows % c == 0]
    tk = max([c for c in steps if c % MXU_N == 0] or steps)
    kv_spec = pl.BlockSpec((1, kv_rows, LANES), lambda bi, h, i: (bi, kv_block, h))
    ins = [q, k, v, shift, lamp, subln]
    specs = [pl.BlockSpec((1, tq, qw), lambda bi, h, i: (bi, i, h)), kv_spec, kv_spec]
    specs += [pl.BlockSpec(a.shape, lambda bi, h, i: (0, 0)) for a in (shift, lamp, subln)]
    tk_exact = max(c for c in steps if c <= TK_EXACT)
    kern = functools.partial(_attn_kernel, diff=diff, tq=tq, tk=tk, tk_exact=tk_exact, lambda_init=lambda_init)
    scratch = [pltpu.VMEM((stack, 1), F32), pltpu.VMEM((stack, 1), F32),
               pltpu.VMEM((stack, LANES), F32), pltpu.VMEM((stack, LANES), F32)]
    if diff:
        scratch.append(pltpu.VMEM((stack, kv_rows), BF16))
    return pl.pallas_call(
        kern,
        grid=(b, heads, t // tq),
        in_specs=specs,
        out_specs=pl.BlockSpec((1, tq, qw), lambda bi, h, i: (bi, i, h)),
        out_shape=jax.ShapeDtypeStruct((b, t, heads * qw), BF16),
        scratch_shapes=scratch,
        compiler_params=_params(("parallel", "parallel", "arbitrary")),
        name="attn_diff" if diff else "attn_gqa",
    )(*ins)


def _gla_local(q_ref, k_ref, v_ref, g_ref, *, nchunk, dkh, dvh, rev):
    c = GLA_CHUNK
    rows = nchunk * c
    ri = lax.broadcasted_iota(jnp.int32, (rows, rows), 0)
    ci = lax.broadcasted_iota(jnp.int32, (rows, rows), 1)
    causal = (ci >= ri) if rev else (ci <= ri)
    tri = jnp.where(jnp.logical_and(ri // c == ci // c, causal), 1.0, 0.0).astype(BF16)
    hi, mid, lo = _split3(g_ref[0])
    bsum = _dot(tri, hi) + _dot(tri, mid) + _dot(tri, lo)
    qf, kf = q_ref[0].astype(F32), k_ref[0].astype(F32)
    qt = (qf * jnp.exp(bsum)).astype(BF16)

    krow = lax.broadcasted_iota(jnp.int32, (c, bsum.shape[1]), 0)
    r64 = lax.broadcasted_iota(jnp.int32, (c, c), 0)
    c64 = lax.broadcasted_iota(jnp.int32, (c, c), 1)
    keep = (c64 >= r64) if rev else (c64 <= r64)
    nsub = c // GLA_SUB
    scaled = []
    for n in range(nchunk):
        bs, kc, qc = bsum[n * c:(n + 1) * c], kf[n * c:(n + 1) * c], qf[n * c:(n + 1) * c]
        b_tot = bs[0:1] if rev else bs[c - 1:c]
        kd = (kc * jnp.exp(b_tot - bs)).astype(BF16)
        qis, kis = [], []
        for i in range(nsub):
            lo_r, hi_r = i * GLA_SUB, (i + 1) * GLA_SUB
            if rev:
                ref = bs[hi_r:hi_r + 1] if i < nsub - 1 else jnp.zeros_like(b_tot)
                valid = krow >= lo_r
            else:
                ref = bs[lo_r - 1:lo_r] if i > 0 else jnp.zeros_like(b_tot)
                valid = krow < hi_r
            kis.append((kc * jnp.exp(jnp.where(valid, ref - bs, 0.0))).astype(BF16))
            qis.append((qc[lo_r:hi_r] * jnp.exp(bs[lo_r:hi_r] - ref)).astype(BF16))
        scaled.append((jnp.exp(b_tot), kd, qis, kis))

    pairs = [(n, h) for n in range(nchunk) for h in range(GLA_HEADS)]
    hks = [slice(h * dkh, (h + 1) * dkh) for h in range(GLA_HEADS)]
    vals = {(n, h): v_ref[0, n * c:(n + 1) * c, h * dvh:(h + 1) * dvh] for n, h in pairs}
    blocks = {(n, h): [_dot_nt(scaled[n][2][i][:, hks[h]], scaled[n][3][i][:, hks[h]]) for i in range(nsub)]
              for n, h in pairs}
    inc = {(n, h): _dot_tn(vals[n, h], scaled[n][1][:, hks[h]]) for n, h in pairs}
    amat = {p: jnp.where(keep, jnp.concatenate(blocks[p], axis=0), 0.0).astype(BF16) for p in pairs}
    intra = {p: _dot(amat[p], vals[p]) for p in pairs}
    local = {(n, h): (scaled[n][0][:, hks[h]], intra[n, h], inc[n, h]) for n, h in pairs}
    return qt, local


def _gla_kernel(qf_ref, kf_ref, vf_ref, gf_ref, qb_ref, kb_ref, vb_ref, gb_ref, s0_ref,
                of_ref, ob_ref, st_out_ref, st_sc, *, nchunk, dkh, dvh):
    j = pl.program_id(1)

    @pl.when(j == 0)
    def _():
        st_sc[...] = s0_ref[0]

    c = GLA_CHUNK
    kw = dict(nchunk=nchunk, dkh=dkh, dvh=dvh)
    sides = ((_gla_local(qf_ref, kf_ref, vf_ref, gf_ref, rev=False, **kw), of_ref),
             (_gla_local(qb_ref, kb_ref, vb_ref, gb_ref, rev=True, **kw), ob_ref))
    for step in range(nchunk):
        for side, ((qt, local), o_ref) in enumerate(sides):
            n = step if side == 0 else nchunk - 1 - step
            for h in range(GLA_HEADS):
                hk, hv = slice(h * dkh, (h + 1) * dkh), slice(h * dvh, (h + 1) * dvh)
                dec, intra, inc = local[n, h]
                st = st_sc[side, h]
                o = _dot_nt(qt[n * c:(n + 1) * c, hk], st.astype(BF16)) + intra
                o_ref[0, n * c:(n + 1) * c, hv] = o.astype(BF16)
                st_sc[side, h] = dec * st + inc

    @pl.when(j == pl.num_programs(1) - 1)
    def _():
        st_out_ref[0] = st_sc[...]


def _gla_scan(q, k, v, g, s0):
    b, t, dk = q.shape
    dv = v.shape[2]
    dkh, dvh = dk // GLA_HEADS, dv // GLA_HEADS
    nchunk = min(GLA_R, t // GLA_CHUNK)
    rows = nchunk * GLA_CHUNK
    n = t // rows
    fwd = lambda w, cb=0: pl.BlockSpec((1, rows, w), lambda bi, j: (bi, j, cb))
    bwd = lambda w, cb=0: pl.BlockSpec((1, rows, w), lambda bi, j: (bi, n - 1 - j, cb))
    st_spec = pl.BlockSpec((1, 2, GLA_HEADS, dvh, dkh), lambda bi, j: (bi, 0, 0, 0, 0))
    kern = functools.partial(_gla_kernel, nchunk=nchunk, dkh=dkh, dvh=dvh)
    return pl.pallas_call(
        kern,
        grid=(b, n),
        in_specs=[fwd(dk), fwd(dk), fwd(dv), fwd(dk, 0), bwd(dk), bwd(dk), bwd(dv), bwd(dk, 1), st_spec],
        out_specs=[fwd(dv), bwd(dv), st_spec],
        out_shape=[jax.ShapeDtypeStruct((b, t, dv), BF16), jax.ShapeDtypeStruct((b, t, dv), BF16),
                   jax.ShapeDtypeStruct(s0.shape, F32)],
        scratch_shapes=[pltpu.VMEM((2, GLA_HEADS, dvh, dkh), F32)],
        compiler_params=_params(("parallel", "arbitrary")),
        name="gla_scan",
    )(q, k, v, g, q, k, v, g, s0)


def _post_kernel(*refs, d, f, gla, dvh):
    if gla:
        x_ref, of_ref, ob_ref, r_ref, on_ref, mod_ref, wo_ref, g2_ref, wgu_ref, wd_ref, o_ref = refs
        r = r_ref[...].astype(F32)
        gate = r * _sigmoid(r)
        parts = []
        for h in range(GLA_HEADS):
            cs = slice(h * dvh, (h + 1) * dvh)
            oh = of_ref[:, cs].astype(F32) + ob_ref[:, cs].astype(F32)
            ms = jnp.mean(oh * oh, axis=-1, keepdims=True)
            parts.append((oh * lax.rsqrt(ms + NORM_EPS) * on_ref[...] * gate[:, cs]).astype(BF16))
        a = jnp.concatenate(parts, axis=1)
    else:
        x_ref, a_ref, mod_ref, wo_ref, g2_ref, wgu_ref, wd_ref, o_ref = refs
        a = a_ref[...]
    m = mod_ref[0]
    gate1, shift2, scale2, gate2 = (m[:, 2 * d:3 * d], m[:, 3 * d:4 * d], m[:, 4 * d:5 * d], m[:, 5 * d:6 * d])
    x1 = x_ref[...] + gate1 * _dot(a, wo_ref[...])
    h2 = _norm_mod(x1, g2_ref[...], shift2, scale2).astype(BF16)
    acts = []
    for c0 in range(0, f, FFN_CHUNK):
        gt = _dot(h2, wgu_ref[:, c0:c0 + FFN_CHUNK])
        up = _dot(h2, wgu_ref[:, f + c0:f + c0 + FFN_CHUNK])
        acts.append((gt * _sigmoid(gt) * up).astype(BF16))
    o_ref[...] = x1 + gate2 * _dot(jnp.concatenate(acts, axis=1), wd_ref[...])


def _post(x2d, mix, modx, rows_per_mod, wo, g2, wgu, wd, *, gla_norm=None):
    n, d = x2d.shape
    f = wd.shape[0]
    tm = min(TM_POST, n)
    mod_blocks = rows_per_mod // tm
    row = lambda c: pl.BlockSpec((tm, c), lambda i: (i, 0))
    gla = gla_norm is not None
    if gla:
        ins = [x2d, *mix, gla_norm]
        specs = [row(d), row(d), row(d), row(d), _const_spec(gla_norm.shape)]
        dvh = gla_norm.shape[1]
    else:
        ins = [x2d, mix]
        specs = [row(d), row(d)]
        dvh = 0
    ins += [modx, wo, g2, wgu, wd]
    specs += [pl.BlockSpec((1, 1, 6 * d), lambda i: (i // mod_blocks, 0, 0)),
              _const_spec(wo.shape), _const_spec((1, d)), _const_spec(wgu.shape), _const_spec(wd.shape)]
    kern = functools.partial(_post_kernel, d=d, f=f, gla=gla, dvh=dvh)
    return pl.pallas_call(
        kern,
        grid=(n // tm,),
        in_specs=specs,
        out_specs=row(d),
        out_shape=jax.ShapeDtypeStruct((n, d), F32),
        compiler_params=_params(("parallel",)),
        name="post_gla" if gla else "post_attn",
    )(*ins)


def _rope_tables(n_tokens, head_dim):
    rows = n_tokens // GRID_W
    rowp = jnp.broadcast_to(jnp.arange(rows)[:, None], (rows, GRID_W)).reshape(-1)
    colp = jnp.broadcast_to(jnp.arange(GRID_W)[None, :], (rows, GRID_W)).reshape(-1)
    half = head_dim // 2
    quarter = head_dim // 4
    inv = ROPE_BASE ** (-jnp.arange(0, half, 2, dtype=F32) / half)

    def axis_angles(pos):
        a = pos.astype(F32)[:, None] * inv[None, :]
        return jnp.concatenate([a, a], axis=-1)

    ang = jnp.concatenate([axis_angles(rowp), axis_angles(colp)], axis=-1)
    reps = LANES // head_dim
    cos = jnp.tile(jnp.cos(ang), (1, reps))
    sin = jnp.tile(jnp.sin(ang), (1, reps))
    first = (jnp.arange(LANES) % (2 * quarter)) < quarter
    sa = jnp.where(first[None, :], -sin, 0.0)
    sb = jnp.where(first[None, :], 0.0, sin)
    return cos, sa, sb


def _seg_matrix(seg):
    idx = jnp.arange(MXU_N) // seg
    return jnp.where(idx[:, None] == idx[None, :], 1.0 / seg, 0.0).astype(BF16)


def kernel(x, c, ctx, c_ctx, ada_w, ada_b, norm1_g, norm2_g, ffn_w_gu, ffn_w_down, diff_w_in, diff_w_out, diff_q_norm, diff_k_norm, diff_lambda_q1, diff_lambda_k1, diff_lambda_q2, diff_lambda_k2, diff_subln, gqa_w_in, gqa_w_out, gqa_q_norm, gqa_k_norm, gla_w_in, gla_gate_w1_fwd, gla_gate_w2_fwd, gla_gate_b_fwd, gla_gate_w1_bwd, gla_gate_w2_bwd, gla_gate_b_bwd, gla_out_norm, gla_w_out):
    b, l, d = x.shape
    lc = ctx.shape[1]
    depth = ada_w.shape[0]
    nl, nc = b * l, b * lc

    cond_rows = ((b + 1 + 7) // 8) * 8
    cond = jnp.zeros((cond_rows, d), F32).at[:b].set(c).at[b].set(c_ctx)
    mod = _modulation(cond, ada_w, ada_b)

    xl = x.reshape(nl, d)
    xc = ctx.reshape(nc, d)
    ones_tab = (jnp.ones((lc, LANES), F32), jnp.zeros((lc, LANES), F32), jnp.zeros((lc, LANES), F32))
    dummy_lam = jnp.zeros((4, DIFF_DH), F32)
    dummy_sub = jnp.ones((1, LANES), F32)

    for i in range(depth):
        last = i == depth - 1
        kind, j = i % N_MIXERS, i // N_MIXERS
        modl = mod[i, :b].reshape(b, 1, 6 * d)
        modc = mod[i, b:b + 1].reshape(1, 1, 6 * d)
        g1 = norm1_g[i].reshape(1, d)
        g2 = norm2_g[i].reshape(1, d)
        wgu = ffn_w_gu[i].astype(BF16)
        wd = ffn_w_down[i].astype(BF16)

        if kind in (0, 1):
            if kind == 0:
                diff, dh = True, DIFF_DH
                w_in, w_out = diff_w_in[j].astype(BF16), diff_w_out[j].astype(BF16)
                gq = jnp.tile(diff_q_norm[j], LANES // dh).reshape(1, LANES) * (dh ** -0.5 * LOG2E)
                gk = jnp.tile(diff_k_norm[j], LANES // dh).reshape(1, LANES)
                nq = nk = nv = d // LANES
                lamp = jnp.stack([diff_lambda_q1[j], diff_lambda_k1[j], diff_lambda_q2[j], diff_lambda_k2[j]])
                subln = diff_subln[j].reshape(1, LANES)
                lambda_init = 0.8 - 0.6 * math.exp(-0.3 * i)
            else:
                diff, dh = False, GQA_DH
                w_in, w_out = gqa_w_in[j].astype(BF16), gqa_w_out[j].astype(BF16)
                gq = gqa_q_norm[j].reshape(1, LANES) * (dh ** -0.5 * LOG2E)
                gk = gqa_k_norm[j].reshape(1, LANES)
                nq = d // LANES
                nk = nv = (w_in.shape[1] - d) // (2 * LANES)
                lamp, subln, lambda_init = dummy_lam, dummy_sub, 0.0
            pmat = _seg_matrix(dh)
            cos, sa, sb = _rope_tables(l, dh)
            kw = dict(nq=nq, nk=nk, nv=nv, rot=dh // 4)
            ql, k_all, v_all = _pre_attn(xl, modl, l, g1, w_in, pmat, gq, gk, cos, sa, sb, l, l + lc, 0, None, **kw)
            qc, k_all, v_all = _pre_attn(xc, modc, nc, g1, w_in, pmat, gq, gk, *ones_tab, lc, l + lc, l,
                                         (k_all, v_all), **kw)
            r3 = lambda t, n: t.reshape(b, n, t.shape[1])
            shift = jnp.full((1, LANES), dh, F32) * jnp.max(jnp.abs(gq)) * jnp.max(jnp.abs(gk))
            akw = dict(diff=diff, lambda_init=lambda_init)
            ol = _attention(r3(ql, l), k_all, v_all, shift, lamp, subln, kv_rows=l + lc, kv_block=0, **akw)
            xl = _post(xl, ol.reshape(nl, d), modl, l, w_out, g2, wgu, wd)
            if not last:
                oc = _attention(r3(qc, lc), k_all, v_all, shift, lamp, subln, kv_rows=lc, kv_block=l // lc, **akw)
                xc = _post(xc, oc.reshape(nc, d), modc, nc, w_out, g2, wgu, wd)
        else:
            dk, dv = gla_gate_w2_fwd.shape[2], gla_w_out.shape[1]
            rank = gla_gate_w1_fwd.shape[2]
            w1 = jnp.zeros((d, MXU_N), F32).at[:, :rank].set(gla_gate_w1_fwd[j]).at[:, rank:2 * rank].set(gla_gate_w1_bwd[j])
            w_in = jnp.concatenate([gla_w_in[j], w1], axis=1).astype(BF16)
            w2 = (jnp.zeros((MXU_N, 2 * dk), F32).at[:rank, :dk].set(gla_gate_w2_fwd[j])
                  .at[rank:2 * rank, dk:].set(gla_gate_w2_bwd[j])).astype(BF16)
            gb = jnp.concatenate([gla_gate_b_fwd[j], gla_gate_b_bwd[j]]).reshape(1, 2 * dk)
            w_out = gla_w_out[j].astype(BF16)
            dkh, dvh = dk // GLA_HEADS, dv // GLA_HEADS
            onorm = gla_out_norm[j].reshape(1, dvh)
            kw = dict(dk=dk, dv=dv, qscale=dkh ** -0.5)
            ql, kl, vl, rl, gl = _pre_gla(xl, modl, l, g1, w_in, w2, gb, **kw)
            qc, kc, vc, rc, gc = _pre_gla(xc, modc, nc, g1, w_in, w2, gb, **kw)
            r3 = lambda t, n: t.reshape(b, n, t.shape[1])
            s0 = jnp.zeros((b, 2, GLA_HEADS, dvh, dkh), F32)
            ocf, ocb, sc = _gla_scan(r3(qc, lc), r3(kc, lc), r3(vc, lc), r3(gc, lc), s0)
            olf, olb, _ = _gla_scan(r3(ql, l), r3(kl, l), r3(vl, l), r3(gl, l), sc)
            xl = _post(xl, (olf.reshape(nl, dv), olb.reshape(nl, dv), rl), modl, l, w_out, g2, wgu, wd, gla_norm=onorm)
            if not last:
                xc = _post(xc, (ocf.reshape(nc, dv), ocb.reshape(nc, dv), rc), modc, nc, w_out, g2, wgu, wd, gla_norm=onorm)
    return xl.reshape(b, l, d)
```

```python
import functools
import math

import jax
import jax.numpy as jnp
from jax import lax
from jax.experimental import pallas as pl
from jax.experimental.pallas import tpu as pltpu

F32 = jnp.float32
BF16 = jnp.bfloat16

GRID_W = 64
N_MIXERS = 3
ROPE_BASE = 10000.0
NORM_EPS = 1e-6
DIFF_DH = 64
GQA_DH = 128
GQA_GROUP = 4
GLA_HEADS = 4
GLA_GATE_RANK = 16
GLA_TAU = 16.0
GLA_CHUNK = 64
GLA_SUB = 16
LOG2E = math.log2(math.e)
ATTN_MIN_DENOM = 1e-22
GLA_MAX_EXPONENT = 80.0

LANES = 128
SUBLANES = 8
MXU_N = 256
VMEM_LIMIT = 56 * 1024 * 1024

MOD_TN = 1536
TM_PRE = 512
TM_POST = 512
FFN_CHUNK = 256
TQ_DIFF = 512
TQ_GQA = 512
TK_ATTN = 3072
TK_EXACT = 768
GLA_R = 4


def _const_spec(shape):
    nd = len(shape)
    return pl.BlockSpec(shape, lambda *_: (0,) * nd, pipeline_mode=pl.Buffered(1))


def _params(sem, vmem=VMEM_LIMIT):
    return pltpu.CompilerParams(dimension_semantics=sem, vmem_limit_bytes=vmem)


def _sigmoid(x):
    return 1.0 / (1.0 + jnp.exp(-x))


def _norm_mod(x, g, shift, scale):
    ms = jnp.mean(x * x, axis=-1, keepdims=True)
    return (x * lax.rsqrt(ms + NORM_EPS) * g) * (1.0 + scale) + shift


def _dot(a, b):
    return jnp.dot(a, b, preferred_element_type=F32)


def _dot_nt(a, b):
    return lax.dot_general(a, b, (((1,), (1,)), ((), ())), preferred_element_type=F32)


def _dot_tn(a, b):
    return lax.dot_general(a, b, (((0,), (0,)), ((), ())), preferred_element_type=F32)


def _split3(x):
    hi = x.astype(BF16)
    r1 = x - hi.astype(F32)
    mid = r1.astype(BF16)
    lo = (r1 - mid.astype(F32)).astype(BF16)
    return hi, mid, lo


def _mod_kernel(cond_ref, w_ref, b_ref, o_ref):
    c = cond_ref[...]
    s = (c * _sigmoid(c)).astype(BF16)
    o_ref[0] = _dot(s, w_ref[0].astype(BF16)) + b_ref[0]


def _modulation(cond, ada_w, ada_b):
    depth, d, n = ada_w.shape
    rows = cond.shape[0]
    tn = MOD_TN
    assert n % tn == 0 and rows % SUBLANES == 0
    return pl.pallas_call(
        _mod_kernel,
        grid=(depth, n // tn),
        in_specs=[
            pl.BlockSpec((rows, d), lambda i, j: (0, 0)),
            pl.BlockSpec((1, d, tn), lambda i, j: (i, 0, j)),
            pl.BlockSpec((1, 1, tn), lambda i, j: (i, 0, j)),
        ],
        out_specs=pl.BlockSpec((1, rows, tn), lambda i, j: (i, 0, j)),
        out_shape=jax.ShapeDtypeStruct((depth, rows, n), F32),
        compiler_params=_params(("parallel", "parallel")),
        name="adaln_mod",
    )(cond, ada_w, ada_b.reshape(depth, 1, n))


def _pre_attn_kernel(x_ref, mod_ref, g1_ref, w_ref, p_ref, gq_ref, gk_ref, cos_ref, sa_ref, sb_ref, *rest,
                     d, nq, nk, nv, rot):
    q_ref, k_ref, v_ref = rest[-3], rest[-2].at[0], rest[-1].at[0]
    x = x_ref[...]
    m = mod_ref[0]
    h = _norm_mod(x, g1_ref[...], m[:, 0:d], m[:, d:2 * d]).astype(BF16)
    cos, sa, sb = cos_ref[...], sa_ref[...], sb_ref[...]
    per = MXU_N // LANES

    def project(grp):
        return _dot(h, w_ref[:, grp * MXU_N:(grp + 1) * MXU_N])

    def emit(grp, y):
        slab0 = grp * per
        if slab0 >= nq + nk:
            j = slab0 - nq - nk
            v_ref[:, j * LANES:(j + per) * LANES] = y.astype(BF16)
            return
        ms = _dot((y * y).astype(BF16), p_ref[...])
        yn = y * lax.rsqrt(ms + NORM_EPS)
        for t in range(per):
            slab = slab0 + t
            out_ref, j, gain = (q_ref, slab, gq_ref) if slab < nq else (k_ref, slab - nq, gk_ref)
            xn = yn[:, t * LANES:(t + 1) * LANES] * gain[...]
            r = xn * cos + pltpu.roll(xn, LANES - rot, 1) * sa + pltpu.roll(xn, rot, 1) * sb
            out_ref[:, j * LANES:(j + 1) * LANES] = r.astype(BF16)

    ngrp = (nq + nk + nv) // per
    y = project(0)
    for grp in range(ngrp):
        y_next = project(grp + 1) if grp + 1 < ngrp else None
        emit(grp, y)
        y = y_next


def _pre_attn(x2d, modx, rows_per_mod, g1, w, pmat, gq, gk, cos, sa, sb, seq, kv_rows, kv_row0, kv_prev,
              *, nq, nk, nv, rot):
    n, d = x2d.shape
    tm = min(TM_PRE, seq)
    assert seq % tm == 0 and rows_per_mod % tm == 0 and kv_row0 % tm == 0
    assert nq % 2 == 0 and nk % 2 == 0 and nv % 2 == 0
    seq_blocks = seq // tm
    mod_blocks = rows_per_mod // tm
    kv_block0 = kv_row0 // tm
    row = lambda c: pl.BlockSpec((tm, c), lambda i: (i, 0))
    tab = pl.BlockSpec((tm, LANES), lambda i: (i % seq_blocks, 0))
    kv = lambda c: pl.BlockSpec((1, tm, c), lambda i: (i // seq_blocks, kv_block0 + i % seq_blocks, 0))
    ins = [x2d, modx, g1, w, pmat, gq, gk, cos, sa, sb]
    specs = [
        row(d),
        pl.BlockSpec((1, 1, 6 * d), lambda i: (i // mod_blocks, 0, 0)),
        _const_spec((1, d)),
        _const_spec(w.shape),
        _const_spec(pmat.shape),
        _const_spec((1, LANES)),
        _const_spec((1, LANES)),
        tab, tab, tab,
    ]
    aliases = {}
    if kv_prev is not None:
        aliases = {len(ins): 1, len(ins) + 1: 2}
        ins += list(kv_prev)
        specs += [pl.BlockSpec(memory_space=pl.ANY)] * 2
    batches = n // seq
    kern = functools.partial(_pre_attn_kernel, d=d, nq=nq, nk=nk, nv=nv, rot=rot)
    return pl.pallas_call(
        kern,
        grid=(n // tm,),
        in_specs=specs,
        out_specs=[row(nq * LANES), kv(nk * LANES), kv(nv * LANES)],
        out_shape=[jax.ShapeDtypeStruct((n, nq * LANES), BF16),
                   jax.ShapeDtypeStruct((batches, kv_rows, nk * LANES), BF16),
                   jax.ShapeDtypeStruct((batches, kv_rows, nv * LANES), BF16)],
        input_output_aliases=aliases,
        compiler_params=_params(("parallel",)),
        name="pre_attn",
    )(*ins)


def _pre_gla_kernel(x_ref, mod_ref, g1_ref, w_ref, w2_ref, gb_ref,
                    q_ref, k_ref, v_ref, r_ref, g_ref, *, d, dk, dv, qscale):
    x = x_ref[...]
    m = mod_ref[0]
    h = _norm_mod(x, g1_ref[...], m[:, 0:d], m[:, d:2 * d]).astype(BF16)
    bounds = (dk, 2 * dk, 2 * dk + dv, 2 * dk + 2 * dv)
    def project(c0):
        return _dot(h, w_ref[:, c0:c0 + MXU_N])

    z = _dot(project(bounds[3]).astype(BF16), w2_ref[...]) + gb_ref[...]
    logsig = jnp.minimum(z, 0.0) - jnp.log(1.0 + jnp.exp(-jnp.abs(z)))
    g_ref[...] = logsig * (1.0 / GLA_TAU)
    y_next = project(0)
    for c0 in range(0, bounds[3], MXU_N):
        y = y_next
        if c0 + MXU_N < bounds[3]:
            y_next = project(c0 + MXU_N)
        if c0 < bounds[0]:
            q_ref[:, c0:c0 + MXU_N] = (y * qscale).astype(BF16)
        elif c0 < bounds[1]:
            k_ref[:, c0 - bounds[0]:c0 - bounds[0] + MXU_N] = y.astype(BF16)
        elif c0 < bounds[2]:
            v_ref[:, c0 - bounds[1]:c0 - bounds[1] + MXU_N] = y.astype(BF16)
        else:
            r_ref[:, c0 - bounds[2]:c0 - bounds[2] + MXU_N] = y.astype(BF16)


def _pre_gla(x2d, modx, rows_per_mod, g1, w, w2, gb, *, dk, dv, qscale):
    n, d = x2d.shape
    tm = min(TM_PRE, n)
    assert n % tm == 0 and rows_per_mod % tm == 0 and dk % MXU_N == 0 and dv % MXU_N == 0
    mod_blocks = rows_per_mod // tm
    row = lambda c: pl.BlockSpec((tm, c), lambda i: (i, 0))
    kern = functools.partial(_pre_gla_kernel, d=d, dk=dk, dv=dv, qscale=qscale)
    return pl.pallas_call(
        kern,
        grid=(n // tm,),
        in_specs=[
            row(d),
            pl.BlockSpec((1, 1, 6 * d), lambda i: (i // mod_blocks, 0, 0)),
            _const_spec((1, d)),
            _const_spec(w.shape),
            _const_spec(w2.shape),
            _const_spec((1, 2 * dk)),
        ],
        out_specs=[row(dk), row(dk), row(dv), row(dv), row(2 * dk)],
        out_shape=[jax.ShapeDtypeStruct((n, c), t) for c, t in
                   ((dk, BF16), (dk, BF16), (dv, BF16), (dv, BF16), (2 * dk, F32))],
        compiler_params=_params(("parallel",)),
        name="pre_gla",
    )(x2d, modx, g1, w, w2, gb)


def _attn_kernel(q_ref, k_ref, v_ref, shift_ref, lam_ref, sub_ref, o_ref, m_sc, l1_sc, l_sc, acc_sc,
                 *maybe_p_sc, diff, tq, tk, tk_exact, lambda_init):
    p_sc = maybe_p_sc[0] if diff else None
    kv_rows = k_ref.shape[1]
    n_steps = kv_rows // tk
    lane = lax.broadcasted_iota(jnp.int32, (1, LANES), 1)
    lo_mask = jnp.where(lane < DIFF_DH, 1.0, 0.0).astype(BF16)
    shift = shift_ref[...]

    def stacked_q(r0):
        q = q_ref[0, pl.ds(r0, tq), :]
        if diff:
            return jnp.concatenate([q * lo_mask, q * (1.0 - lo_mask)], axis=0)
        return jnp.concatenate([q[:, g * LANES:(g + 1) * LANES] for g in range(GQA_GROUP)], axis=0)

    def probabilities(qs, i):
        s = _dot_nt(qs, k_ref[0, i * tk:(i + 1) * tk, :])
        psum, ps = None, []
        for c in range(tk // LANES):
            pc = jnp.exp2(s[:, c * LANES:(c + 1) * LANES] - shift)
            psum = pc if psum is None else psum + pc
            ps.append(pc.astype(BF16))
        return jnp.concatenate(ps, axis=1), psum

    def finish_diff(a, r0):
        ms = jnp.mean(a * a, axis=-1, keepdims=True)
        a = a * lax.rsqrt(ms + NORM_EPS) * sub_ref[...] * (1.0 - lambda_init)
        o_ref[0, pl.ds(r0, tq), :] = a.astype(BF16)

    def finish_gqa(o, r0):
        for g in range(GQA_GROUP):
            o_ref[0, pl.ds(r0, tq), g * LANES:(g + 1) * LANES] = o[g * tq:(g + 1) * tq].astype(BF16)

    if diff:
        lp = lam_ref[...]
        lam = (jnp.exp(jnp.sum(lp[0:1] * lp[1:2], axis=-1, keepdims=True))
               - jnp.exp(jnp.sum(lp[2:3] * lp[3:4], axis=-1, keepdims=True)) + lambda_init)

    def q_block(j, carry):
        r0 = pl.multiple_of(j * tq, tq)
        qs = stacked_q(r0)
        if diff:
            for i in range(n_steps):
                p, psum = probabilities(qs, i)
                p_sc[:, i * tk:(i + 1) * tk] = p
                if i == 0:
                    l_sc[...] = psum
                else:
                    l_sc[...] += psum
            denom = jnp.sum(l_sc[...], axis=-1, keepdims=True)
            ok = jnp.min(denom) >= ATTN_MIN_DENOM

            @pl.when(ok)
            def _():
                l1, l2 = denom[:tq], denom[tq:]
                rho = jnp.broadcast_to(lam * l1 / l2, (tq, LANES)).astype(BF16)
                acc = None
                for i in range(n_steps):
                    mix = [p_sc[0:tq, i * tk + c * LANES:i * tk + (c + 1) * LANES]
                           - rho * p_sc[tq:2 * tq, i * tk + c * LANES:i * tk + (c + 1) * LANES]
                           for c in range(tk // LANES)]
                    pv = _dot(jnp.concatenate(mix, axis=1), v_ref[0, i * tk:(i + 1) * tk, :])
                    acc = pv if acc is None else acc + pv
                finish_diff(acc / l1, r0)
        else:
            for i in range(n_steps):
                p, psum = probabilities(qs, i)
                pv = _dot(p, v_ref[0, i * tk:(i + 1) * tk, :])
                if i == 0:
                    l_sc[...] = psum
                    acc_sc[...] = pv
                else:
                    l_sc[...] += psum
                    acc_sc[...] += pv
            denom = jnp.sum(l_sc[...], axis=-1, keepdims=True)
            finish_gqa(acc_sc[...] / denom, r0)
            ok = jnp.min(denom) >= ATTN_MIN_DENOM

        @pl.when(jnp.logical_not(ok))
        def _():
            m_sc[...] = jnp.full(m_sc.shape, -jnp.inf, F32)
            l1_sc[...] = jnp.zeros(l1_sc.shape, F32)
            acc_sc[...] = jnp.zeros(acc_sc.shape, F32)

            def body(i, c):
                off = pl.multiple_of(i * tk_exact, tk_exact)
                s = _dot_nt(qs, k_ref[0, pl.ds(off, tk_exact), :])
                m_old = m_sc[...]
                m_new = jnp.maximum(m_old, jnp.max(s, axis=-1, keepdims=True))
                alpha = jnp.exp2(m_old - m_new)
                p = jnp.exp2(s - m_new)
                l1_sc[...] = alpha * l1_sc[...] + jnp.sum(p, axis=-1, keepdims=True)
                acc_sc[...] = alpha * acc_sc[...] + _dot(p.astype(BF16), v_ref[0, pl.ds(off, tk_exact), :])
                m_sc[...] = m_new
                return c

            lax.fori_loop(0, kv_rows // tk_exact, body, 0)
            o = acc_sc[...] / l1_sc[...]
            if diff:
                finish_diff(o[:tq] - lam * o[tq:], r0)
            else:
                finish_gqa(o, r0)

        return carry

    lax.fori_loop(0, q_ref.shape[1] // tq, q_block, 0)


def _attention(q, k, v, shift, lamp, subln, *, diff, lambda_init, kv_rows, kv_block):
    b, t, _ = q.shape
    tq = min(TQ_DIFF if diff else TQ_GQA, t)
    assert t % tq == 0 and kv_rows % LANES == 0
    qw = LANES if diff else GQA_GROUP * LANES
    heads = k.shape[2] // LANES
    stack = (2 if diff else GQA_GROUP) * tq
    steps = [c for c in range(LANES, min(TK_ATTN, kv_rows) + 1, LANES) if kv_rows % c == 0]
    tk = max([c for c in steps if c % MXU_N == 0] or steps)
    kv_spec = pl.BlockSpec((1, kv_rows, LANES), lambda bi, h: (bi, kv_block, h))
    q_spec = pl.BlockSpec((1, t, qw), lambda bi, h: (bi, 0, h))
    ins = [q, k, v, shift, lamp, subln]
    specs = [q_spec, kv_spec, kv_spec]
    specs += [pl.BlockSpec(a.shape, lambda bi, h: (0, 0)) for a in (shift, lamp, subln)]
    tk_exact = max(c for c in steps if c <= TK_EXACT)
    kern = functools.partial(_attn_kernel, diff=diff, tq=tq, tk=tk, tk_exact=tk_exact, lambda_init=lambda_init)
    scratch = [pltpu.VMEM((stack, 1), F32), pltpu.VMEM((stack, 1), F32),
               pltpu.VMEM((stack, LANES), F32), pltpu.VMEM((stack, LANES), F32)]
    if diff:
        scratch.append(pltpu.VMEM((stack, kv_rows), BF16))
    return pl.pallas_call(
        kern,
        grid=(b, heads),
        in_specs=specs,
        out_specs=q_spec,
        out_shape=jax.ShapeDtypeStruct((b, t, heads * qw), BF16),
        scratch_shapes=scratch,
        compiler_params=_params(("parallel", "parallel")),
        name="attn_diff" if diff else "attn_gqa",
    )(*ins)


def _gla_cumsum(g_ref, *, nchunk, rev):
    c = GLA_CHUNK
    rows = nchunk * c
    ri = lax.broadcasted_iota(jnp.int32, (rows, rows), 0)
    ci = lax.broadcasted_iota(jnp.int32, (rows, rows), 1)
    causal = (ci >= ri) if rev else (ci <= ri)
    tri = jnp.where(jnp.logical_and(ri // c == ci // c, causal), 1.0, 0.0).astype(BF16)
    hi, mid, lo = _split3(g_ref[0])
    return _dot(tri, hi) + _dot(tri, mid) + _dot(tri, lo)


def _gla_sub_ref(bs, i, rev):
    nsub = GLA_CHUNK // GLA_SUB
    if rev:
        return bs[(i + 1) * GLA_SUB:(i + 1) * GLA_SUB + 1] if i < nsub - 1 else jnp.zeros_like(bs[0:1])
    return bs[i * GLA_SUB - 1:i * GLA_SUB] if i > 0 else jnp.zeros_like(bs[0:1])


def _gla_max_diag_exponent(bsum, *, nchunk, rev):
    c = GLA_CHUNK
    worst = None
    for n in range(nchunk):
        bs = bsum[n * c:(n + 1) * c]
        for i in range(c // GLA_SUB):
            far = i * GLA_SUB if rev else (i + 1) * GLA_SUB - 1
            e = _gla_sub_ref(bs, i, rev) - bs[far:far + 1]
            worst = e if worst is None else jnp.maximum(worst, e)
    return jnp.max(worst, axis=-1, keepdims=True)


def _gla_local(q_ref, k_ref, v_ref, bsum, *, nchunk, dkh, dvh, rev, exact_diag):
    c = GLA_CHUNK
    qf, kf = q_ref[0].astype(F32), k_ref[0].astype(F32)
    qt = (qf * jnp.exp(bsum)).astype(BF16)

    krow = lax.broadcasted_iota(jnp.int32, (c, bsum.shape[1]), 0)
    r64 = lax.broadcasted_iota(jnp.int32, (c, c), 0)
    c64 = lax.broadcasted_iota(jnp.int32, (c, c), 1)
    keep = (c64 >= r64) if rev else (c64 <= r64)
    if exact_diag:
        keep = (c64 // GLA_SUB > r64 // GLA_SUB) if rev else (c64 // GLA_SUB < r64 // GLA_SUB)
    rs = lax.broadcasted_iota(jnp.int32, (GLA_SUB, c), 0)
    cs = lax.broadcasted_iota(jnp.int32, (GLA_SUB, c), 1)
    nsub = c // GLA_SUB
    hks = [slice(h * dkh, (h + 1) * dkh) for h in range(GLA_HEADS)]
    scaled, diag = [], {}
    for n in range(nchunk):
        bs, kc, qc = bsum[n * c:(n + 1) * c], kf[n * c:(n + 1) * c], qf[n * c:(n + 1) * c]
        b_tot = bs[0:1] if rev else bs[c - 1:c]
        kd = (kc * jnp.exp(b_tot - bs)).astype(BF16)
        qis, kis = [], []
        for i in range(nsub):
            lo_r, hi_r = i * GLA_SUB, (i + 1) * GLA_SUB
            ref = _gla_sub_ref(bs, i, rev)
            if exact_diag:
                valid = (krow >= hi_r) if rev else (krow < lo_r)
            else:
                valid = (krow >= lo_r) if rev else (krow < hi_r)
            kis.append((kc * jnp.exp(jnp.where(valid, ref - bs, 0.0))).astype(BF16))
            qis.append((qc[lo_r:hi_r] * jnp.exp(bs[lo_r:hi_r] - ref)).astype(BF16))
            if exact_diag:
                cols = [jnp.zeros((GLA_SUB, c), F32) for _ in range(GLA_HEADS)]
                for s in range(lo_r, hi_r):
                    e = jnp.exp(jnp.minimum(bs[lo_r:hi_r] - bs[s:s + 1], 0.0))
                    prod = qc[lo_r:hi_r] * e * kc[s:s + 1]
                    pair = (rs + lo_r <= s) if rev else (rs + lo_r >= s)
                    for h in range(GLA_HEADS):
                        col = jnp.sum(prod[:, hks[h]], axis=-1, keepdims=True)
                        cols[h] = cols[h] + jnp.where(jnp.logical_and(cs == s, pair), col, 0.0)
                for h in range(GLA_HEADS):
                    diag[n, h, i] = cols[h]
        scaled.append((jnp.exp(b_tot), kd, qis, kis))

    pairs = [(n, h) for n in range(nchunk) for h in range(GLA_HEADS)]
    vals = {(n, h): v_ref[0, n * c:(n + 1) * c, h * dvh:(h + 1) * dvh] for n, h in pairs}
    blocks = {(n, h): [_dot_nt(scaled[n][2][i][:, hks[h]], scaled[n][3][i][:, hks[h]]) for i in range(nsub)]
              for n, h in pairs}
    inc = {(n, h): _dot_tn(vals[n, h], scaled[n][1][:, hks[h]]) for n, h in pairs}
    amat = {}
    for n, h in pairs:
        a = jnp.where(keep, jnp.concatenate(blocks[n, h], axis=0), 0.0)
        if exact_diag:
            a = a + jnp.concatenate([diag[n, h, i] for i in range(nsub)], axis=0)
        amat[n, h] = a.astype(BF16)
    intra = {p: _dot(amat[p], vals[p]) for p in pairs}
    local = {(n, h): (scaled[n][0][:, hks[h]], intra[n, h], inc[n, h]) for n, h in pairs}
    return qt, local


def _gla_kernel(qf_ref, kf_ref, vf_ref, gf_ref, qb_ref, kb_ref, vb_ref, gb_ref, s0_ref,
                of_ref, ob_ref, st_out_ref, st_sc, *, nchunk, dkh, dvh):
    j = pl.program_id(1)

    @pl.when(j == 0)
    def _():
        st_sc[...] = s0_ref[0]

    c = GLA_CHUNK
    bf = _gla_cumsum(gf_ref, nchunk=nchunk, rev=False)
    bb = _gla_cumsum(gb_ref, nchunk=nchunk, rev=True)
    worst = jnp.maximum(_gla_max_diag_exponent(bf, nchunk=nchunk, rev=False),
                        _gla_max_diag_exponent(bb, nchunk=nchunk, rev=True))
    factored_ok = jnp.max(worst) <= GLA_MAX_EXPONENT

    def sweep(exact_diag):
        kw = dict(nchunk=nchunk, dkh=dkh, dvh=dvh, exact_diag=exact_diag)
        sides = ((_gla_local(qf_ref, kf_ref, vf_ref, bf, rev=False, **kw), of_ref),
                 (_gla_local(qb_ref, kb_ref, vb_ref, bb, rev=True, **kw), ob_ref))
        for step in range(nchunk):
            for side, ((qt, local), o_ref) in enumerate(sides):
                n = step if side == 0 else nchunk - 1 - step
                for h in range(GLA_HEADS):
                    hk, hv = slice(h * dkh, (h + 1) * dkh), slice(h * dvh, (h + 1) * dvh)
                    dec, intra, inc = local[n, h]
                    st = st_sc[side, h]
                    o = _dot_nt(qt[n * c:(n + 1) * c, hk], st.astype(BF16)) + intra
                    o_ref[0, n * c:(n + 1) * c, hv] = o.astype(BF16)
                    st_sc[side, h] = dec * st + inc

    @pl.when(factored_ok)
    def _():
        sweep(False)

    @pl.when(jnp.logical_not(factored_ok))
    def _():
        sweep(True)

    @pl.when(j == pl.num_programs(1) - 1)
    def _():
        st_out_ref[0] = st_sc[...]


def _gla_scan(q, k, v, g, s0):
    b, t, dk = q.shape
    dv = v.shape[2]
    dkh, dvh = dk // GLA_HEADS, dv // GLA_HEADS
    nchunk = min(GLA_R, t // GLA_CHUNK)
    rows = nchunk * GLA_CHUNK
    assert t % rows == 0
    n = t // rows
    fwd = lambda w, cb=0: pl.BlockSpec((1, rows, w), lambda bi, j: (bi, j, cb))
    bwd = lambda w, cb=0: pl.BlockSpec((1, rows, w), lambda bi, j: (bi, n - 1 - j, cb))
    st_spec = pl.BlockSpec((1, 2, GLA_HEADS, dvh, dkh), lambda bi, j: (bi, 0, 0, 0, 0))
    kern = functools.partial(_gla_kernel, nchunk=nchunk, dkh=dkh, dvh=dvh)
    return pl.pallas_call(
        kern,
        grid=(b, n),
        in_specs=[fwd(dk), fwd(dk), fwd(dv), fwd(dk, 0), bwd(dk), bwd(dk), bwd(dv), bwd(dk, 1), st_spec],
        out_specs=[fwd(dv), bwd(dv), st_spec],
        out_shape=[jax.ShapeDtypeStruct((b, t, dv), BF16), jax.ShapeDtypeStruct((b, t, dv), BF16),
                   jax.ShapeDtypeStruct(s0.shape, F32)],
        scratch_shapes=[pltpu.VMEM((2, GLA_HEADS, dvh, dkh), F32)],
        compiler_params=_params(("parallel", "arbitrary")),
        name="gla_scan",
    )(q, k, v, g, q, k, v, g, s0)


def _post_kernel(*refs, d, f, gla, dvh):
    if gla:
        x_ref, of_ref, ob_ref, r_ref, on_ref, mod_ref, wo_ref, g2_ref, wgu_ref, wd_ref, o_ref = refs
        r = r_ref[...].astype(F32)
        gate = r * _sigmoid(r)
        parts = []
        for h in range(GLA_HEADS):
            cs = slice(h * dvh, (h + 1) * dvh)
            oh = of_ref[:, cs].astype(F32) + ob_ref[:, cs].astype(F32)
            ms = jnp.mean(oh * oh, axis=-1, keepdims=True)
            parts.append((oh * lax.rsqrt(ms + NORM_EPS) * on_ref[...] * gate[:, cs]).astype(BF16))
        a = jnp.concatenate(parts, axis=1)
    else:
        x_ref, a_ref, mod_ref, wo_ref, g2_ref, wgu_ref, wd_ref, o_ref = refs
        a = a_ref[...]
    m = mod_ref[0]
    gate1, shift2, scale2, gate2 = (m[:, 2 * d:3 * d], m[:, 3 * d:4 * d], m[:, 4 * d:5 * d], m[:, 5 * d:6 * d])
    x1 = x_ref[...] + gate1 * _dot(a, wo_ref[...])
    h2 = _norm_mod(x1, g2_ref[...], shift2, scale2).astype(BF16)
    acts = []
    for c0 in range(0, f, FFN_CHUNK):
        gt = _dot(h2, wgu_ref[:, c0:c0 + FFN_CHUNK])
        up = _dot(h2, wgu_ref[:, f + c0:f + c0 + FFN_CHUNK])
        acts.append((gt * _sigmoid(gt) * up).astype(BF16))
    o_ref[...] = x1 + gate2 * _dot(jnp.concatenate(acts, axis=1), wd_ref[...])


def _post(x2d, mix, modx, rows_per_mod, wo, g2, wgu, wd, *, gla_norm=None):
    n, d = x2d.shape
    f = wd.shape[0]
    tm = min(TM_POST, n)
    assert n % tm == 0 and rows_per_mod % tm == 0 and f % FFN_CHUNK == 0
    mod_blocks = rows_per_mod // tm
    row = lambda c: pl.BlockSpec((tm, c), lambda i: (i, 0))
    gla = gla_norm is not None
    if gla:
        ins = [x2d, *mix, gla_norm]
        specs = [row(d), row(d), row(d), row(d), _const_spec(gla_norm.shape)]
        dvh = gla_norm.shape[1]
    else:
        ins = [x2d, mix]
        specs = [row(d), row(d)]
        dvh = 0
    ins += [modx, wo, g2, wgu, wd]
    specs += [pl.BlockSpec((1, 1, 6 * d), lambda i: (i // mod_blocks, 0, 0)),
              _const_spec(wo.shape), _const_spec((1, d)), _const_spec(wgu.shape), _const_spec(wd.shape)]
    kern = functools.partial(_post_kernel, d=d, f=f, gla=gla, dvh=dvh)
    return pl.pallas_call(
        kern,
        grid=(n // tm,),
        in_specs=specs,
        out_specs=row(d),
        out_shape=jax.ShapeDtypeStruct((n, d), F32),
        compiler_params=_params(("parallel",)),
        name="post_gla" if gla else "post_attn",
    )(*ins)


def _rope_tables(n_tokens, head_dim):
    rows = n_tokens // GRID_W
    rowp = jnp.broadcast_to(jnp.arange(rows)[:, None], (rows, GRID_W)).reshape(-1)
    colp = jnp.broadcast_to(jnp.arange(GRID_W)[None, :], (rows, GRID_W)).reshape(-1)
    half = head_dim // 2
    quarter = head_dim // 4
    inv = ROPE_BASE ** (-jnp.arange(0, half, 2, dtype=F32) / half)

    def axis_angles(pos):
        a = pos.astype(F32)[:, None] * inv[None, :]
        return jnp.concatenate([a, a], axis=-1)

    ang = jnp.concatenate([axis_angles(rowp), axis_angles(colp)], axis=-1)
    reps = LANES // head_dim
    cos = jnp.tile(jnp.cos(ang), (1, reps))
    sin = jnp.tile(jnp.sin(ang), (1, reps))
    first = (jnp.arange(LANES) % (2 * quarter)) < quarter
    sa = jnp.where(first[None, :], -sin, 0.0)
    sb = jnp.where(first[None, :], 0.0, sin)
    return cos, sa, sb


def _seg_matrix(seg):
    idx = jnp.arange(MXU_N) // seg
    return jnp.where(idx[:, None] == idx[None, :], 1.0 / seg, 0.0).astype(BF16)


def kernel(x, c, ctx, c_ctx, ada_w, ada_b, norm1_g, norm2_g, ffn_w_gu, ffn_w_down, diff_w_in, diff_w_out, diff_q_norm, diff_k_norm, diff_lambda_q1, diff_lambda_k1, diff_lambda_q2, diff_lambda_k2, diff_subln, gqa_w_in, gqa_w_out, gqa_q_norm, gqa_k_norm, gla_w_in, gla_gate_w1_fwd, gla_gate_w2_fwd, gla_gate_b_fwd, gla_gate_w1_bwd, gla_gate_w2_bwd, gla_gate_b_bwd, gla_out_norm, gla_w_out):
    b, l, d = x.shape
    lc = ctx.shape[1]
    depth = ada_w.shape[0]
    nl, nc = b * l, b * lc

    assert l % lc == 0 and l % GRID_W == 0 and lc % GLA_CHUNK == 0
    cond_rows = pl.cdiv(b + 1, SUBLANES) * SUBLANES
    cond = jnp.zeros((cond_rows, d), F32).at[:b].set(c).at[b].set(c_ctx)
    mod = _modulation(cond, ada_w, ada_b)

    xl = x.reshape(nl, d)
    xc = ctx.reshape(nc, d)
    ones_tab = (jnp.ones((lc, LANES), F32), jnp.zeros((lc, LANES), F32), jnp.zeros((lc, LANES), F32))
    dummy_lam = jnp.zeros((4, DIFF_DH), F32)
    dummy_sub = jnp.ones((1, LANES), F32)

    for i in range(depth):
        last = i == depth - 1
        kind, j = i % N_MIXERS, i // N_MIXERS
        modl = mod[i, :b].reshape(b, 1, 6 * d)
        modc = mod[i, b:b + 1].reshape(1, 1, 6 * d)
        g1 = norm1_g[i].reshape(1, d)
        g2 = norm2_g[i].reshape(1, d)
        wgu = ffn_w_gu[i].astype(BF16)
        wd = ffn_w_down[i].astype(BF16)

        if kind in (0, 1):
            if kind == 0:
                diff, dh = True, DIFF_DH
                w_in, w_out = diff_w_in[j].astype(BF16), diff_w_out[j].astype(BF16)
                gq = jnp.tile(diff_q_norm[j], LANES // dh).reshape(1, LANES) * (dh ** -0.5 * LOG2E)
                gk = jnp.tile(diff_k_norm[j], LANES // dh).reshape(1, LANES)
                nq = nk = nv = d // LANES
                lamp = jnp.stack([diff_lambda_q1[j], diff_lambda_k1[j], diff_lambda_q2[j], diff_lambda_k2[j]])
                subln = diff_subln[j].reshape(1, LANES)
                lambda_init = 0.8 - 0.6 * math.exp(-0.3 * i)
            else:
                diff, dh = False, GQA_DH
                w_in, w_out = gqa_w_in[j].astype(BF16), gqa_w_out[j].astype(BF16)
                gq = gqa_q_norm[j].reshape(1, LANES) * (dh ** -0.5 * LOG2E)
                gk = gqa_k_norm[j].reshape(1, LANES)
                nq = d // LANES
                nk = nv = (w_in.shape[1] - d) // (2 * LANES)
                lamp, subln, lambda_init = dummy_lam, dummy_sub, 0.0
            pmat = _seg_matrix(dh)
            cos, sa, sb = _rope_tables(l, dh)
            kw = dict(nq=nq, nk=nk, nv=nv, rot=dh // 4)
            ql, k_all, v_all = _pre_attn(xl, modl, l, g1, w_in, pmat, gq, gk, cos, sa, sb, l, l + lc, 0, None, **kw)
            qc, k_all, v_all = _pre_attn(xc, modc, nc, g1, w_in, pmat, gq, gk, *ones_tab, lc, l + lc, l,
                                         (k_all, v_all), **kw)
            r3 = lambda t, n: t.reshape(b, n, t.shape[1])
            shift = jnp.full((1, LANES), dh, F32) * jnp.max(jnp.abs(gq)) * jnp.max(jnp.abs(gk))
            akw = dict(diff=diff, lambda_init=lambda_init)
            ol = _attention(r3(ql, l), k_all, v_all, shift, lamp, subln, kv_rows=l + lc, kv_block=0, **akw)
            xl = _post(xl, ol.reshape(nl, d), modl, l, w_out, g2, wgu, wd)
            if not last:
                oc = _attention(r3(qc, lc), k_all, v_all, shift, lamp, subln, kv_rows=lc, kv_block=l // lc, **akw)
                xc = _post(xc, oc.reshape(nc, d), modc, nc, w_out, g2, wgu, wd)
        else:
            dk, dv = gla_gate_w2_fwd.shape[2], gla_w_out.shape[1]
            rank = gla_gate_w1_fwd.shape[2]
            w1 = jnp.zeros((d, MXU_N), F32).at[:, :rank].set(gla_gate_w1_fwd[j]).at[:, rank:2 * rank].set(gla_gate_w1_bwd[j])
            w_in = jnp.concatenate([gla_w_in[j], w1], axis=1).astype(BF16)
            w2 = (jnp.zeros((MXU_N, 2 * dk), F32).at[:rank, :dk].set(gla_gate_w2_fwd[j])
                  .at[rank:2 * rank, dk:].set(gla_gate_w2_bwd[j])).astype(BF16)
            gb = jnp.concatenate([gla_gate_b_fwd[j], gla_gate_b_bwd[j]]).reshape(1, 2 * dk)
            w_out = gla_w_out[j].astype(BF16)
            dkh, dvh = dk // GLA_HEADS, dv // GLA_HEADS
            onorm = gla_out_norm[j].reshape(1, dvh)
            kw = dict(dk=dk, dv=dv, qscale=dkh ** -0.5)
            ql, kl, vl, rl, gl = _pre_gla(xl, modl, l, g1, w_in, w2, gb, **kw)
            qc, kc, vc, rc, gc = _pre_gla(xc, modc, nc, g1, w_in, w2, gb, **kw)
            r3 = lambda t, n: t.reshape(b, n, t.shape[1])
            s0 = jnp.zeros((b, 2, GLA_HEADS, dvh, dkh), F32)
            ocf, ocb, sc = _gla_scan(r3(qc, lc), r3(kc, lc), r3(vc, lc), r3(gc, lc), s0)
            olf, olb, _ = _gla_scan(r3(ql, l), r3(kl, l), r3(vl, l), r3(gl, l), sc)
            xl = _post(xl, (olf.reshape(nl, dv), olb.reshape(nl, dv), rl), modl, l, w_out, g2, wgu, wd, gla_norm=onorm)
            if not last:
                xc = _post(xc, (ocf.reshape(nc, dv), ocb.reshape(nc, dv), rc), modc, nc, w_out, g2, wgu, wd, gla_norm=onorm)
    return xl.reshape(b, l, d)
```

```python
import functools
import math

import jax
import jax.numpy as jnp
from jax import lax
from jax.experimental import pallas as pl
from jax.experimental.pallas import tpu as pltpu

F32 = jnp.float32
BF16 = jnp.bfloat16

GRID_W = 64
N_MIXERS = 3
ROPE_BASE = 10000.0
NORM_EPS = 1e-6
DIFF_DH = 64
GQA_DH = 128
GQA_GROUP = 4
GLA_HEADS = 4
GLA_GATE_RANK = 16
GLA_TAU = 16.0
GLA_CHUNK = 64
GLA_SUB = 16
LOG2E = math.log2(math.e)
ATTN_MIN_DENOM = 1e-22
GLA_MAX_EXPONENT = 80.0

LANES = 128
SUBLANES = 8
MXU_N = 256
VMEM_LIMIT = 56 * 1024 * 1024

MOD_TN = 1536
TM_PRE = 512
TM_POST = 512
FFN_CHUNK = 256
TQ_DIFF = 1024
TQ_GQA = 512
TK_ATTN = 3072
TK_ATTN_DIFF = 1024
TK_EXACT = 768
GLA_R = 4


def _const_spec(shape):
    nd = len(shape)
    return pl.BlockSpec(shape, lambda *_: (0,) * nd, pipeline_mode=pl.Buffered(1))


def _params(sem, vmem=VMEM_LIMIT):
    return pltpu.CompilerParams(dimension_semantics=sem, vmem_limit_bytes=vmem)


def _sigmoid(x):
    return 1.0 / (1.0 + jnp.exp(-x))


def _norm_mod(x, g, shift, scale):
    ms = jnp.mean(x * x, axis=-1, keepdims=True)
    return (x * lax.rsqrt(ms + NORM_EPS) * g) * (1.0 + scale) + shift


def _dot(a, b):
    return jnp.dot(a, b, preferred_element_type=F32)


def _dot_nt(a, b):
    return lax.dot_general(a, b, (((1,), (1,)), ((), ())), preferred_element_type=F32)


def _dot_tn(a, b):
    return lax.dot_general(a, b, (((0,), (0,)), ((), ())), preferred_element_type=F32)


def _split3(x):
    hi = x.astype(BF16)
    r1 = x - hi.astype(F32)
    mid = r1.astype(BF16)
    lo = (r1 - mid.astype(F32)).astype(BF16)
    return hi, mid, lo


def _mod_kernel(cond_ref, w_ref, b_ref, o_ref):
    c = cond_ref[...]
    s = (c * _sigmoid(c)).astype(BF16)
    o_ref[0] = _dot(s, w_ref[0].astype(BF16)) + b_ref[0]


def _modulation(cond, ada_w, ada_b):
    depth, d, n = ada_w.shape
    rows = cond.shape[0]
    tn = MOD_TN
    assert n % tn == 0 and rows % SUBLANES == 0
    return pl.pallas_call(
        _mod_kernel,
        grid=(depth, n // tn),
        in_specs=[
            pl.BlockSpec((rows, d), lambda i, j: (0, 0)),
            pl.BlockSpec((1, d, tn), lambda i, j: (i, 0, j)),
            pl.BlockSpec((1, 1, tn), lambda i, j: (i, 0, j)),
        ],
        out_specs=pl.BlockSpec((1, rows, tn), lambda i, j: (i, 0, j)),
        out_shape=jax.ShapeDtypeStruct((depth, rows, n), F32),
        compiler_params=_params(("parallel", "parallel")),
        name="adaln_mod",
    )(cond, ada_w, ada_b.reshape(depth, 1, n))


def _pre_attn_kernel(x_ref, mod_ref, g1_ref, w_ref, p_ref, gq_ref, gk_ref, cos_ref, sa_ref, sb_ref, *rest,
                     d, nq, nk, nv, rot):
    q_ref, k_ref, v_ref = rest[-3], rest[-2].at[0], rest[-1].at[0]
    x = x_ref[...]
    m = mod_ref[0]
    h = _norm_mod(x, g1_ref[...], m[:, 0:d], m[:, d:2 * d]).astype(BF16)
    cos, sa, sb = cos_ref[...], sa_ref[...], sb_ref[...]
    per = MXU_N // LANES

    def project(grp):
        return _dot(h, w_ref[:, grp * MXU_N:(grp + 1) * MXU_N])

    def emit(grp, y):
        slab0 = grp * per
        if slab0 >= nq + nk:
            j = slab0 - nq - nk
            v_ref[:, j * LANES:(j + per) * LANES] = y.astype(BF16)
            return
        ms = _dot((y * y).astype(BF16), p_ref[...])
        yn = y * lax.rsqrt(ms + NORM_EPS)
        for t in range(per):
            slab = slab0 + t
            out_ref, j, gain = (q_ref, slab, gq_ref) if slab < nq else (k_ref, slab - nq, gk_ref)
            xn = yn[:, t * LANES:(t + 1) * LANES] * gain[...]
            r = xn * cos + pltpu.roll(xn, LANES - rot, 1) * sa + pltpu.roll(xn, rot, 1) * sb
            out_ref[:, j * LANES:(j + 1) * LANES] = r.astype(BF16)

    ngrp = (nq + nk + nv) // per
    y = project(0)
    for grp in range(ngrp):
        y_next = project(grp + 1) if grp + 1 < ngrp else None
        emit(grp, y)
        y = y_next


def _pre_attn(x2d, modx, rows_per_mod, g1, w, pmat, gq, gk, cos, sa, sb, seq, kv_rows, kv_row0, kv_prev,
              *, nq, nk, nv, rot):
    n, d = x2d.shape
    tm = min(TM_PRE, seq)
    assert seq % tm == 0 and rows_per_mod % tm == 0 and kv_row0 % tm == 0
    assert nq % 2 == 0 and nk % 2 == 0 and nv % 2 == 0
    seq_blocks = seq // tm
    mod_blocks = rows_per_mod // tm
    kv_block0 = kv_row0 // tm
    row = lambda c: pl.BlockSpec((tm, c), lambda i: (i, 0))
    tab = pl.BlockSpec((tm, LANES), lambda i: (i % seq_blocks, 0))
    kv = lambda c: pl.BlockSpec((1, tm, c), lambda i: (i // seq_blocks, kv_block0 + i % seq_blocks, 0))
    ins = [x2d, modx, g1, w, pmat, gq, gk, cos, sa, sb]
    specs = [
        row(d),
        pl.BlockSpec((1, 1, 6 * d), lambda i: (i // mod_blocks, 0, 0)),
        _const_spec((1, d)),
        _const_spec(w.shape),
        _const_spec(pmat.shape),
        _const_spec((1, LANES)),
        _const_spec((1, LANES)),
        tab, tab, tab,
    ]
    aliases = {}
    if kv_prev is not None:
        aliases = {len(ins): 1, len(ins) + 1: 2}
        ins += list(kv_prev)
        specs += [pl.BlockSpec(memory_space=pl.ANY)] * 2
    batches = n // seq
    kern = functools.partial(_pre_attn_kernel, d=d, nq=nq, nk=nk, nv=nv, rot=rot)
    return pl.pallas_call(
        kern,
        grid=(n // tm,),
        in_specs=specs,
        out_specs=[row(nq * LANES), kv(nk * LANES), kv(nv * LANES)],
        out_shape=[jax.ShapeDtypeStruct((n, nq * LANES), BF16),
                   jax.ShapeDtypeStruct((batches, kv_rows, nk * LANES), BF16),
                   jax.ShapeDtypeStruct((batches, kv_rows, nv * LANES), BF16)],
        input_output_aliases=aliases,
        compiler_params=_params(("parallel",)),
        name="pre_attn",
    )(*ins)


def _pre_gla_kernel(x_ref, mod_ref, g1_ref, w_ref, w2_ref, gb_ref,
                    q_ref, k_ref, v_ref, r_ref, g_ref, *, d, dk, dv, qscale):
    x = x_ref[...]
    m = mod_ref[0]
    h = _norm_mod(x, g1_ref[...], m[:, 0:d], m[:, d:2 * d]).astype(BF16)
    bounds = (dk, 2 * dk, 2 * dk + dv, 2 * dk + 2 * dv)
    def project(c0):
        return _dot(h, w_ref[:, c0:c0 + MXU_N])

    z = _dot(project(bounds[3]).astype(BF16), w2_ref[...]) + gb_ref[...]
    logsig = jnp.minimum(z, 0.0) - jnp.log(1.0 + jnp.exp(-jnp.abs(z)))
    g_ref[...] = logsig * (1.0 / GLA_TAU)
    y_next = project(0)
    for c0 in range(0, bounds[3], MXU_N):
        y = y_next
        if c0 + MXU_N < bounds[3]:
            y_next = project(c0 + MXU_N)
        if c0 < bounds[0]:
            q_ref[:, c0:c0 + MXU_N] = (y * qscale).astype(BF16)
        elif c0 < bounds[1]:
            k_ref[:, c0 - bounds[0]:c0 - bounds[0] + MXU_N] = y.astype(BF16)
        elif c0 < bounds[2]:
            v_ref[:, c0 - bounds[1]:c0 - bounds[1] + MXU_N] = y.astype(BF16)
        else:
            r_ref[:, c0 - bounds[2]:c0 - bounds[2] + MXU_N] = y.astype(BF16)


def _pre_gla(x2d, modx, rows_per_mod, g1, w, w2, gb, *, dk, dv, qscale):
    n, d = x2d.shape
    tm = min(TM_PRE, n)
    assert n % tm == 0 and rows_per_mod % tm == 0 and dk % MXU_N == 0 and dv % MXU_N == 0
    mod_blocks = rows_per_mod // tm
    row = lambda c: pl.BlockSpec((tm, c), lambda i: (i, 0))
    kern = functools.partial(_pre_gla_kernel, d=d, dk=dk, dv=dv, qscale=qscale)
    return pl.pallas_call(
        kern,
        grid=(n // tm,),
        in_specs=[
            row(d),
            pl.BlockSpec((1, 1, 6 * d), lambda i: (i // mod_blocks, 0, 0)),
            _const_spec((1, d)),
            _const_spec(w.shape),
            _const_spec(w2.shape),
            _const_spec((1, 2 * dk)),
        ],
        out_specs=[row(dk), row(dk), row(dv), row(dv), row(2 * dk)],
        out_shape=[jax.ShapeDtypeStruct((n, c), t) for c, t in
                   ((dk, BF16), (dk, BF16), (dv, BF16), (dv, BF16), (2 * dk, F32))],
        compiler_params=_params(("parallel",)),
        name="pre_gla",
    )(x2d, modx, g1, w, w2, gb)


def _attn_kernel(q_ref, k_ref, v_ref, shift_ref, lam_ref, sub_ref, o_ref, m_sc, l1_sc, l_sc, acc_sc,
                 *maybe_p_sc, diff, tq, tk, tk_exact, lambda_init):
    p_sc = maybe_p_sc[0] if diff else None
    kv_rows = k_ref.shape[1]
    n_steps = kv_rows // tk
    lane = lax.broadcasted_iota(jnp.int32, (1, LANES), 1)
    lo_mask = jnp.where(lane < DIFF_DH, 1.0, 0.0).astype(BF16)
    shift = shift_ref[...]

    def stacked_q(r0):
        q = q_ref[0, pl.ds(r0, tq), :]
        if diff:
            return jnp.concatenate([q * lo_mask, q * (1.0 - lo_mask)], axis=0)
        return jnp.concatenate([q[:, g * LANES:(g + 1) * LANES] for g in range(GQA_GROUP)], axis=0)

    def probabilities(qs, i):
        s = _dot_nt(qs, k_ref[0, i * tk:(i + 1) * tk, :])
        psum, ps = None, []
        for c in range(tk // LANES):
            pc = jnp.exp2(s[:, c * LANES:(c + 1) * LANES] - shift)
            psum = pc if psum is None else psum + pc
            ps.append(pc.astype(BF16))
        return jnp.concatenate(ps, axis=1), psum

    def finish_diff(a, r0):
        ms = jnp.mean(a * a, axis=-1, keepdims=True)
        a = a * lax.rsqrt(ms + NORM_EPS) * sub_ref[...] * (1.0 - lambda_init)
        o_ref[0, pl.ds(r0, tq), :] = a.astype(BF16)

    def finish_gqa(o, r0):
        for g in range(GQA_GROUP):
            o_ref[0, pl.ds(r0, tq), g * LANES:(g + 1) * LANES] = o[g * tq:(g + 1) * tq].astype(BF16)

    if diff:
        lp = lam_ref[...]
        lam = (jnp.exp(jnp.sum(lp[0:1] * lp[1:2], axis=-1, keepdims=True))
               - jnp.exp(jnp.sum(lp[2:3] * lp[3:4], axis=-1, keepdims=True)) + lambda_init)

    def q_block(j, carry):
        r0 = pl.multiple_of(j * tq, tq)
        qs = stacked_q(r0)
        if diff:
            for i in range(n_steps):
                p, psum = probabilities(qs, i)
                p_sc[:, i * tk:(i + 1) * tk] = p
                if i == 0:
                    l_sc[...] = psum
                else:
                    l_sc[...] += psum
            denom = jnp.sum(l_sc[...], axis=-1, keepdims=True)
            ok = jnp.min(denom) >= ATTN_MIN_DENOM

            @pl.when(ok)
            def _():
                l1, l2 = denom[:tq], denom[tq:]
                rho = jnp.broadcast_to(lam * l1 / l2, (tq, LANES)).astype(BF16)
                acc = None
                for i in range(n_steps):
                    mix = [p_sc[0:tq, i * tk + c * LANES:i * tk + (c + 1) * LANES]
                           - rho * p_sc[tq:2 * tq, i * tk + c * LANES:i * tk + (c + 1) * LANES]
                           for c in range(tk // LANES)]
                    pv = _dot(jnp.concatenate(mix, axis=1), v_ref[0, i * tk:(i + 1) * tk, :])
                    acc = pv if acc is None else acc + pv
                finish_diff(acc / l1, r0)
        else:
            for i in range(n_steps):
                p, psum = probabilities(qs, i)
                pv = _dot(p, v_ref[0, i * tk:(i + 1) * tk, :])
                if i == 0:
                    l_sc[...] = psum
                    acc_sc[...] = pv
                else:
                    l_sc[...] += psum
                    acc_sc[...] += pv
            denom = jnp.sum(l_sc[...], axis=-1, keepdims=True)
            finish_gqa(acc_sc[...] / denom, r0)
            ok = jnp.min(denom) >= ATTN_MIN_DENOM

        @pl.when(jnp.logical_not(ok))
        def _():
            m_sc[...] = jnp.full(m_sc.shape, -jnp.inf, F32)
            l1_sc[...] = jnp.zeros(l1_sc.shape, F32)
            acc_sc[...] = jnp.zeros(acc_sc.shape, F32)

            def body(i, c):
                off = pl.multiple_of(i * tk_exact, tk_exact)
                s = _dot_nt(qs, k_ref[0, pl.ds(off, tk_exact), :])
                m_old = m_sc[...]
                m_new = jnp.maximum(m_old, jnp.max(s, axis=-1, keepdims=True))
                alpha = jnp.exp2(m_old - m_new)
                p = jnp.exp2(s - m_new)
                l1_sc[...] = alpha * l1_sc[...] + jnp.sum(p, axis=-1, keepdims=True)
                acc_sc[...] = alpha * acc_sc[...] + _dot(p.astype(BF16), v_ref[0, pl.ds(off, tk_exact), :])
                m_sc[...] = m_new
                return c

            lax.fori_loop(0, kv_rows // tk_exact, body, 0)
            o = acc_sc[...] / l1_sc[...]
            if diff:
                finish_diff(o[:tq] - lam * o[tq:], r0)
            else:
                finish_gqa(o, r0)

        return carry

    lax.fori_loop(0, q_ref.shape[1] // tq, q_block, 0)


def _attention(q, k, v, shift, lamp, subln, *, diff, lambda_init, kv_rows, kv_block):
    b, t, _ = q.shape
    tq = min(TQ_DIFF if diff else TQ_GQA, t)
    assert t % tq == 0 and kv_rows % LANES == 0
    qw = LANES if diff else GQA_GROUP * LANES
    heads = k.shape[2] // LANES
    stack = (2 if diff else GQA_GROUP) * tq
    tk_max = TK_ATTN_DIFF if diff else TK_ATTN
    steps = [c for c in range(LANES, min(tk_max, kv_rows) + 1, LANES) if kv_rows % c == 0]
    tk = max([c for c in steps if c % MXU_N == 0] or steps)
    kv_spec = pl.BlockSpec((1, kv_rows, LANES), lambda bi, h, *_: (bi, kv_block, h))
    if diff:
        grid = (b, heads, t // tq)
        q_spec = pl.BlockSpec((1, tq, qw), lambda bi, h, i: (bi, i, h))
    else:
        grid = (b, heads)
        q_spec = pl.BlockSpec((1, t, qw), lambda bi, h: (bi, 0, h))
    ins = [q, k, v, shift, lamp, subln]
    specs = [q_spec, kv_spec, kv_spec]
    specs += [pl.BlockSpec(a.shape, lambda bi, h, *_: (0, 0)) for a in (shift, lamp, subln)]
    tk_exact = max(c for c in steps if c <= TK_EXACT)
    kern = functools.partial(_attn_kernel, diff=diff, tq=tq, tk=tk, tk_exact=tk_exact, lambda_init=lambda_init)
    scratch = [pltpu.VMEM((stack, 1), F32), pltpu.VMEM((stack, 1), F32),
               pltpu.VMEM((stack, LANES), F32), pltpu.VMEM((stack, LANES), F32)]
    if diff:
        scratch.append(pltpu.VMEM((stack, kv_rows), BF16))
    return pl.pallas_call(
        kern,
        grid=grid,
        in_specs=specs,
        out_specs=q_spec,
        out_shape=jax.ShapeDtypeStruct((b, t, heads * qw), BF16),
        scratch_shapes=scratch,
        compiler_params=_params(("parallel",) * len(grid), vmem=62 * 1024 * 1024),
        name="attn_diff" if diff else "attn_gqa",
    )(*ins)


def _gla_cumsum(g_ref, *, nchunk, rev):
    c = GLA_CHUNK
    rows = nchunk * c
    ri = lax.broadcasted_iota(jnp.int32, (rows, rows), 0)
    ci = lax.broadcasted_iota(jnp.int32, (rows, rows), 1)
    causal = (ci >= ri) if rev else (ci <= ri)
    tri = jnp.where(jnp.logical_and(ri // c == ci // c, causal), 1.0, 0.0).astype(BF16)
    hi, mid, lo = _split3(g_ref[0])
    return _dot(tri, hi) + _dot(tri, mid) + _dot(tri, lo)


def _gla_sub_ref(bs, i, rev):
    nsub = GLA_CHUNK // GLA_SUB
    if rev:
        return bs[(i + 1) * GLA_SUB:(i + 1) * GLA_SUB + 1] if i < nsub - 1 else jnp.zeros_like(bs[0:1])
    return bs[i * GLA_SUB - 1:i * GLA_SUB] if i > 0 else jnp.zeros_like(bs[0:1])


def _gla_max_diag_exponent(bsum, *, nchunk, rev):
    c = GLA_CHUNK
    worst = None
    for n in range(nchunk):
        bs = bsum[n * c:(n + 1) * c]
        for i in range(c // GLA_SUB):
            far = i * GLA_SUB if rev else (i + 1) * GLA_SUB - 1
            e = _gla_sub_ref(bs, i, rev) - bs[far:far + 1]
            worst = e if worst is None else jnp.maximum(worst, e)
    return jnp.max(worst, axis=-1, keepdims=True)


def _gla_local(q_ref, k_ref, v_ref, bsum, *, nchunk, dkh, dvh, rev, exact_diag):
    c = GLA_CHUNK
    qf, kf = q_ref[0].astype(F32), k_ref[0].astype(F32)
    qt = (qf * jnp.exp(bsum)).astype(BF16)

    krow = lax.broadcasted_iota(jnp.int32, (c, bsum.shape[1]), 0)
    r64 = lax.broadcasted_iota(jnp.int32, (c, c), 0)
    c64 = lax.broadcasted_iota(jnp.int32, (c, c), 1)
    keep = (c64 >= r64) if rev else (c64 <= r64)
    if exact_diag:
        keep = (c64 // GLA_SUB > r64 // GLA_SUB) if rev else (c64 // GLA_SUB < r64 // GLA_SUB)
    rs = lax.broadcasted_iota(jnp.int32, (GLA_SUB, c), 0)
    cs = lax.broadcasted_iota(jnp.int32, (GLA_SUB, c), 1)
    nsub = c // GLA_SUB
    hks = [slice(h * dkh, (h + 1) * dkh) for h in range(GLA_HEADS)]
    scaled, diag = [], {}
    for n in range(nchunk):
        bs, kc, qc = bsum[n * c:(n + 1) * c], kf[n * c:(n + 1) * c], qf[n * c:(n + 1) * c]
        b_tot = bs[0:1] if rev else bs[c - 1:c]
        kd = (kc * jnp.exp(b_tot - bs)).astype(BF16)
        qis, kis = [], []
        for i in range(nsub):
            lo_r, hi_r = i * GLA_SUB, (i + 1) * GLA_SUB
            ref = _gla_sub_ref(bs, i, rev)
            if exact_diag:
                valid = (krow >= hi_r) if rev else (krow < lo_r)
            else:
                valid = (krow >= lo_r) if rev else (krow < hi_r)
            kis.append((kc * jnp.exp(jnp.where(valid, ref - bs, 0.0))).astype(BF16))
            qis.append((qc[lo_r:hi_r] * jnp.exp(bs[lo_r:hi_r] - ref)).astype(BF16))
            if exact_diag:
                cols = [jnp.zeros((GLA_SUB, c), F32) for _ in range(GLA_HEADS)]
                for s in range(lo_r, hi_r):
                    e = jnp.exp(jnp.minimum(bs[lo_r:hi_r] - bs[s:s + 1], 0.0))
                    prod = qc[lo_r:hi_r] * e * kc[s:s + 1]
                    pair = (rs + lo_r <= s) if rev else (rs + lo_r >= s)
                    for h in range(GLA_HEADS):
                        col = jnp.sum(prod[:, hks[h]], axis=-1, keepdims=True)
                        cols[h] = cols[h] + jnp.where(jnp.logical_and(cs == s, pair), col, 0.0)
                for h in range(GLA_HEADS):
                    diag[n, h, i] = cols[h]
        scaled.append((jnp.exp(b_tot), kd, qis, kis))

    pairs = [(n, h) for n in range(nchunk) for h in range(GLA_HEADS)]
    vals = {(n, h): v_ref[0, n * c:(n + 1) * c, h * dvh:(h + 1) * dvh] for n, h in pairs}
    blocks = {(n, h): [_dot_nt(scaled[n][2][i][:, hks[h]], scaled[n][3][i][:, hks[h]]) for i in range(nsub)]
              for n, h in pairs}
    inc = {(n, h): _dot_tn(vals[n, h], scaled[n][1][:, hks[h]]) for n, h in pairs}
    amat = {}
    for n, h in pairs:
        a = jnp.where(keep, jnp.concatenate(blocks[n, h], axis=0), 0.0)
        if exact_diag:
            a = a + jnp.concatenate([diag[n, h, i] for i in range(nsub)], axis=0)
        amat[n, h] = a.astype(BF16)
    intra = {p: _dot(amat[p], vals[p]) for p in pairs}
    local = {(n, h): (scaled[n][0][:, hks[h]], intra[n, h], inc[n, h]) for n, h in pairs}
    return qt, local


def _gla_kernel(qf_ref, kf_ref, vf_ref, gf_ref, qb_ref, kb_ref, vb_ref, gb_ref, s0_ref,
                of_ref, ob_ref, st_out_ref, st_sc, *, nchunk, dkh, dvh):
    j = pl.program_id(1)

    @pl.when(j == 0)
    def _():
        st_sc[...] = s0_ref[0]

    c = GLA_CHUNK
    bf = _gla_cumsum(gf_ref, nchunk=nchunk, rev=False)
    bb = _gla_cumsum(gb_ref, nchunk=nchunk, rev=True)
    worst = jnp.maximum(_gla_max_diag_exponent(bf, nchunk=nchunk, rev=False),
                        _gla_max_diag_exponent(bb, nchunk=nchunk, rev=True))
    factored_ok = jnp.max(worst) <= GLA_MAX_EXPONENT

    def sweep(exact_diag):
        kw = dict(nchunk=nchunk, dkh=dkh, dvh=dvh, exact_diag=exact_diag)
        sides = ((_gla_local(qf_ref, kf_ref, vf_ref, bf, rev=False, **kw), of_ref),
                 (_gla_local(qb_ref, kb_ref, vb_ref, bb, rev=True, **kw), ob_ref))
        for step in range(nchunk):
            for side, ((qt, local), o_ref) in enumerate(sides):
                n = step if side == 0 else nchunk - 1 - step
                for h in range(GLA_HEADS):
                    hk, hv = slice(h * dkh, (h + 1) * dkh), slice(h * dvh, (h + 1) * dvh)
                    dec, intra, inc = local[n, h]
                    st = st_sc[side, h]
                    o = _dot_nt(qt[n * c:(n + 1) * c, hk], st.astype(BF16)) + intra
                    o_ref[0, n * c:(n + 1) * c, hv] = o.astype(BF16)
                    st_sc[side, h] = dec * st + inc

    @pl.when(factored_ok)
    def _():
        sweep(False)

    @pl.when(jnp.logical_not(factored_ok))
    def _():
        sweep(True)

    @pl.when(j == pl.num_programs(1) - 1)
    def _():
        st_out_ref[0] = st_sc[...]


def _gla_scan(q, k, v, g, s0):
    b, t, dk = q.shape
    dv = v.shape[2]
    dkh, dvh = dk // GLA_HEADS, dv // GLA_HEADS
    nchunk = min(GLA_R, t // GLA_CHUNK)
    rows = nchunk * GLA_CHUNK
    assert t % rows == 0
    n = t // rows
    fwd = lambda w, cb=0: pl.BlockSpec((1, rows, w), lambda bi, j: (bi, j, cb))
    bwd = lambda w, cb=0: pl.BlockSpec((1, rows, w), lambda bi, j: (bi, n - 1 - j, cb))
    st_spec = pl.BlockSpec((1, 2, GLA_HEADS, dvh, dkh), lambda bi, j: (bi, 0, 0, 0, 0))
    kern = functools.partial(_gla_kernel, nchunk=nchunk, dkh=dkh, dvh=dvh)
    return pl.pallas_call(
        kern,
        grid=(b, n),
        in_specs=[fwd(dk), fwd(dk), fwd(dv), fwd(dk, 0), bwd(dk), bwd(dk), bwd(dv), bwd(dk, 1), st_spec],
        out_specs=[fwd(dv), bwd(dv), st_spec],
        out_shape=[jax.ShapeDtypeStruct((b, t, dv), BF16), jax.ShapeDtypeStruct((b, t, dv), BF16),
                   jax.ShapeDtypeStruct(s0.shape, F32)],
        scratch_shapes=[pltpu.VMEM((2, GLA_HEADS, dvh, dkh), F32)],
        compiler_params=_params(("parallel", "arbitrary")),
        name="gla_scan",
    )(q, k, v, g, q, k, v, g, s0)


def _post_kernel(*refs, d, f, gla, dvh):
    if gla:
        x_ref, of_ref, ob_ref, r_ref, on_ref, mod_ref, wo_ref, g2_ref, wgu_ref, wd_ref, o_ref = refs
        r = r_ref[...].astype(F32)
        gate = r * _sigmoid(r)
        parts = []
        for h in range(GLA_HEADS):
            cs = slice(h * dvh, (h + 1) * dvh)
            oh = of_ref[:, cs].astype(F32) + ob_ref[:, cs].astype(F32)
            ms = jnp.mean(oh * oh, axis=-1, keepdims=True)
            parts.append((oh * lax.rsqrt(ms + NORM_EPS) * on_ref[...] * gate[:, cs]).astype(BF16))
        a = jnp.concatenate(parts, axis=1)
    else:
        x_ref, a_ref, mod_ref, wo_ref, g2_ref, wgu_ref, wd_ref, o_ref = refs
        a = a_ref[...]
    m = mod_ref[0]
    gate1, shift2, scale2, gate2 = (m[:, 2 * d:3 * d], m[:, 3 * d:4 * d], m[:, 4 * d:5 * d], m[:, 5 * d:6 * d])
    x1 = x_ref[...] + gate1 * _dot(a, wo_ref[...])
    h2 = _norm_mod(x1, g2_ref[...], shift2, scale2).astype(BF16)
    acts = []
    for c0 in range(0, f, FFN_CHUNK):
        gt = _dot(h2, wgu_ref[:, c0:c0 + FFN_CHUNK])
        up = _dot(h2, wgu_ref[:, f + c0:f + c0 + FFN_CHUNK])
        acts.append((gt * _sigmoid(gt) * up).astype(BF16))
    o_ref[...] = x1 + gate2 * _dot(jnp.concatenate(acts, axis=1), wd_ref[...])


def _post(x2d, mix, modx, rows_per_mod, wo, g2, wgu, wd, *, gla_norm=None):
    n, d = x2d.shape
    f = wd.shape[0]
    tm = min(TM_POST, n)
    assert n % tm == 0 and rows_per_mod % tm == 0 and f % FFN_CHUNK == 0
    mod_blocks = rows_per_mod // tm
    row = lambda c: pl.BlockSpec((tm, c), lambda i: (i, 0))
    gla = gla_norm is not None
    if gla:
        ins = [x2d, *mix, gla_norm]
        specs = [row(d), row(d), row(d), row(d), _const_spec(gla_norm.shape)]
        dvh = gla_norm.shape[1]
    else:
        ins = [x2d, mix]
        specs = [row(d), row(d)]
        dvh = 0
    ins += [modx, wo, g2, wgu, wd]
    specs += [pl.BlockSpec((1, 1, 6 * d), lambda i: (i // mod_blocks, 0, 0)),
              _const_spec(wo.shape), _const_spec((1, d)), _const_spec(wgu.shape), _const_spec(wd.shape)]
    kern = functools.partial(_post_kernel, d=d, f=f, gla=gla, dvh=dvh)
    return pl.pallas_call(
        kern,
        grid=(n // tm,),
        in_specs=specs,
        out_specs=row(d),
        out_shape=jax.ShapeDtypeStruct((n, d), F32),
        compiler_params=_params(("parallel",)),
        name="post_gla" if gla else "post_attn",
    )(*ins)


def _rope_tables(n_tokens, head_dim):
    rows = n_tokens // GRID_W
    rowp = jnp.broadcast_to(jnp.arange(rows)[:, None], (rows, GRID_W)).reshape(-1)
    colp = jnp.broadcast_to(jnp.arange(GRID_W)[None, :], (rows, GRID_W)).reshape(-1)
    half = head_dim // 2
    quarter = head_dim // 4
    inv = ROPE_BASE ** (-jnp.arange(0, half, 2, dtype=F32) / half)

    def axis_angles(pos):
        a = pos.astype(F32)[:, None] * inv[None, :]
        return jnp.concatenate([a, a], axis=-1)

    ang = jnp.concatenate([axis_angles(rowp), axis_angles(colp)], axis=-1)
    reps = LANES // head_dim
    cos = jnp.tile(jnp.cos(ang), (1, reps))
    sin = jnp.tile(jnp.sin(ang), (1, reps))
    first = (jnp.arange(LANES) % (2 * quarter)) < quarter
    sa = jnp.where(first[None, :], -sin, 0.0)
    sb = jnp.where(first[None, :], 0.0, sin)
    return cos, sa, sb


def _seg_matrix(seg):
    idx = jnp.arange(MXU_N) // seg
    return jnp.where(idx[:, None] == idx[None, :], 1.0 / seg, 0.0).astype(BF16)


def kernel(x, c, ctx, c_ctx, ada_w, ada_b, norm1_g, norm2_g, ffn_w_gu, ffn_w_down, diff_w_in, diff_w_out, diff_q_norm, diff_k_norm, diff_lambda_q1, diff_lambda_k1, diff_lambda_q2, diff_lambda_k2, diff_subln, gqa_w_in, gqa_w_out, gqa_q_norm, gqa_k_norm, gla_w_in, gla_gate_w1_fwd, gla_gate_w2_fwd, gla_gate_b_fwd, gla_gate_w1_bwd, gla_gate_w2_bwd, gla_gate_b_bwd, gla_out_norm, gla_w_out):
    b, l, d = x.shape
    lc = ctx.shape[1]
    depth = ada_w.shape[0]
    nl, nc = b * l, b * lc

    assert l % lc == 0 and l % GRID_W == 0 and lc % GLA_CHUNK == 0
    cond_rows = pl.cdiv(b + 1, SUBLANES) * SUBLANES
    cond = jnp.zeros((cond_rows, d), F32).at[:b].set(c).at[b].set(c_ctx)
    mod = _modulation(cond, ada_w, ada_b)

    xl = x.reshape(nl, d)
    xc = ctx.reshape(nc, d)
    ones_tab = (jnp.ones((lc, LANES), F32), jnp.zeros((lc, LANES), F32), jnp.zeros((lc, LANES), F32))
    dummy_lam = jnp.zeros((4, DIFF_DH), F32)
    dummy_sub = jnp.ones((1, LANES), F32)

    for i in range(depth):
        last = i == depth - 1
        kind, j = i % N_MIXERS, i // N_MIXERS
        modl = mod[i, :b].reshape(b, 1, 6 * d)
        modc = mod[i, b:b + 1].reshape(1, 1, 6 * d)
        g1 = norm1_g[i].reshape(1, d)
        g2 = norm2_g[i].reshape(1, d)
        wgu = ffn_w_gu[i].astype(BF16)
        wd = ffn_w_down[i].astype(BF16)

        if kind in (0, 1):
            if kind == 0:
                diff, dh = True, DIFF_DH
                w_in, w_out = diff_w_in[j].astype(BF16), diff_w_out[j].astype(BF16)
                gq = jnp.tile(diff_q_norm[j], LANES // dh).reshape(1, LANES) * (dh ** -0.5 * LOG2E)
                gk = jnp.tile(diff_k_norm[j], LANES // dh).reshape(1, LANES)
                nq = nk = nv = d // LANES
                lamp = jnp.stack([diff_lambda_q1[j], diff_lambda_k1[j], diff_lambda_q2[j], diff_lambda_k2[j]])
                subln = diff_subln[j].reshape(1, LANES)
                lambda_init = 0.8 - 0.6 * math.exp(-0.3 * i)
            else:
                diff, dh = False, GQA_DH
                w_in, w_out = gqa_w_in[j].astype(BF16), gqa_w_out[j].astype(BF16)
                gq = gqa_q_norm[j].reshape(1, LANES) * (dh ** -0.5 * LOG2E)
                gk = gqa_k_norm[j].reshape(1, LANES)
                nq = d // LANES
                nk = nv = (w_in.shape[1] - d) // (2 * LANES)
                lamp, subln, lambda_init = dummy_lam, dummy_sub, 0.0
            pmat = _seg_matrix(dh)
            cos, sa, sb = _rope_tables(l, dh)
            kw = dict(nq=nq, nk=nk, nv=nv, rot=dh // 4)
            ql, k_all, v_all = _pre_attn(xl, modl, l, g1, w_in, pmat, gq, gk, cos, sa, sb, l, l + lc, 0, None, **kw)
            qc, k_all, v_all = _pre_attn(xc, modc, nc, g1, w_in, pmat, gq, gk, *ones_tab, lc, l + lc, l,
                                         (k_all, v_all), **kw)
            r3 = lambda t, n: t.reshape(b, n, t.shape[1])
            shift = jnp.full((1, LANES), dh, F32) * jnp.max(jnp.abs(gq)) * jnp.max(jnp.abs(gk))
            akw = dict(diff=diff, lambda_init=lambda_init)
            ol = _attention(r3(ql, l), k_all, v_all, shift, lamp, subln, kv_rows=l + lc, kv_block=0, **akw)
            xl = _post(xl, ol.reshape(nl, d), modl, l, w_out, g2, wgu, wd)
            if not last:
                oc = _attention(r3(qc, lc), k_all, v_all, shift, lamp, subln, kv_rows=lc, kv_block=l // lc, **akw)
                xc = _post(xc, oc.reshape(nc, d), modc, nc, w_out, g2, wgu, wd)
        else:
            dk, dv = gla_gate_w2_fwd.shape[2], gla_w_out.shape[1]
            rank = gla_gate_w1_fwd.shape[2]
            w1 = jnp.zeros((d, MXU_N), F32).at[:, :rank].set(gla_gate_w1_fwd[j]).at[:, rank:2 * rank].set(gla_gate_w1_bwd[j])
            w_in = jnp.concatenate([gla_w_in[j], w1], axis=1).astype(BF16)
            w2 = (jnp.zeros((MXU_N, 2 * dk), F32).at[:rank, :dk].set(gla_gate_w2_fwd[j])
                  .at[rank:2 * rank, dk:].set(gla_gate_w2_bwd[j])).astype(BF16)
            gb = jnp.concatenate([gla_gate_b_fwd[j], gla_gate_b_bwd[j]]).reshape(1, 2 * dk)
            w_out = gla_w_out[j].astype(BF16)
            dkh, dvh = dk // GLA_HEADS, dv // GLA_HEADS
            onorm = gla_out_norm[j].reshape(1, dvh)
            kw = dict(dk=dk, dv=dv, qscale=dkh ** -0.5)
            ql, kl, vl, rl, gl = _pre_gla(xl, modl, l, g1, w_in, w2, gb, **kw)
            qc, kc, vc, rc, gc = _pre_gla(xc, modc, nc, g1, w_in, w2, gb, **kw)
            r3 = lambda t, n: t.reshape(b, n, t.shape[1])
            s0 = jnp.zeros((b, 2, GLA_HEADS, dvh, dkh), F32)
            ocf, ocb, sc = _gla_scan(r3(qc, lc), r3(kc, lc), r3(vc, lc), r3(gc, lc), s0)
            olf, olb, _ = _gla_scan(r3(ql, l), r3(kl, l), r3(vl, l), r3(gl, l), sc)
            xl = _post(xl, (olf.reshape(nl, dv), olb.reshape(nl, dv), rl), modl, l, w_out, g2, wgu, wd, gla_norm=onorm)
            if not last:
                xc = _post(xc, (ocf.reshape(nc, dv), ocb.reshape(nc, dv), rc), modc, nc, w_out, g2, wgu, wd, gla_norm=onorm)
    return xl.reshape(b, l, d)
```

```python
import functools
import math

import jax
import jax.numpy as jnp
from jax import lax
from jax.experimental import pallas as pl
from jax.experimental.pallas import tpu as pltpu

F32 = jnp.float32
BF16 = jnp.bfloat16

GRID_W = 64
N_MIXERS = 3
ROPE_BASE = 10000.0
NORM_EPS = 1e-6
DIFF_DH = 64
GQA_DH = 128
GQA_GROUP = 4
GLA_HEADS = 4
GLA_GATE_RANK = 16
GLA_TAU = 16.0
GLA_CHUNK = 64
GLA_SUB = 16
LOG2E = math.log2(math.e)
ATTN_MIN_DENOM = 1e-22
GLA_MAX_EXPONENT = 80.0

LANES = 128
SUBLANES = 8
MXU_N = 256
VMEM_LIMIT = 56 * 1024 * 1024
VMEM_LIMIT_LARGE = 62 * 1024 * 1024

MOD_TN = 1536
TM_PRE = 512
TM_POST = 1024
FFN_CHUNK = 256
TQ_DIFF = 1024
TQ_GQA = 512
TK_ATTN = 3072
TK_ATTN_DIFF = 1024
TK_EXACT = 768
GLA_R = 4


def _const_spec(shape):
    nd = len(shape)
    return pl.BlockSpec(shape, lambda *_: (0,) * nd, pipeline_mode=pl.Buffered(1))


def _params(sem, vmem=VMEM_LIMIT):
    return pltpu.CompilerParams(dimension_semantics=sem, vmem_limit_bytes=vmem)


def _sigmoid(x):
    return 1.0 / (1.0 + jnp.exp(-x))


def _norm_mod(x, g, shift, scale):
    ms = jnp.mean(x * x, axis=-1, keepdims=True)
    return (x * lax.rsqrt(ms + NORM_EPS) * g) * (1.0 + scale) + shift


def _dot(a, b):
    return jnp.dot(a, b, preferred_element_type=F32)


def _dot_nt(a, b):
    return lax.dot_general(a, b, (((1,), (1,)), ((), ())), preferred_element_type=F32)


def _dot_tn(a, b):
    return lax.dot_general(a, b, (((0,), (0,)), ((), ())), preferred_element_type=F32)


def _split3(x):
    hi = x.astype(BF16)
    r1 = x - hi.astype(F32)
    mid = r1.astype(BF16)
    lo = (r1 - mid.astype(F32)).astype(BF16)
    return hi, mid, lo


def _mod_kernel(cond_ref, w_ref, b_ref, o_ref):
    c = cond_ref[...]
    s = (c * _sigmoid(c)).astype(BF16)
    o_ref[0] = _dot(s, w_ref[0].astype(BF16)) + b_ref[0]


def _modulation(cond, ada_w, ada_b):
    depth, d, n = ada_w.shape
    rows = cond.shape[0]
    tn = MOD_TN
    assert n % tn == 0 and rows % SUBLANES == 0
    return pl.pallas_call(
        _mod_kernel,
        grid=(depth, n // tn),
        in_specs=[
            pl.BlockSpec((rows, d), lambda i, j: (0, 0)),
            pl.BlockSpec((1, d, tn), lambda i, j: (i, 0, j)),
            pl.BlockSpec((1, 1, tn), lambda i, j: (i, 0, j)),
        ],
        out_specs=pl.BlockSpec((1, rows, tn), lambda i, j: (i, 0, j)),
        out_shape=jax.ShapeDtypeStruct((depth, rows, n), F32),
        compiler_params=_params(("parallel", "parallel")),
        name="adaln_mod",
    )(cond, ada_w, ada_b.reshape(depth, 1, n))


def _pre_attn_kernel(x_ref, mod_ref, g1_ref, w_ref, p_ref, gq_ref, gk_ref, cos_ref, sa_ref, sb_ref, *rest,
                     d, nq, nk, nv, rot):
    q_ref, k_ref, v_ref = rest[-3], rest[-2].at[0], rest[-1].at[0]
    x = x_ref[...]
    m = mod_ref[0]
    h = _norm_mod(x, g1_ref[...], m[:, 0:d], m[:, d:2 * d]).astype(BF16)
    cos, sa, sb = cos_ref[...], sa_ref[...], sb_ref[...]
    per = MXU_N // LANES

    def project(grp):
        return _dot(h, w_ref[:, grp * MXU_N:(grp + 1) * MXU_N])

    def emit(grp, y):
        slab0 = grp * per
        if slab0 >= nq + nk:
            j = slab0 - nq - nk
            v_ref[:, j * LANES:(j + per) * LANES] = y.astype(BF16)
            return
        ms = _dot((y * y).astype(BF16), p_ref[...])
        yn = y * lax.rsqrt(ms + NORM_EPS)
        for t in range(per):
            slab = slab0 + t
            out_ref, j, gain = (q_ref, slab, gq_ref) if slab < nq else (k_ref, slab - nq, gk_ref)
            xn = yn[:, t * LANES:(t + 1) * LANES] * gain[...]
            r = xn * cos + pltpu.roll(xn, LANES - rot, 1) * sa + pltpu.roll(xn, rot, 1) * sb
            out_ref[:, j * LANES:(j + 1) * LANES] = r.astype(BF16)

    ngrp = (nq + nk + nv) // per
    y = project(0)
    for grp in range(ngrp):
        y_next = project(grp + 1) if grp + 1 < ngrp else None
        emit(grp, y)
        y = y_next


def _pre_attn(x2d, modx, rows_per_mod, g1, w, pmat, gq, gk, cos, sa, sb, seq, kv_rows, kv_row0, kv_prev,
              *, nq, nk, nv, rot):
    n, d = x2d.shape
    tm = min(TM_PRE, seq)
    assert seq % tm == 0 and rows_per_mod % tm == 0 and kv_row0 % tm == 0
    assert nq % 2 == 0 and nk % 2 == 0 and nv % 2 == 0
    seq_blocks = seq // tm
    mod_blocks = rows_per_mod // tm
    kv_block0 = kv_row0 // tm
    row = lambda c: pl.BlockSpec((tm, c), lambda i: (i, 0))
    tab = pl.BlockSpec((tm, LANES), lambda i: (i % seq_blocks, 0))
    kv = lambda c: pl.BlockSpec((1, tm, c), lambda i: (i // seq_blocks, kv_block0 + i % seq_blocks, 0))
    ins = [x2d, modx, g1, w, pmat, gq, gk, cos, sa, sb]
    specs = [
        row(d),
        pl.BlockSpec((1, 1, 6 * d), lambda i: (i // mod_blocks, 0, 0)),
        _const_spec((1, d)),
        _const_spec(w.shape),
        _const_spec(pmat.shape),
        _const_spec((1, LANES)),
        _const_spec((1, LANES)),
        tab, tab, tab,
    ]
    aliases = {}
    if kv_prev is not None:
        aliases = {len(ins): 1, len(ins) + 1: 2}
        ins += list(kv_prev)
        specs += [pl.BlockSpec(memory_space=pl.ANY)] * 2
    batches = n // seq
    kern = functools.partial(_pre_attn_kernel, d=d, nq=nq, nk=nk, nv=nv, rot=rot)
    return pl.pallas_call(
        kern,
        grid=(n // tm,),
        in_specs=specs,
        out_specs=[row(nq * LANES), kv(nk * LANES), kv(nv * LANES)],
        out_shape=[jax.ShapeDtypeStruct((n, nq * LANES), BF16),
                   jax.ShapeDtypeStruct((batches, kv_rows, nk * LANES), BF16),
                   jax.ShapeDtypeStruct((batches, kv_rows, nv * LANES), BF16)],
        input_output_aliases=aliases,
        compiler_params=_params(("parallel",)),
        name="pre_attn",
    )(*ins)


def _pre_gla_kernel(x_ref, mod_ref, g1_ref, w_ref, w2_ref, gb_ref,
                    q_ref, k_ref, v_ref, r_ref, g_ref, *, d, dk, dv, qscale):
    x = x_ref[...]
    m = mod_ref[0]
    h = _norm_mod(x, g1_ref[...], m[:, 0:d], m[:, d:2 * d]).astype(BF16)
    bounds = (dk, 2 * dk, 2 * dk + dv, 2 * dk + 2 * dv)
    def project(c0):
        return _dot(h, w_ref[:, c0:c0 + MXU_N])

    z = _dot(project(bounds[3]).astype(BF16), w2_ref[...]) + gb_ref[...]
    logsig = jnp.minimum(z, 0.0) - jnp.log(1.0 + jnp.exp(-jnp.abs(z)))
    g_ref[...] = logsig * (1.0 / GLA_TAU)
    y_next = project(0)
    for c0 in range(0, bounds[3], MXU_N):
        y = y_next
        if c0 + MXU_N < bounds[3]:
            y_next = project(c0 + MXU_N)
        if c0 < bounds[0]:
            q_ref[:, c0:c0 + MXU_N] = (y * qscale).astype(BF16)
        elif c0 < bounds[1]:
            k_ref[:, c0 - bounds[0]:c0 - bounds[0] + MXU_N] = y.astype(BF16)
        elif c0 < bounds[2]:
            v_ref[:, c0 - bounds[1]:c0 - bounds[1] + MXU_N] = y.astype(BF16)
        else:
            r_ref[:, c0 - bounds[2]:c0 - bounds[2] + MXU_N] = y.astype(BF16)


def _pre_gla(x2d, modx, rows_per_mod, g1, w, w2, gb, *, dk, dv, qscale):
    n, d = x2d.shape
    tm = min(TM_PRE, n)
    assert n % tm == 0 and rows_per_mod % tm == 0 and dk % MXU_N == 0 and dv % MXU_N == 0
    mod_blocks = rows_per_mod // tm
    row = lambda c: pl.BlockSpec((tm, c), lambda i: (i, 0))
    kern = functools.partial(_pre_gla_kernel, d=d, dk=dk, dv=dv, qscale=qscale)
    return pl.pallas_call(
        kern,
        grid=(n // tm,),
        in_specs=[
            row(d),
            pl.BlockSpec((1, 1, 6 * d), lambda i: (i // mod_blocks, 0, 0)),
            _const_spec((1, d)),
            _const_spec(w.shape),
            _const_spec(w2.shape),
            _const_spec((1, 2 * dk)),
        ],
        out_specs=[row(dk), row(dk), row(dv), row(dv), row(2 * dk)],
        out_shape=[jax.ShapeDtypeStruct((n, c), t) for c, t in
                   ((dk, BF16), (dk, BF16), (dv, BF16), (dv, BF16), (2 * dk, F32))],
        compiler_params=_params(("parallel",)),
        name="pre_gla",
    )(x2d, modx, g1, w, w2, gb)


def _attn_kernel(q_ref, k_ref, v_ref, shift_ref, lam_ref, sub_ref, o_ref, m_sc, l1_sc, l_sc, acc_sc,
                 *maybe_p_sc, diff, tq, tk, tk_exact, lambda_init):
    p_sc = maybe_p_sc[0] if diff else None
    kv_rows = k_ref.shape[1]
    n_steps = kv_rows // tk
    lane = lax.broadcasted_iota(jnp.int32, (1, LANES), 1)
    lo_mask = jnp.where(lane < DIFF_DH, 1.0, 0.0).astype(BF16)
    shift = shift_ref[...]

    def stacked_q(r0):
        q = q_ref[0, pl.ds(r0, tq), :]
        if diff:
            return jnp.concatenate([q * lo_mask, q * (1.0 - lo_mask)], axis=0)
        return jnp.concatenate([q[:, g * LANES:(g + 1) * LANES] for g in range(GQA_GROUP)], axis=0)

    def probabilities(qs, i):
        s = _dot_nt(qs, k_ref[0, i * tk:(i + 1) * tk, :])
        psum, ps = None, []
        for c in range(tk // LANES):
            pc = jnp.exp2(s[:, c * LANES:(c + 1) * LANES] - shift)
            psum = pc if psum is None else psum + pc
            ps.append(pc.astype(BF16))
        return jnp.concatenate(ps, axis=1), psum

    def finish_diff(a, r0):
        ms = jnp.mean(a * a, axis=-1, keepdims=True)
        a = a * lax.rsqrt(ms + NORM_EPS) * sub_ref[...] * (1.0 - lambda_init)
        o_ref[0, pl.ds(r0, tq), :] = a.astype(BF16)

    def finish_gqa(o, r0):
        for g in range(GQA_GROUP):
            o_ref[0, pl.ds(r0, tq), g * LANES:(g + 1) * LANES] = o[g * tq:(g + 1) * tq].astype(BF16)

    if diff:
        lp = lam_ref[...]
        lam = (jnp.exp(jnp.sum(lp[0:1] * lp[1:2], axis=-1, keepdims=True))
               - jnp.exp(jnp.sum(lp[2:3] * lp[3:4], axis=-1, keepdims=True)) + lambda_init)

    def q_block(j, carry):
        r0 = pl.multiple_of(j * tq, tq)
        qs = stacked_q(r0)
        if diff:
            for i in range(n_steps):
                p, psum = probabilities(qs, i)
                p_sc[:, i * tk:(i + 1) * tk] = p
                if i == 0:
                    l_sc[...] = psum
                else:
                    l_sc[...] += psum
            denom = jnp.sum(l_sc[...], axis=-1, keepdims=True)
            ok = jnp.min(denom) >= ATTN_MIN_DENOM

            @pl.when(ok)
            def _():
                l1, l2 = denom[:tq], denom[tq:]
                rho = jnp.broadcast_to(lam * l1 / l2, (tq, LANES)).astype(BF16)
                acc = None
                for i in range(n_steps):
                    mix = [p_sc[0:tq, i * tk + c * LANES:i * tk + (c + 1) * LANES]
                           - rho * p_sc[tq:2 * tq, i * tk + c * LANES:i * tk + (c + 1) * LANES]
                           for c in range(tk // LANES)]
                    pv = _dot(jnp.concatenate(mix, axis=1), v_ref[0, i * tk:(i + 1) * tk, :])
                    acc = pv if acc is None else acc + pv
                finish_diff(acc / l1, r0)
        else:
            for i in range(n_steps):
                p, psum = probabilities(qs, i)
                pv = _dot(p, v_ref[0, i * tk:(i + 1) * tk, :])
                if i == 0:
                    l_sc[...] = psum
                    acc_sc[...] = pv
                else:
                    l_sc[...] += psum
                    acc_sc[...] += pv
            denom = jnp.sum(l_sc[...], axis=-1, keepdims=True)
            finish_gqa(acc_sc[...] / denom, r0)
            ok = jnp.min(denom) >= ATTN_MIN_DENOM

        @pl.when(jnp.logical_not(ok))
        def _():
            m_sc[...] = jnp.full(m_sc.shape, -jnp.inf, F32)
            l1_sc[...] = jnp.zeros(l1_sc.shape, F32)
            acc_sc[...] = jnp.zeros(acc_sc.shape, F32)

            def body(i, c):
                off = pl.multiple_of(i * tk_exact, tk_exact)
                s = _dot_nt(qs, k_ref[0, pl.ds(off, tk_exact), :])
                m_old = m_sc[...]
                m_new = jnp.maximum(m_old, jnp.max(s, axis=-1, keepdims=True))
                alpha = jnp.exp2(m_old - m_new)
                p = jnp.exp2(s - m_new)
                l1_sc[...] = alpha * l1_sc[...] + jnp.sum(p, axis=-1, keepdims=True)
                acc_sc[...] = alpha * acc_sc[...] + _dot(p.astype(BF16), v_ref[0, pl.ds(off, tk_exact), :])
                m_sc[...] = m_new
                return c

            lax.fori_loop(0, kv_rows // tk_exact, body, 0)
            o = acc_sc[...] / l1_sc[...]
            if diff:
                finish_diff(o[:tq] - lam * o[tq:], r0)
            else:
                finish_gqa(o, r0)

        return carry

    lax.fori_loop(0, q_ref.shape[1] // tq, q_block, 0)


def _attention(q, k, v, shift, lamp, subln, *, diff, lambda_init, kv_rows, kv_block):
    b, t, _ = q.shape
    tq = min(TQ_DIFF if diff else TQ_GQA, t)
    assert t % tq == 0 and kv_rows % LANES == 0
    qw = LANES if diff else GQA_GROUP * LANES
    heads = k.shape[2] // LANES
    stack = (2 if diff else GQA_GROUP) * tq
    tk_max = TK_ATTN_DIFF if diff else TK_ATTN
    steps = [c for c in range(LANES, min(tk_max, kv_rows) + 1, LANES) if kv_rows % c == 0]
    tk = max([c for c in steps if c % MXU_N == 0] or steps)
    kv_spec = pl.BlockSpec((1, kv_rows, LANES), lambda bi, h, *_: (bi, kv_block, h))
    if diff:
        grid = (b, heads, t // tq)
        q_spec = pl.BlockSpec((1, tq, qw), lambda bi, h, i: (bi, i, h))
    else:
        grid = (b, heads)
        q_spec = pl.BlockSpec((1, t, qw), lambda bi, h: (bi, 0, h))
    ins = [q, k, v, shift, lamp, subln]
    specs = [q_spec, kv_spec, kv_spec]
    specs += [pl.BlockSpec(a.shape, lambda bi, h, *_: (0, 0)) for a in (shift, lamp, subln)]
    tk_exact = max(c for c in steps if c <= TK_EXACT)
    kern = functools.partial(_attn_kernel, diff=diff, tq=tq, tk=tk, tk_exact=tk_exact, lambda_init=lambda_init)
    scratch = [pltpu.VMEM((stack, 1), F32), pltpu.VMEM((stack, 1), F32),
               pltpu.VMEM((stack, LANES), F32), pltpu.VMEM((stack, LANES), F32)]
    if diff:
        scratch.append(pltpu.VMEM((stack, kv_rows), BF16))
    return pl.pallas_call(
        kern,
        grid=grid,
        in_specs=specs,
        out_specs=q_spec,
        out_shape=jax.ShapeDtypeStruct((b, t, heads * qw), BF16),
        scratch_shapes=scratch,
        compiler_params=_params(("parallel",) * len(grid), vmem=VMEM_LIMIT_LARGE),
        name="attn_diff" if diff else "attn_gqa",
    )(*ins)


def _gla_cumsum(g_ref, *, nchunk, rev):
    c = GLA_CHUNK
    rows = nchunk * c
    ri = lax.broadcasted_iota(jnp.int32, (rows, rows), 0)
    ci = lax.broadcasted_iota(jnp.int32, (rows, rows), 1)
    causal = (ci >= ri) if rev else (ci <= ri)
    tri = jnp.where(jnp.logical_and(ri // c == ci // c, causal), 1.0, 0.0).astype(BF16)
    hi, mid, lo = _split3(g_ref[0])
    return _dot(tri, hi) + _dot(tri, mid) + _dot(tri, lo)


def _gla_sub_ref(bs, i, rev):
    nsub = GLA_CHUNK // GLA_SUB
    if rev:
        return bs[(i + 1) * GLA_SUB:(i + 1) * GLA_SUB + 1] if i < nsub - 1 else jnp.zeros_like(bs[0:1])
    return bs[i * GLA_SUB - 1:i * GLA_SUB] if i > 0 else jnp.zeros_like(bs[0:1])


def _gla_max_diag_exponent(bsum, *, nchunk, rev):
    c = GLA_CHUNK
    worst = None
    for n in range(nchunk):
        bs = bsum[n * c:(n + 1) * c]
        for i in range(c // GLA_SUB):
            far = i * GLA_SUB if rev else (i + 1) * GLA_SUB - 1
            e = _gla_sub_ref(bs, i, rev) - bs[far:far + 1]
            worst = e if worst is None else jnp.maximum(worst, e)
    return jnp.max(worst, axis=-1, keepdims=True)


def _gla_local(q_ref, k_ref, v_ref, bsum, *, nchunk, dkh, dvh, rev, exact_diag):
    c = GLA_CHUNK
    qf, kf = q_ref[0].astype(F32), k_ref[0].astype(F32)
    qt = (qf * jnp.exp(bsum)).astype(BF16)

    krow = lax.broadcasted_iota(jnp.int32, (c, bsum.shape[1]), 0)
    r64 = lax.broadcasted_iota(jnp.int32, (c, c), 0)
    c64 = lax.broadcasted_iota(jnp.int32, (c, c), 1)
    keep = (c64 >= r64) if rev else (c64 <= r64)
    if exact_diag:
        keep = (c64 // GLA_SUB > r64 // GLA_SUB) if rev else (c64 // GLA_SUB < r64 // GLA_SUB)
    rs = lax.broadcasted_iota(jnp.int32, (GLA_SUB, c), 0)
    cs = lax.broadcasted_iota(jnp.int32, (GLA_SUB, c), 1)
    nsub = c // GLA_SUB
    hks = [slice(h * dkh, (h + 1) * dkh) for h in range(GLA_HEADS)]
    scaled, diag = [], {}
    for n in range(nchunk):
        bs, kc, qc = bsum[n * c:(n + 1) * c], kf[n * c:(n + 1) * c], qf[n * c:(n + 1) * c]
        b_tot = bs[0:1] if rev else bs[c - 1:c]
        kd = (kc * jnp.exp(b_tot - bs)).astype(BF16)
        qis, kis = [], []
        for i in range(nsub):
            lo_r, hi_r = i * GLA_SUB, (i + 1) * GLA_SUB
            ref = _gla_sub_ref(bs, i, rev)
            if exact_diag:
                valid = (krow >= hi_r) if rev else (krow < lo_r)
            else:
                valid = (krow >= lo_r) if rev else (krow < hi_r)
            kis.append((kc * jnp.exp(jnp.where(valid, ref - bs, 0.0))).astype(BF16))
            qis.append((qc[lo_r:hi_r] * jnp.exp(bs[lo_r:hi_r] - ref)).astype(BF16))
            if exact_diag:
                cols = [jnp.zeros((GLA_SUB, c), F32) for _ in range(GLA_HEADS)]
                for s in range(lo_r, hi_r):
                    e = jnp.exp(jnp.minimum(bs[lo_r:hi_r] - bs[s:s + 1], 0.0))
                    prod = qc[lo_r:hi_r] * e * kc[s:s + 1]
                    pair = (rs + lo_r <= s) if rev else (rs + lo_r >= s)
                    for h in range(GLA_HEADS):
                        col = jnp.sum(prod[:, hks[h]], axis=-1, keepdims=True)
                        cols[h] = cols[h] + jnp.where(jnp.logical_and(cs == s, pair), col, 0.0)
                for h in range(GLA_HEADS):
                    diag[n, h, i] = cols[h]
        scaled.append((jnp.exp(b_tot), kd, qis, kis))

    pairs = [(n, h) for n in range(nchunk) for h in range(GLA_HEADS)]
    vals = {(n, h): v_ref[0, n * c:(n + 1) * c, h * dvh:(h + 1) * dvh] for n, h in pairs}
    blocks = {(n, h): [_dot_nt(scaled[n][2][i][:, hks[h]], scaled[n][3][i][:, hks[h]]) for i in range(nsub)]
              for n, h in pairs}
    inc = {(n, h): _dot_tn(vals[n, h], scaled[n][1][:, hks[h]]) for n, h in pairs}
    amat = {}
    for n, h in pairs:
        a = jnp.where(keep, jnp.concatenate(blocks[n, h], axis=0), 0.0)
        if exact_diag:
            a = a + jnp.concatenate([diag[n, h, i] for i in range(nsub)], axis=0)
        amat[n, h] = a.astype(BF16)
    intra = {p: _dot(amat[p], vals[p]) for p in pairs}
    local = {(n, h): (scaled[n][0][:, hks[h]], intra[n, h], inc[n, h]) for n, h in pairs}
    return qt, local


def _gla_kernel(qf_ref, kf_ref, vf_ref, gf_ref, qb_ref, kb_ref, vb_ref, gb_ref, s0_ref,
                of_ref, ob_ref, st_out_ref, st_sc, *, nchunk, dkh, dvh):
    j = pl.program_id(1)

    @pl.when(j == 0)
    def _():
        st_sc[...] = s0_ref[0]

    c = GLA_CHUNK
    bf = _gla_cumsum(gf_ref, nchunk=nchunk, rev=False)
    bb = _gla_cumsum(gb_ref, nchunk=nchunk, rev=True)
    worst = jnp.maximum(_gla_max_diag_exponent(bf, nchunk=nchunk, rev=False),
                        _gla_max_diag_exponent(bb, nchunk=nchunk, rev=True))
    factored_ok = jnp.max(worst) <= GLA_MAX_EXPONENT

    def sweep(exact_diag):
        kw = dict(nchunk=nchunk, dkh=dkh, dvh=dvh, exact_diag=exact_diag)
        sides = ((_gla_local(qf_ref, kf_ref, vf_ref, bf, rev=False, **kw), of_ref),
                 (_gla_local(qb_ref, kb_ref, vb_ref, bb, rev=True, **kw), ob_ref))
        for step in range(nchunk):
            for side, ((qt, local), o_ref) in enumerate(sides):
                n = step if side == 0 else nchunk - 1 - step
                for h in range(GLA_HEADS):
                    hk, hv = slice(h * dkh, (h + 1) * dkh), slice(h * dvh, (h + 1) * dvh)
                    dec, intra, inc = local[n, h]
                    st = st_sc[side, h]
                    o = _dot_nt(qt[n * c:(n + 1) * c, hk], st.astype(BF16)) + intra
                    o_ref[0, n * c:(n + 1) * c, hv] = o.astype(BF16)
                    st_sc[side, h] = dec * st + inc

    @pl.when(factored_ok)
    def _():
        sweep(False)

    @pl.when(jnp.logical_not(factored_ok))
    def _():
        sweep(True)

    @pl.when(j == pl.num_programs(1) - 1)
    def _():
        st_out_ref[0] = st_sc[...]


def _gla_scan(q, k, v, g, s0):
    b, t, dk = q.shape
    dv = v.shape[2]
    dkh, dvh = dk // GLA_HEADS, dv // GLA_HEADS
    nchunk = min(GLA_R, t // GLA_CHUNK)
    rows = nchunk * GLA_CHUNK
    assert t % rows == 0
    n = t // rows
    fwd = lambda w, cb=0: pl.BlockSpec((1, rows, w), lambda bi, j: (bi, j, cb))
    bwd = lambda w, cb=0: pl.BlockSpec((1, rows, w), lambda bi, j: (bi, n - 1 - j, cb))
    st_spec = pl.BlockSpec((1, 2, GLA_HEADS, dvh, dkh), lambda bi, j: (bi, 0, 0, 0, 0))
    kern = functools.partial(_gla_kernel, nchunk=nchunk, dkh=dkh, dvh=dvh)
    return pl.pallas_call(
        kern,
        grid=(b, n),
        in_specs=[fwd(dk), fwd(dk), fwd(dv), fwd(dk, 0), bwd(dk), bwd(dk), bwd(dv), bwd(dk, 1), st_spec],
        out_specs=[fwd(dv), bwd(dv), st_spec],
        out_shape=[jax.ShapeDtypeStruct((b, t, dv), BF16), jax.ShapeDtypeStruct((b, t, dv), BF16),
                   jax.ShapeDtypeStruct(s0.shape, F32)],
        scratch_shapes=[pltpu.VMEM((2, GLA_HEADS, dvh, dkh), F32)],
        compiler_params=_params(("parallel", "arbitrary")),
        name="gla_scan",
    )(q, k, v, g, q, k, v, g, s0)


def _post_kernel(*refs, d, f, gla, dvh):
    if gla:
        x_ref, of_ref, ob_ref, r_ref, on_ref, mod_ref, wo_ref, g2_ref, wgu_ref, wd_ref, o_ref = refs
        r = r_ref[...].astype(F32)
        gate = r * _sigmoid(r)
        parts = []
        for h in range(GLA_HEADS):
            cs = slice(h * dvh, (h + 1) * dvh)
            oh = of_ref[:, cs].astype(F32) + ob_ref[:, cs].astype(F32)
            ms = jnp.mean(oh * oh, axis=-1, keepdims=True)
            parts.append((oh * lax.rsqrt(ms + NORM_EPS) * on_ref[...] * gate[:, cs]).astype(BF16))
        a = jnp.concatenate(parts, axis=1)
    else:
        x_ref, a_ref, mod_ref, wo_ref, g2_ref, wgu_ref, wd_ref, o_ref = refs
        a = a_ref[...]
    m = mod_ref[0]
    gate1, shift2, scale2, gate2 = (m[:, 2 * d:3 * d], m[:, 3 * d:4 * d], m[:, 4 * d:5 * d], m[:, 5 * d:6 * d])
    x1 = x_ref[...] + gate1 * _dot(a, wo_ref[...])
    h2 = _norm_mod(x1, g2_ref[...], shift2, scale2).astype(BF16)
    acts = []
    for c0 in range(0, f, FFN_CHUNK):
        gt = _dot(h2, wgu_ref[:, c0:c0 + FFN_CHUNK])
        up = _dot(h2, wgu_ref[:, f + c0:f + c0 + FFN_CHUNK])
        acts.append((gt * _sigmoid(gt) * up).astype(BF16))
    o_ref[...] = x1 + gate2 * _dot(jnp.concatenate(acts, axis=1), wd_ref[...])


def _post(x2d, mix, modx, rows_per_mod, wo, g2, wgu, wd, *, gla_norm=None):
    n, d = x2d.shape
    f = wd.shape[0]
    tm = min(TM_POST, n)
    assert n % tm == 0 and rows_per_mod % tm == 0 and f % FFN_CHUNK == 0
    mod_blocks = rows_per_mod // tm
    row = lambda c: pl.BlockSpec((tm, c), lambda i: (i, 0))
    gla = gla_norm is not None
    if gla:
        ins = [x2d, *mix, gla_norm]
        specs = [row(d), row(d), row(d), row(d), _const_spec(gla_norm.shape)]
        dvh = gla_norm.shape[1]
    else:
        ins = [x2d, mix]
        specs = [row(d), row(d)]
        dvh = 0
    ins += [modx, wo, g2, wgu, wd]
    specs += [pl.BlockSpec((1, 1, 6 * d), lambda i: (i // mod_blocks, 0, 0)),
              _const_spec(wo.shape), _const_spec((1, d)), _const_spec(wgu.shape), _const_spec(wd.shape)]
    kern = functools.partial(_post_kernel, d=d, f=f, gla=gla, dvh=dvh)
    return pl.pallas_call(
        kern,
        grid=(n // tm,),
        in_specs=specs,
        out_specs=row(d),
        out_shape=jax.ShapeDtypeStruct((n, d), F32),
        compiler_params=_params(("parallel",), vmem=VMEM_LIMIT_LARGE),
        name="post_gla" if gla else "post_attn",
    )(*ins)


def _rope_tables(n_tokens, head_dim):
    rows = n_tokens // GRID_W
    rowp = jnp.broadcast_to(jnp.arange(rows)[:, None], (rows, GRID_W)).reshape(-1)
    colp = jnp.broadcast_to(jnp.arange(GRID_W)[None, :], (rows, GRID_W)).reshape(-1)
    half = head_dim // 2
    quarter = head_dim // 4
    inv = ROPE_BASE ** (-jnp.arange(0, half, 2, dtype=F32) / half)

    def axis_angles(pos):
        a = pos.astype(F32)[:, None] * inv[None, :]
        return jnp.concatenate([a, a], axis=-1)

    ang = jnp.concatenate([axis_angles(rowp), axis_angles(colp)], axis=-1)
    reps = LANES // head_dim
    cos = jnp.tile(jnp.cos(ang), (1, reps))
    sin = jnp.tile(jnp.sin(ang), (1, reps))
    first = (jnp.arange(LANES) % (2 * quarter)) < quarter
    sa = jnp.where(first[None, :], -sin, 0.0)
    sb = jnp.where(first[None, :], 0.0, sin)
    return cos, sa, sb


def _seg_matrix(seg):
    idx = jnp.arange(MXU_N) // seg
    return jnp.where(idx[:, None] == idx[None, :], 1.0 / seg, 0.0).astype(BF16)


def kernel(x, c, ctx, c_ctx, ada_w, ada_b, norm1_g, norm2_g, ffn_w_gu, ffn_w_down, diff_w_in, diff_w_out, diff_q_norm, diff_k_norm, diff_lambda_q1, diff_lambda_k1, diff_lambda_q2, diff_lambda_k2, diff_subln, gqa_w_in, gqa_w_out, gqa_q_norm, gqa_k_norm, gla_w_in, gla_gate_w1_fwd, gla_gate_w2_fwd, gla_gate_b_fwd, gla_gate_w1_bwd, gla_gate_w2_bwd, gla_gate_b_bwd, gla_out_norm, gla_w_out):
    b, l, d = x.shape
    lc = ctx.shape[1]
    depth = ada_w.shape[0]
    nl, nc = b * l, b * lc

    assert l % lc == 0 and l % GRID_W == 0 and lc % GLA_CHUNK == 0
    cond_rows = pl.cdiv(b + 1, SUBLANES) * SUBLANES
    cond = jnp.zeros((cond_rows, d), F32).at[:b].set(c).at[b].set(c_ctx)
    mod = _modulation(cond, ada_w, ada_b)

    xl = x.reshape(nl, d)
    xc = ctx.reshape(nc, d)
    ones_tab = (jnp.ones((lc, LANES), F32), jnp.zeros((lc, LANES), F32), jnp.zeros((lc, LANES), F32))
    dummy_lam = jnp.zeros((4, DIFF_DH), F32)
    dummy_sub = jnp.ones((1, LANES), F32)

    for i in range(depth):
        last = i == depth - 1
        kind, j = i % N_MIXERS, i // N_MIXERS
        modl = mod[i, :b].reshape(b, 1, 6 * d)
        modc = mod[i, b:b + 1].reshape(1, 1, 6 * d)
        g1 = norm1_g[i].reshape(1, d)
        g2 = norm2_g[i].reshape(1, d)
        wgu = ffn_w_gu[i].astype(BF16)
        wd = ffn_w_down[i].astype(BF16)

        if kind in (0, 1):
            if kind == 0:
                diff, dh = True, DIFF_DH
                w_in, w_out = diff_w_in[j].astype(BF16), diff_w_out[j].astype(BF16)
                gq = jnp.tile(diff_q_norm[j], LANES // dh).reshape(1, LANES) * (dh ** -0.5 * LOG2E)
                gk = jnp.tile(diff_k_norm[j], LANES // dh).reshape(1, LANES)
                nq = nk = nv = d // LANES
                lamp = jnp.stack([diff_lambda_q1[j], diff_lambda_k1[j], diff_lambda_q2[j], diff_lambda_k2[j]])
                subln = diff_subln[j].reshape(1, LANES)
                lambda_init = 0.8 - 0.6 * math.exp(-0.3 * i)
            else:
                diff, dh = False, GQA_DH
                w_in, w_out = gqa_w_in[j].astype(BF16), gqa_w_out[j].astype(BF16)
                gq = gqa_q_norm[j].reshape(1, LANES) * (dh ** -0.5 * LOG2E)
                gk = gqa_k_norm[j].reshape(1, LANES)
                nq = d // LANES
                nk = nv = (w_in.shape[1] - d) // (2 * LANES)
                lamp, subln, lambda_init = dummy_lam, dummy_sub, 0.0
            pmat = _seg_matrix(dh)
            cos, sa, sb = _rope_tables(l, dh)
            kw = dict(nq=nq, nk=nk, nv=nv, rot=dh // 4)
            ql, k_all, v_all = _pre_attn(xl, modl, l, g1, w_in, pmat, gq, gk, cos, sa, sb, l, l + lc, 0, None, **kw)
            qc, k_all, v_all = _pre_attn(xc, modc, nc, g1, w_in, pmat, gq, gk, *ones_tab, lc, l + lc, l,
                                         (k_all, v_all), **kw)
            r3 = lambda t, n: t.reshape(b, n, t.shape[1])
            shift = jnp.full((1, LANES), dh, F32) * jnp.max(jnp.abs(gq)) * jnp.max(jnp.abs(gk))
            akw = dict(diff=diff, lambda_init=lambda_init)
            ol = _attention(r3(ql, l), k_all, v_all, shift, lamp, subln, kv_rows=l + lc, kv_block=0, **akw)
            xl = _post(xl, ol.reshape(nl, d), modl, l, w_out, g2, wgu, wd)
            if not last:
                oc = _attention(r3(qc, lc), k_all, v_all, shift, lamp, subln, kv_rows=lc, kv_block=l // lc, **akw)
                xc = _post(xc, oc.reshape(nc, d), modc, nc, w_out, g2, wgu, wd)
        else:
            dk, dv = gla_gate_w2_fwd.shape[2], gla_w_out.shape[1]
            rank = gla_gate_w1_fwd.shape[2]
            w1 = jnp.zeros((d, MXU_N), F32).at[:, :rank].set(gla_gate_w1_fwd[j]).at[:, rank:2 * rank].set(gla_gate_w1_bwd[j])
            w_in = jnp.concatenate([gla_w_in[j], w1], axis=1).astype(BF16)
            w2 = (jnp.zeros((MXU_N, 2 * dk), F32).at[:rank, :dk].set(gla_gate_w2_fwd[j])
                  .at[rank:2 * rank, dk:].set(gla_gate_w2_bwd[j])).astype(BF16)
            gb = jnp.concatenate([gla_gate_b_fwd[j], gla_gate_b_bwd[j]]).reshape(1, 2 * dk)
            w_out = gla_w_out[j].astype(BF16)
            dkh, dvh = dk // GLA_HEADS, dv // GLA_HEADS
            onorm = gla_out_norm[j].reshape(1, dvh)
            kw = dict(dk=dk, dv=dv, qscale=dkh ** -0.5)
            ql, kl, vl, rl, gl = _pre_gla(xl, modl, l, g1, w_in, w2, gb, **kw)
            qc, kc, vc, rc, gc = _pre_gla(xc, modc, nc, g1, w_in, w2, gb, **kw)
            r3 = lambda t, n: t.reshape(b, n, t.shape[1])
            s0 = jnp.zeros((b, 2, GLA_HEADS, dvh, dkh), F32)
            ocf, ocb, sc = _gla_scan(r3(qc, lc), r3(kc, lc), r3(vc, lc), r3(gc, lc), s0)
            olf, olb, _ = _gla_scan(r3(ql, l), r3(kl, l), r3(vl, l), r3(gl, l), sc)
            xl = _post(xl, (olf.reshape(nl, dv), olb.reshape(nl, dv), rl), modl, l, w_out, g2, wgu, wd, gla_norm=onorm)
            if not last:
                xc = _post(xc, (ocf.reshape(nc, dv), ocb.reshape(nc, dv), rc), modc, nc, w_out, g2, wgu, wd, gla_norm=onorm)
    return xl.reshape(b, l, d)
```

```python
import functools
import math

import jax
import jax.numpy as jnp
from jax import lax
from jax.experimental import pallas as pl
from jax.experimental.pallas import tpu as pltpu

F32 = jnp.float32
BF16 = jnp.bfloat16

GRID_W = 64
N_MIXERS = 3
ROPE_BASE = 10000.0
NORM_EPS = 1e-6
DIFF_DH = 64
GQA_DH = 128
GQA_GROUP = 4
GLA_HEADS = 4
GLA_GATE_RANK = 16
GLA_TAU = 16.0
GLA_CHUNK = 64
GLA_SUB = 16
LOG2E = math.log2(math.e)
ATTN_MIN_DENOM = 1e-22
GLA_MAX_EXPONENT = 80.0

LANES = 128
SUBLANES = 8
MXU_N = 256
VMEM_LIMIT = 56 * 1024 * 1024
VMEM_LIMIT_LARGE = 62 * 1024 * 1024

MOD_TN = 1536
TM_PRE = 1024
TM_POST = 1024
FFN_CHUNK = 256
TQ_DIFF = 1024
TQ_GQA = 512
TK_ATTN = 3072
TK_ATTN_DIFF = 1024
TK_EXACT = 768
GLA_R = 4


def _const_spec(shape):
    nd = len(shape)
    return pl.BlockSpec(shape, lambda *_: (0,) * nd, pipeline_mode=pl.Buffered(1))


def _params(sem, vmem=VMEM_LIMIT):
    return pltpu.CompilerParams(dimension_semantics=sem, vmem_limit_bytes=vmem)


def _sigmoid(x):
    return 1.0 / (1.0 + jnp.exp(-x))


def _norm_mod(x, g, shift, scale):
    ms = jnp.mean(x * x, axis=-1, keepdims=True)
    return (x * lax.rsqrt(ms + NORM_EPS) * g) * (1.0 + scale) + shift


def _dot(a, b):
    return jnp.dot(a, b, preferred_element_type=F32)


def _dot_nt(a, b):
    return lax.dot_general(a, b, (((1,), (1,)), ((), ())), preferred_element_type=F32)


def _dot_tn(a, b):
    return lax.dot_general(a, b, (((0,), (0,)), ((), ())), preferred_element_type=F32)


def _split3(x):
    hi = x.astype(BF16)
    r1 = x - hi.astype(F32)
    mid = r1.astype(BF16)
    lo = (r1 - mid.astype(F32)).astype(BF16)
    return hi, mid, lo


def _mod_kernel(cond_ref, w_ref, b_ref, o_ref):
    c = cond_ref[...]
    s = (c * _sigmoid(c)).astype(BF16)
    o_ref[0] = _dot(s, w_ref[0].astype(BF16)) + b_ref[0]


def _modulation(cond, ada_w, ada_b):
    depth, d, n = ada_w.shape
    rows = cond.shape[0]
    tn = MOD_TN
    assert n % tn == 0 and rows % SUBLANES == 0
    return pl.pallas_call(
        _mod_kernel,
        grid=(depth, n // tn),
        in_specs=[
            pl.BlockSpec((rows, d), lambda i, j: (0, 0)),
            pl.BlockSpec((1, d, tn), lambda i, j: (i, 0, j)),
            pl.BlockSpec((1, 1, tn), lambda i, j: (i, 0, j)),
        ],
        out_specs=pl.BlockSpec((1, rows, tn), lambda i, j: (i, 0, j)),
        out_shape=jax.ShapeDtypeStruct((depth, rows, n), F32),
        compiler_params=_params(("parallel", "parallel")),
        name="adaln_mod",
    )(cond, ada_w, ada_b.reshape(depth, 1, n))


def _pre_attn_kernel(x_ref, mod_ref, g1_ref, w_ref, p_ref, gq_ref, gk_ref, cos_ref, sa_ref, sb_ref, *rest,
                     d, nq, nk, nv, rot):
    q_ref, k_ref, v_ref = rest[-3], rest[-2].at[0], rest[-1].at[0]
    x = x_ref[...]
    m = mod_ref[0]
    h = _norm_mod(x, g1_ref[...], m[:, 0:d], m[:, d:2 * d]).astype(BF16)
    cos, sa, sb = cos_ref[...], sa_ref[...], sb_ref[...]
    per = MXU_N // LANES

    def project(grp):
        return _dot(h, w_ref[:, grp * MXU_N:(grp + 1) * MXU_N])

    def emit(grp, y):
        slab0 = grp * per
        if slab0 >= nq + nk:
            j = slab0 - nq - nk
            v_ref[:, j * LANES:(j + per) * LANES] = y.astype(BF16)
            return
        ms = _dot((y * y).astype(BF16), p_ref[...])
        yn = y * lax.rsqrt(ms + NORM_EPS)
        for t in range(per):
            slab = slab0 + t
            out_ref, j, gain = (q_ref, slab, gq_ref) if slab < nq else (k_ref, slab - nq, gk_ref)
            xn = yn[:, t * LANES:(t + 1) * LANES] * gain[...]
            r = xn * cos + pltpu.roll(xn, LANES - rot, 1) * sa + pltpu.roll(xn, rot, 1) * sb
            out_ref[:, j * LANES:(j + 1) * LANES] = r.astype(BF16)

    ngrp = (nq + nk + nv) // per
    y = project(0)
    for grp in range(ngrp):
        y_next = project(grp + 1) if grp + 1 < ngrp else None
        emit(grp, y)
        y = y_next


def _pre_attn(x2d, modx, rows_per_mod, g1, w, pmat, gq, gk, cos, sa, sb, seq, kv_rows, kv_row0, kv_prev,
              *, nq, nk, nv, rot):
    n, d = x2d.shape
    tm = min(TM_PRE, seq)
    assert seq % tm == 0 and rows_per_mod % tm == 0 and kv_row0 % tm == 0
    assert nq % 2 == 0 and nk % 2 == 0 and nv % 2 == 0
    seq_blocks = seq // tm
    mod_blocks = rows_per_mod // tm
    kv_block0 = kv_row0 // tm
    row = lambda c: pl.BlockSpec((tm, c), lambda i: (i, 0))
    tab = pl.BlockSpec((tm, LANES), lambda i: (i % seq_blocks, 0))
    kv = lambda c: pl.BlockSpec((1, tm, c), lambda i: (i // seq_blocks, kv_block0 + i % seq_blocks, 0))
    ins = [x2d, modx, g1, w, pmat, gq, gk, cos, sa, sb]
    specs = [
        row(d),
        pl.BlockSpec((1, 1, 6 * d), lambda i: (i // mod_blocks, 0, 0)),
        _const_spec((1, d)),
        _const_spec(w.shape),
        _const_spec(pmat.shape),
        _const_spec((1, LANES)),
        _const_spec((1, LANES)),
        tab, tab, tab,
    ]
    aliases = {}
    if kv_prev is not None:
        aliases = {len(ins): 1, len(ins) + 1: 2}
        ins += list(kv_prev)
        specs += [pl.BlockSpec(memory_space=pl.ANY)] * 2
    batches = n // seq
    kern = functools.partial(_pre_attn_kernel, d=d, nq=nq, nk=nk, nv=nv, rot=rot)
    return pl.pallas_call(
        kern,
        grid=(n // tm,),
        in_specs=specs,
        out_specs=[row(nq * LANES), kv(nk * LANES), kv(nv * LANES)],
        out_shape=[jax.ShapeDtypeStruct((n, nq * LANES), BF16),
                   jax.ShapeDtypeStruct((batches, kv_rows, nk * LANES), BF16),
                   jax.ShapeDtypeStruct((batches, kv_rows, nv * LANES), BF16)],
        input_output_aliases=aliases,
        compiler_params=_params(("parallel",)),
        name="pre_attn",
    )(*ins)


def _pre_gla_kernel(x_ref, mod_ref, g1_ref, w_ref, w2_ref, gb_ref,
                    q_ref, k_ref, v_ref, r_ref, g_ref, *, d, dk, dv, qscale):
    x = x_ref[...]
    m = mod_ref[0]
    h = _norm_mod(x, g1_ref[...], m[:, 0:d], m[:, d:2 * d]).astype(BF16)
    bounds = (dk, 2 * dk, 2 * dk + dv, 2 * dk + 2 * dv)
    def project(c0):
        return _dot(h, w_ref[:, c0:c0 + MXU_N])

    z = _dot(project(bounds[3]).astype(BF16), w2_ref[...]) + gb_ref[...]
    logsig = jnp.minimum(z, 0.0) - jnp.log(1.0 + jnp.exp(-jnp.abs(z)))
    g_ref[...] = logsig * (1.0 / GLA_TAU)
    y_next = project(0)
    for c0 in range(0, bounds[3], MXU_N):
        y = y_next
        if c0 + MXU_N < bounds[3]:
            y_next = project(c0 + MXU_N)
        if c0 < bounds[0]:
            q_ref[:, c0:c0 + MXU_N] = (y * qscale).astype(BF16)
        elif c0 < bounds[1]:
            k_ref[:, c0 - bounds[0]:c0 - bounds[0] + MXU_N] = y.astype(BF16)
        elif c0 < bounds[2]:
            v_ref[:, c0 - bounds[1]:c0 - bounds[1] + MXU_N] = y.astype(BF16)
        else:
            r_ref[:, c0 - bounds[2]:c0 - bounds[2] + MXU_N] = y.astype(BF16)


def _pre_gla(x2d, modx, rows_per_mod, g1, w, w2, gb, *, dk, dv, qscale):
    n, d = x2d.shape
    tm = min(TM_PRE, n)
    assert n % tm == 0 and rows_per_mod % tm == 0 and dk % MXU_N == 0 and dv % MXU_N == 0
    mod_blocks = rows_per_mod // tm
    row = lambda c: pl.BlockSpec((tm, c), lambda i: (i, 0))
    kern = functools.partial(_pre_gla_kernel, d=d, dk=dk, dv=dv, qscale=qscale)
    return pl.pallas_call(
        kern,
        grid=(n // tm,),
        in_specs=[
            row(d),
            pl.BlockSpec((1, 1, 6 * d), lambda i: (i // mod_blocks, 0, 0)),
            _const_spec((1, d)),
            _const_spec(w.shape),
            _const_spec(w2.shape),
            _const_spec((1, 2 * dk)),
        ],
        out_specs=[row(dk), row(dk), row(dv), row(dv), row(2 * dk)],
        out_shape=[jax.ShapeDtypeStruct((n, c), t) for c, t in
                   ((dk, BF16), (dk, BF16), (dv, BF16), (dv, BF16), (2 * dk, F32))],
        compiler_params=_params(("parallel",)),
        name="pre_gla",
    )(x2d, modx, g1, w, w2, gb)


def _attn_kernel(q_ref, k_ref, v_ref, shift_ref, lam_ref, sub_ref, o_ref, m_sc, l1_sc, l_sc, acc_sc,
                 *maybe_p_sc, diff, tq, tk, tk_exact, lambda_init):
    p_sc = maybe_p_sc[0] if diff else None
    kv_rows = k_ref.shape[1]
    n_steps = kv_rows // tk
    lane = lax.broadcasted_iota(jnp.int32, (1, LANES), 1)
    lo_mask = jnp.where(lane < DIFF_DH, 1.0, 0.0).astype(BF16)
    shift = shift_ref[...]

    def stacked_q(r0):
        q = q_ref[0, pl.ds(r0, tq), :]
        if diff:
            return jnp.concatenate([q * lo_mask, q * (1.0 - lo_mask)], axis=0)
        return jnp.concatenate([q[:, g * LANES:(g + 1) * LANES] for g in range(GQA_GROUP)], axis=0)

    def probabilities(qs, i):
        s = _dot_nt(qs, k_ref[0, i * tk:(i + 1) * tk, :])
        psum, ps = None, []
        for c in range(tk // LANES):
            pc = jnp.exp2(s[:, c * LANES:(c + 1) * LANES] - shift)
            psum = pc if psum is None else psum + pc
            ps.append(pc.astype(BF16))
        return jnp.concatenate(ps, axis=1), psum

    def finish_diff(a, r0):
        ms = jnp.mean(a * a, axis=-1, keepdims=True)
        a = a * lax.rsqrt(ms + NORM_EPS) * sub_ref[...] * (1.0 - lambda_init)
        o_ref[0, pl.ds(r0, tq), :] = a.astype(BF16)

    def finish_gqa(o, r0):
        for g in range(GQA_GROUP):
            o_ref[0, pl.ds(r0, tq), g * LANES:(g + 1) * LANES] = o[g * tq:(g + 1) * tq].astype(BF16)

    if diff:
        lp = lam_ref[...]
        lam = (jnp.exp(jnp.sum(lp[0:1] * lp[1:2], axis=-1, keepdims=True))
               - jnp.exp(jnp.sum(lp[2:3] * lp[3:4], axis=-1, keepdims=True)) + lambda_init)

    def q_block(j, carry):
        r0 = pl.multiple_of(j * tq, tq)
        qs = stacked_q(r0)
        if diff:
            for i in range(n_steps):
                p, psum = probabilities(qs, i)
                p_sc[:, i * tk:(i + 1) * tk] = p
                if i == 0:
                    l_sc[...] = psum
                else:
                    l_sc[...] += psum
            denom = jnp.sum(l_sc[...], axis=-1, keepdims=True)
            ok = jnp.min(denom) >= ATTN_MIN_DENOM

            @pl.when(ok)
            def _():
                l1, l2 = denom[:tq], denom[tq:]
                rho = jnp.broadcast_to(lam * l1 / l2, (tq, LANES)).astype(BF16)
                acc = None
                for i in range(n_steps):
                    mix = [p_sc[0:tq, i * tk + c * LANES:i * tk + (c + 1) * LANES]
                           - rho * p_sc[tq:2 * tq, i * tk + c * LANES:i * tk + (c + 1) * LANES]
                           for c in range(tk // LANES)]
                    pv = _dot(jnp.concatenate(mix, axis=1), v_ref[0, i * tk:(i + 1) * tk, :])
                    acc = pv if acc is None else acc + pv
                finish_diff(acc / l1, r0)
        else:
            for i in range(n_steps):
                p, psum = probabilities(qs, i)
                pv = _dot(p, v_ref[0, i * tk:(i + 1) * tk, :])
                if i == 0:
                    l_sc[...] = psum
                    acc_sc[...] = pv
                else:
                    l_sc[...] += psum
                    acc_sc[...] += pv
            denom = jnp.sum(l_sc[...], axis=-1, keepdims=True)
            finish_gqa(acc_sc[...] / denom, r0)
            ok = jnp.min(denom) >= ATTN_MIN_DENOM

        @pl.when(jnp.logical_not(ok))
        def _():
            m_sc[...] = jnp.full(m_sc.shape, -jnp.inf, F32)
            l1_sc[...] = jnp.zeros(l1_sc.shape, F32)
            acc_sc[...] = jnp.zeros(acc_sc.shape, F32)

            def body(i, c):
                off = pl.multiple_of(i * tk_exact, tk_exact)
                s = _dot_nt(qs, k_ref[0, pl.ds(off, tk_exact), :])
                m_old = m_sc[...]
                m_new = jnp.maximum(m_old, jnp.max(s, axis=-1, keepdims=True))
                alpha = jnp.exp2(m_old - m_new)
                p = jnp.exp2(s - m_new)
                l1_sc[...] = alpha * l1_sc[...] + jnp.sum(p, axis=-1, keepdims=True)
                acc_sc[...] = alpha * acc_sc[...] + _dot(p.astype(BF16), v_ref[0, pl.ds(off, tk_exact), :])
                m_sc[...] = m_new
                return c

            lax.fori_loop(0, kv_rows // tk_exact, body, 0)
            o = acc_sc[...] / l1_sc[...]
            if diff:
                finish_diff(o[:tq] - lam * o[tq:], r0)
            else:
                finish_gqa(o, r0)

        return carry

    lax.fori_loop(0, q_ref.shape[1] // tq, q_block, 0)


def _attention(q, k, v, shift, lamp, subln, *, diff, lambda_init, kv_rows, kv_block):
    b, t, _ = q.shape
    tq = min(TQ_DIFF if diff else TQ_GQA, t)
    assert t % tq == 0 and kv_rows % LANES == 0
    qw = LANES if diff else GQA_GROUP * LANES
    heads = k.shape[2] // LANES
    stack = (2 if diff else GQA_GROUP) * tq
    tk_max = TK_ATTN_DIFF if diff else TK_ATTN
    steps = [c for c in range(LANES, min(tk_max, kv_rows) + 1, LANES) if kv_rows % c == 0]
    tk = max([c for c in steps if c % MXU_N == 0] or steps)
    kv_spec = pl.BlockSpec((1, kv_rows, LANES), lambda bi, h, *_: (bi, kv_block, h))
    if diff:
        grid = (b, heads, t // tq)
        q_spec = pl.BlockSpec((1, tq, qw), lambda bi, h, i: (bi, i, h))
    else:
        grid = (b, heads)
        q_spec = pl.BlockSpec((1, t, qw), lambda bi, h: (bi, 0, h))
    ins = [q, k, v, shift, lamp, subln]
    specs = [q_spec, kv_spec, kv_spec]
    specs += [pl.BlockSpec(a.shape, lambda bi, h, *_: (0, 0)) for a in (shift, lamp, subln)]
    tk_exact = max(c for c in steps if c <= TK_EXACT)
    kern = functools.partial(_attn_kernel, diff=diff, tq=tq, tk=tk, tk_exact=tk_exact, lambda_init=lambda_init)
    scratch = [pltpu.VMEM((stack, 1), F32), pltpu.VMEM((stack, 1), F32),
               pltpu.VMEM((stack, LANES), F32), pltpu.VMEM((stack, LANES), F32)]
    if diff:
        scratch.append(pltpu.VMEM((stack, kv_rows), BF16))
    return pl.pallas_call(
        kern,
        grid=grid,
        in_specs=specs,
        out_specs=q_spec,
        out_shape=jax.ShapeDtypeStruct((b, t, heads * qw), BF16),
        scratch_shapes=scratch,
        compiler_params=_params(("parallel",) * len(grid), vmem=VMEM_LIMIT_LARGE),
        name="attn_diff" if diff else "attn_gqa",
    )(*ins)


def _gla_cumsum(g_ref, *, nchunk, rev):
    c = GLA_CHUNK
    rows = nchunk * c
    ri = lax.broadcasted_iota(jnp.int32, (rows, rows), 0)
    ci = lax.broadcasted_iota(jnp.int32, (rows, rows), 1)
    causal = (ci >= ri) if rev else (ci <= ri)
    tri = jnp.where(jnp.logical_and(ri // c == ci // c, causal), 1.0, 0.0).astype(BF16)
    hi, mid, lo = _split3(g_ref[0])
    return _dot(tri, hi) + _dot(tri, mid) + _dot(tri, lo)


def _gla_sub_ref(bs, i, rev):
    nsub = GLA_CHUNK // GLA_SUB
    if rev:
        return bs[(i + 1) * GLA_SUB:(i + 1) * GLA_SUB + 1] if i < nsub - 1 else jnp.zeros_like(bs[0:1])
    return bs[i * GLA_SUB - 1:i * GLA_SUB] if i > 0 else jnp.zeros_like(bs[0:1])


def _gla_max_diag_exponent(bsum, *, nchunk, rev):
    c = GLA_CHUNK
    worst = None
    for n in range(nchunk):
        bs = bsum[n * c:(n + 1) * c]
        for i in range(c // GLA_SUB):
            far = i * GLA_SUB if rev else (i + 1) * GLA_SUB - 1
            e = _gla_sub_ref(bs, i, rev) - bs[far:far + 1]
            worst = e if worst is None else jnp.maximum(worst, e)
    return jnp.max(worst, axis=-1, keepdims=True)


def _gla_local(q_ref, k_ref, v_ref, bsum, *, nchunk, dkh, dvh, rev, exact_diag):
    c = GLA_CHUNK
    qf, kf = q_ref[0].astype(F32), k_ref[0].astype(F32)
    qt = (qf * jnp.exp(bsum)).astype(BF16)

    krow = lax.broadcasted_iota(jnp.int32, (c, bsum.shape[1]), 0)
    r64 = lax.broadcasted_iota(jnp.int32, (c, c), 0)
    c64 = lax.broadcasted_iota(jnp.int32, (c, c), 1)
    keep = (c64 >= r64) if rev else (c64 <= r64)
    if exact_diag:
        keep = (c64 // GLA_SUB > r64 // GLA_SUB) if rev else (c64 // GLA_SUB < r64 // GLA_SUB)
    rs = lax.broadcasted_iota(jnp.int32, (GLA_SUB, c), 0)
    cs = lax.broadcasted_iota(jnp.int32, (GLA_SUB, c), 1)
    nsub = c // GLA_SUB
    hks = [slice(h * dkh, (h + 1) * dkh) for h in range(GLA_HEADS)]
    scaled, diag = [], {}
    for n in range(nchunk):
        bs, kc, qc = bsum[n * c:(n + 1) * c], kf[n * c:(n + 1) * c], qf[n * c:(n + 1) * c]
        b_tot = bs[0:1] if rev else bs[c - 1:c]
        kd = (kc * jnp.exp(b_tot - bs)).astype(BF16)
        qis, kis = [], []
        for i in range(nsub):
            lo_r, hi_r = i * GLA_SUB, (i + 1) * GLA_SUB
            ref = _gla_sub_ref(bs, i, rev)
            if exact_diag:
                valid = (krow >= hi_r) if rev else (krow < lo_r)
            else:
                valid = (krow >= lo_r) if rev else (krow < hi_r)
            kis.append((kc * jnp.exp(jnp.where(valid, ref - bs, 0.0))).astype(BF16))
            qis.append((qc[lo_r:hi_r] * jnp.exp(bs[lo_r:hi_r] - ref)).astype(BF16))
            if exact_diag:
                cols = [jnp.zeros((GLA_SUB, c), F32) for _ in range(GLA_HEADS)]
                for s in range(lo_r, hi_r):
                    e = jnp.exp(jnp.minimum(bs[lo_r:hi_r] - bs[s:s + 1], 0.0))
                    prod = qc[lo_r:hi_r] * e * kc[s:s + 1]
                    pair = (rs + lo_r <= s) if rev else (rs + lo_r >= s)
                    for h in range(GLA_HEADS):
                        col = jnp.sum(prod[:, hks[h]], axis=-1, keepdims=True)
                        cols[h] = cols[h] + jnp.where(jnp.logical_and(cs == s, pair), col, 0.0)
                for h in range(GLA_HEADS):
                    diag[n, h, i] = cols[h]
        scaled.append((jnp.exp(b_tot), kd, qis, kis))

    pairs = [(n, h) for n in range(nchunk) for h in range(GLA_HEADS)]
    vals = {(n, h): v_ref[0, n * c:(n + 1) * c, h * dvh:(h + 1) * dvh] for n, h in pairs}
    blocks = {(n, h): [_dot_nt(scaled[n][2][i][:, hks[h]], scaled[n][3][i][:, hks[h]]) for i in range(nsub)]
              for n, h in pairs}
    inc = {(n, h): _dot_tn(vals[n, h], scaled[n][1][:, hks[h]]) for n, h in pairs}
    amat = {}
    for n, h in pairs:
        a = jnp.where(keep, jnp.concatenate(blocks[n, h], axis=0), 0.0)
        if exact_diag:
            a = a + jnp.concatenate([diag[n, h, i] for i in range(nsub)], axis=0)
        amat[n, h] = a.astype(BF16)
    intra = {p: _dot(amat[p], vals[p]) for p in pairs}
    local = {(n, h): (scaled[n][0][:, hks[h]], intra[n, h], inc[n, h]) for n, h in pairs}
    return qt, local


def _gla_kernel(qf_ref, kf_ref, vf_ref, gf_ref, qb_ref, kb_ref, vb_ref, gb_ref, s0_ref,
                of_ref, ob_ref, st_out_ref, st_sc, *, nchunk, dkh, dvh):
    j = pl.program_id(1)

    @pl.when(j == 0)
    def _():
        st_sc[...] = s0_ref[0]

    c = GLA_CHUNK
    bf = _gla_cumsum(gf_ref, nchunk=nchunk, rev=False)
    bb = _gla_cumsum(gb_ref, nchunk=nchunk, rev=True)
    worst = jnp.maximum(_gla_max_diag_exponent(bf, nchunk=nchunk, rev=False),
                        _gla_max_diag_exponent(bb, nchunk=nchunk, rev=True))
    factored_ok = jnp.max(worst) <= GLA_MAX_EXPONENT

    def sweep(exact_diag):
        kw = dict(nchunk=nchunk, dkh=dkh, dvh=dvh, exact_diag=exact_diag)
        sides = ((_gla_local(qf_ref, kf_ref, vf_ref, bf, rev=False, **kw), of_ref),
                 (_gla_local(qb_ref, kb_ref, vb_ref, bb, rev=True, **kw), ob_ref))
        for step in range(nchunk):
            for side, ((qt, local), o_ref) in enumerate(sides):
                n = step if side == 0 else nchunk - 1 - step
                for h in range(GLA_HEADS):
                    hk, hv = slice(h * dkh, (h + 1) * dkh), slice(h * dvh, (h + 1) * dvh)
                    dec, intra, inc = local[n, h]
                    st = st_sc[side, h]
                    o = _dot_nt(qt[n * c:(n + 1) * c, hk], st.astype(BF16)) + intra
                    o_ref[0, n * c:(n + 1) * c, hv] = o.astype(BF16)
                    st_sc[side, h] = dec * st + inc

    @pl.when(factored_ok)
    def _():
        sweep(False)

    @pl.when(jnp.logical_not(factored_ok))
    def _():
        sweep(True)

    @pl.when(j == pl.num_programs(1) - 1)
    def _():
        st_out_ref[0] = st_sc[...]


def _gla_scan(q, k, v, g, s0):
    b, t, dk = q.shape
    dv = v.shape[2]
    dkh, dvh = dk // GLA_HEADS, dv // GLA_HEADS
    nchunk = min(GLA_R, t // GLA_CHUNK)
    rows = nchunk * GLA_CHUNK
    assert t % rows == 0
    n = t // rows
    fwd = lambda w, cb=0: pl.BlockSpec((1, rows, w), lambda bi, j: (bi, j, cb))
    bwd = lambda w, cb=0: pl.BlockSpec((1, rows, w), lambda bi, j: (bi, n - 1 - j, cb))
    st_spec = pl.BlockSpec((1, 2, GLA_HEADS, dvh, dkh), lambda bi, j: (bi, 0, 0, 0, 0))
    kern = functools.partial(_gla_kernel, nchunk=nchunk, dkh=dkh, dvh=dvh)
    return pl.pallas_call(
        kern,
        grid=(b, n),
        in_specs=[fwd(dk), fwd(dk), fwd(dv), fwd(dk, 0), bwd(dk), bwd(dk), bwd(dv), bwd(dk, 1), st_spec],
        out_specs=[fwd(dv), bwd(dv), st_spec],
        out_shape=[jax.ShapeDtypeStruct((b, t, dv), BF16), jax.ShapeDtypeStruct((b, t, dv), BF16),
                   jax.ShapeDtypeStruct(s0.shape, F32)],
        scratch_shapes=[pltpu.VMEM((2, GLA_HEADS, dvh, dkh), F32)],
        compiler_params=_params(("parallel", "arbitrary")),
        name="gla_scan",
    )(q, k, v, g, q, k, v, g, s0)


def _post_kernel(*refs, d, f, gla, dvh):
    if gla:
        x_ref, of_ref, ob_ref, r_ref, on_ref, mod_ref, wo_ref, g2_ref, wgu_ref, wd_ref, o_ref = refs
        r = r_ref[...].astype(F32)
        gate = r * _sigmoid(r)
        parts = []
        for h in range(GLA_HEADS):
            cs = slice(h * dvh, (h + 1) * dvh)
            oh = of_ref[:, cs].astype(F32) + ob_ref[:, cs].astype(F32)
            ms = jnp.mean(oh * oh, axis=-1, keepdims=True)
            parts.append((oh * lax.rsqrt(ms + NORM_EPS) * on_ref[...] * gate[:, cs]).astype(BF16))
        a = jnp.concatenate(parts, axis=1)
    else:
        x_ref, a_ref, mod_ref, wo_ref, g2_ref, wgu_ref, wd_ref, o_ref = refs
        a = a_ref[...]
    m = mod_ref[0]
    gate1, shift2, scale2, gate2 = (m[:, 2 * d:3 * d], m[:, 3 * d:4 * d], m[:, 4 * d:5 * d], m[:, 5 * d:6 * d])
    x1 = x_ref[...] + gate1 * _dot(a, wo_ref[...])
    h2 = _norm_mod(x1, g2_ref[...], shift2, scale2).astype(BF16)
    acts = []
    for c0 in range(0, f, FFN_CHUNK):
        gt = _dot(h2, wgu_ref[:, c0:c0 + FFN_CHUNK])
        up = _dot(h2, wgu_ref[:, f + c0:f + c0 + FFN_CHUNK])
        acts.append((gt * _sigmoid(gt) * up).astype(BF16))
    o_ref[...] = x1 + gate2 * _dot(jnp.concatenate(acts, axis=1), wd_ref[...])


def _post(x2d, mix, modx, rows_per_mod, wo, g2, wgu, wd, *, gla_norm=None):
    n, d = x2d.shape
    f = wd.shape[0]
    tm = min(TM_POST, n)
    assert n % tm == 0 and rows_per_mod % tm == 0 and f % FFN_CHUNK == 0
    mod_blocks = rows_per_mod // tm
    row = lambda c: pl.BlockSpec((tm, c), lambda i: (i, 0))
    gla = gla_norm is not None
    if gla:
        ins = [x2d, *mix, gla_norm]
        specs = [row(d), row(d), row(d), row(d), _const_spec(gla_norm.shape)]
        dvh = gla_norm.shape[1]
    else:
        ins = [x2d, mix]
        specs = [row(d), row(d)]
        dvh = 0
    ins += [modx, wo, g2, wgu, wd]
    specs += [pl.BlockSpec((1, 1, 6 * d), lambda i: (i // mod_blocks, 0, 0)),
              _const_spec(wo.shape), _const_spec((1, d)), _const_spec(wgu.shape), _const_spec(wd.shape)]
    kern = functools.partial(_post_kernel, d=d, f=f, gla=gla, dvh=dvh)
    return pl.pallas_call(
        kern,
        grid=(n // tm,),
        in_specs=specs,
        out_specs=row(d),
        out_shape=jax.ShapeDtypeStruct((n, d), F32),
        compiler_params=_params(("parallel",), vmem=VMEM_LIMIT_LARGE),
        name="post_gla" if gla else "post_attn",
    )(*ins)


def _rope_tables(n_tokens, head_dim):
    rows = n_tokens // GRID_W
    rowp = jnp.broadcast_to(jnp.arange(rows)[:, None], (rows, GRID_W)).reshape(-1)
    colp = jnp.broadcast_to(jnp.arange(GRID_W)[None, :], (rows, GRID_W)).reshape(-1)
    half = head_dim // 2
    quarter = head_dim // 4
    inv = ROPE_BASE ** (-jnp.arange(0, half, 2, dtype=F32) / half)

    def axis_angles(pos):
        a = pos.astype(F32)[:, None] * inv[None, :]
        return jnp.concatenate([a, a], axis=-1)

    ang = jnp.concatenate([axis_angles(rowp), axis_angles(colp)], axis=-1)
    reps = LANES // head_dim
    cos = jnp.tile(jnp.cos(ang), (1, reps))
    sin = jnp.tile(jnp.sin(ang), (1, reps))
    first = (jnp.arange(LANES) % (2 * quarter)) < quarter
    sa = jnp.where(first[None, :], -sin, 0.0)
    sb = jnp.where(first[None, :], 0.0, sin)
    return cos, sa, sb


def _seg_matrix(seg):
    idx = jnp.arange(MXU_N) // seg
    return jnp.where(idx[:, None] == idx[None, :], 1.0 / seg, 0.0).astype(BF16)


def kernel(x, c, ctx, c_ctx, ada_w, ada_b, norm1_g, norm2_g, ffn_w_gu, ffn_w_down, diff_w_in, diff_w_out, diff_q_norm, diff_k_norm, diff_lambda_q1, diff_lambda_k1, diff_lambda_q2, diff_lambda_k2, diff_subln, gqa_w_in, gqa_w_out, gqa_q_norm, gqa_k_norm, gla_w_in, gla_gate_w1_fwd, gla_gate_w2_fwd, gla_gate_b_fwd, gla_gate_w1_bwd, gla_gate_w2_bwd, gla_gate_b_bwd, gla_out_norm, gla_w_out):
    b, l, d = x.shape
    lc = ctx.shape[1]
    depth = ada_w.shape[0]
    nl, nc = b * l, b * lc

    assert l % lc == 0 and l % GRID_W == 0 and lc % GLA_CHUNK == 0
    cond_rows = pl.cdiv(b + 1, SUBLANES) * SUBLANES
    cond = jnp.zeros((cond_rows, d), F32).at[:b].set(c).at[b].set(c_ctx)
    mod = _modulation(cond, ada_w, ada_b)

    xl = x.reshape(nl, d)
    xc = ctx.reshape(nc, d)
    ones_tab = (jnp.ones((lc, LANES), F32), jnp.zeros((lc, LANES), F32), jnp.zeros((lc, LANES), F32))
    dummy_lam = jnp.zeros((4, DIFF_DH), F32)
    dummy_sub = jnp.ones((1, LANES), F32)

    for i in range(depth):
        last = i == depth - 1
        kind, j = i % N_MIXERS, i // N_MIXERS
        modl = mod[i, :b].reshape(b, 1, 6 * d)
        modc = mod[i, b:b + 1].reshape(1, 1, 6 * d)
        g1 = norm1_g[i].reshape(1, d)
        g2 = norm2_g[i].reshape(1, d)
        wgu = ffn_w_gu[i].astype(BF16)
        wd = ffn_w_down[i].astype(BF16)

        if kind in (0, 1):
            if kind == 0:
                diff, dh = True, DIFF_DH
                w_in, w_out = diff_w_in[j].astype(BF16), diff_w_out[j].astype(BF16)
                gq = jnp.tile(diff_q_norm[j], LANES // dh).reshape(1, LANES) * (dh ** -0.5 * LOG2E)
                gk = jnp.tile(diff_k_norm[j], LANES // dh).reshape(1, LANES)
                nq = nk = nv = d // LANES
                lamp = jnp.stack([diff_lambda_q1[j], diff_lambda_k1[j], diff_lambda_q2[j], diff_lambda_k2[j]])
                subln = diff_subln[j].reshape(1, LANES)
                lambda_init = 0.8 - 0.6 * math.exp(-0.3 * i)
            else:
                diff, dh = False, GQA_DH
                w_in, w_out = gqa_w_in[j].astype(BF16), gqa_w_out[j].astype(BF16)
                gq = gqa_q_norm[j].reshape(1, LANES) * (dh ** -0.5 * LOG2E)
                gk = gqa_k_norm[j].reshape(1, LANES)
                nq = d // LANES
                nk = nv = (w_in.shape[1] - d) // (2 * LANES)
                lamp, subln, lambda_init = dummy_lam, dummy_sub, 0.0
            pmat = _seg_matrix(dh)
            cos, sa, sb = _rope_tables(l, dh)
            kw = dict(nq=nq, nk=nk, nv=nv, rot=dh // 4)
            ql, k_all, v_all = _pre_attn(xl, modl, l, g1, w_in, pmat, gq, gk, cos, sa, sb, l, l + lc, 0, None, **kw)
            qc, k_all, v_all = _pre_attn(xc, modc, nc, g1, w_in, pmat, gq, gk, *ones_tab, lc, l + lc, l,
                                         (k_all, v_all), **kw)
            r3 = lambda t, n: t.reshape(b, n, t.shape[1])
            shift = jnp.full((1, LANES), dh, F32) * jnp.max(jnp.abs(gq)) * jnp.max(jnp.abs(gk))
            akw = dict(diff=diff, lambda_init=lambda_init)
            ol = _attention(r3(ql, l), k_all, v_all, shift, lamp, subln, kv_rows=l + lc, kv_block=0, **akw)
            xl = _post(xl, ol.reshape(nl, d), modl, l, w_out, g2, wgu, wd)
            if not last:
                oc = _attention(r3(qc, lc), k_all, v_all, shift, lamp, subln, kv_rows=lc, kv_block=l // lc, **akw)
                xc = _post(xc, oc.reshape(nc, d), modc, nc, w_out, g2, wgu, wd)
        else:
            dk, dv = gla_gate_w2_fwd.shape[2], gla_w_out.shape[1]
            rank = gla_gate_w1_fwd.shape[2]
            w1 = jnp.zeros((d, MXU_N), F32).at[:, :rank].set(gla_gate_w1_fwd[j]).at[:, rank:2 * rank].set(gla_gate_w1_bwd[j])
            w_in = jnp.concatenate([gla_w_in[j], w1], axis=1).astype(BF16)
            w2 = (jnp.zeros((MXU_N, 2 * dk), F32).at[:rank, :dk].set(gla_gate_w2_fwd[j])
                  .at[rank:2 * rank, dk:].set(gla_gate_w2_bwd[j])).astype(BF16)
            gb = jnp.concatenate([gla_gate_b_fwd[j], gla_gate_b_bwd[j]]).reshape(1, 2 * dk)
            w_out = gla_w_out[j].astype(BF16)
            dkh, dvh = dk // GLA_HEADS, dv // GLA_HEADS
            onorm = gla_out_norm[j].reshape(1, dvh)
            kw = dict(dk=dk, dv=dv, qscale=dkh ** -0.5)
            ql, kl, vl, rl, gl = _pre_gla(xl, modl, l, g1, w_in, w2, gb, **kw)
            qc, kc, vc, rc, gc = _pre_gla(xc, modc, nc, g1, w_in, w2, gb, **kw)
            r3 = lambda t, n: t.reshape(b, n, t.shape[1])
            s0 = jnp.zeros((b, 2, GLA_HEADS, dvh, dkh), F32)
            ocf, ocb, sc = _gla_scan(r3(qc, lc), r3(kc, lc), r3(vc, lc), r3(gc, lc), s0)
            olf, olb, _ = _gla_scan(r3(ql, l), r3(kl, l), r3(vl, l), r3(gl, l), sc)
            xl = _post(xl, (olf.reshape(nl, dv), olb.reshape(nl, dv), rl), modl, l, w_out, g2, wgu, wd, gla_norm=onorm)
            if not last:
                xc = _post(xc, (ocf.reshape(nc, dv), ocb.reshape(nc, dv), rc), modc, nc, w_out, g2, wgu, wd, gla_norm=onorm)
    return xl.reshape(b, l, d)
```

```python
import functools
import math

import jax
import jax.numpy as jnp
from jax import lax
from jax.experimental import pallas as pl
from jax.experimental.pallas import tpu as pltpu

F32 = jnp.float32
BF16 = jnp.bfloat16

GRID_W = 64
N_MIXERS = 3
ROPE_BASE = 10000.0
NORM_EPS = 1e-6
DIFF_DH = 64
GQA_DH = 128
GQA_GROUP = 4
GLA_HEADS = 4
GLA_GATE_RANK = 16
GLA_TAU = 16.0
GLA_CHUNK = 64
GLA_SUB = 16
LOG2E = math.log2(math.e)
ATTN_MIN_DENOM = 1e-22
GLA_MAX_EXPONENT = 80.0

LANES = 128
SUBLANES = 8
MXU_N = 256
VMEM_LIMIT = 56 * 1024 * 1024
VMEM_LIMIT_LARGE = 62 * 1024 * 1024

MOD_TN = 1536
TM_PRE = 1024
TM_POST = 1024
FFN_CHUNK = 256
TQ_DIFF = 1024
TQ_GQA = 1024
TK_ATTN = 3072
TK_ATTN_DIFF = 1024
TK_EXACT = 768
GLA_R = 4


def _const_spec(shape):
    nd = len(shape)
    return pl.BlockSpec(shape, lambda *_: (0,) * nd, pipeline_mode=pl.Buffered(1))


def _params(sem, vmem=VMEM_LIMIT):
    return pltpu.CompilerParams(dimension_semantics=sem, vmem_limit_bytes=vmem)


def _sigmoid(x):
    return 1.0 / (1.0 + jnp.exp(-x))


def _norm_mod(x, g, shift, scale):
    ms = jnp.mean(x * x, axis=-1, keepdims=True)
    return (x * lax.rsqrt(ms + NORM_EPS) * g) * (1.0 + scale) + shift


def _dot(a, b):
    return jnp.dot(a, b, preferred_element_type=F32)


def _dot_nt(a, b):
    return lax.dot_general(a, b, (((1,), (1,)), ((), ())), preferred_element_type=F32)


def _dot_tn(a, b):
    return lax.dot_general(a, b, (((0,), (0,)), ((), ())), preferred_element_type=F32)


def _split3(x):
    hi = x.astype(BF16)
    r1 = x - hi.astype(F32)
    mid = r1.astype(BF16)
    lo = (r1 - mid.astype(F32)).astype(BF16)
    return hi, mid, lo


def _mod_kernel(cond_ref, w_ref, b_ref, o_ref):
    c = cond_ref[...]
    s = (c * _sigmoid(c)).astype(BF16)
    o_ref[0] = _dot(s, w_ref[0].astype(BF16)) + b_ref[0]


def _modulation(cond, ada_w, ada_b):
    depth, d, n = ada_w.shape
    rows = cond.shape[0]
    tn = MOD_TN
    assert n % tn == 0 and rows % SUBLANES == 0
    return pl.pallas_call(
        _mod_kernel,
        grid=(depth, n // tn),
        in_specs=[
            pl.BlockSpec((rows, d), lambda i, j: (0, 0)),
            pl.BlockSpec((1, d, tn), lambda i, j: (i, 0, j)),
            pl.BlockSpec((1, 1, tn), lambda i, j: (i, 0, j)),
        ],
        out_specs=pl.BlockSpec((1, rows, tn), lambda i, j: (i, 0, j)),
        out_shape=jax.ShapeDtypeStruct((depth, rows, n), F32),
        compiler_params=_params(("parallel", "parallel")),
        name="adaln_mod",
    )(cond, ada_w, ada_b.reshape(depth, 1, n))


def _pre_attn_kernel(x_ref, mod_ref, g1_ref, w_ref, p_ref, gq_ref, gk_ref, cos_ref, sa_ref, sb_ref, *rest,
                     d, nq, nk, nv, rot):
    q_ref, k_ref, v_ref = rest[-3], rest[-2].at[0], rest[-1].at[0]
    x = x_ref[...]
    m = mod_ref[0]
    h = _norm_mod(x, g1_ref[...], m[:, 0:d], m[:, d:2 * d]).astype(BF16)
    cos, sa, sb = cos_ref[...], sa_ref[...], sb_ref[...]
    per = MXU_N // LANES

    def project(grp):
        return _dot(h, w_ref[:, grp * MXU_N:(grp + 1) * MXU_N])

    def emit(grp, y):
        slab0 = grp * per
        if slab0 >= nq + nk:
            j = slab0 - nq - nk
            v_ref[:, j * LANES:(j + per) * LANES] = y.astype(BF16)
            return
        ms = _dot((y * y).astype(BF16), p_ref[...])
        yn = y * lax.rsqrt(ms + NORM_EPS)
        for t in range(per):
            slab = slab0 + t
            out_ref, j, gain = (q_ref, slab, gq_ref) if slab < nq else (k_ref, slab - nq, gk_ref)
            xn = yn[:, t * LANES:(t + 1) * LANES] * gain[...]
            r = xn * cos + pltpu.roll(xn, LANES - rot, 1) * sa + pltpu.roll(xn, rot, 1) * sb
            out_ref[:, j * LANES:(j + 1) * LANES] = r.astype(BF16)

    ngrp = (nq + nk + nv) // per
    y = project(0)
    for grp in range(ngrp):
        y_next = project(grp + 1) if grp + 1 < ngrp else None
        emit(grp, y)
        y = y_next


def _pre_attn(x2d, modx, rows_per_mod, g1, w, pmat, gq, gk, cos, sa, sb, seq, kv_rows, kv_row0, kv_prev,
              *, nq, nk, nv, rot):
    n, d = x2d.shape
    tm = min(TM_PRE, seq)
    assert seq % tm == 0 and rows_per_mod % tm == 0 and kv_row0 % tm == 0
    assert nq % 2 == 0 and nk % 2 == 0 and nv % 2 == 0
    seq_blocks = seq // tm
    mod_blocks = rows_per_mod // tm
    kv_block0 = kv_row0 // tm
    row = lambda c: pl.BlockSpec((tm, c), lambda i: (i, 0))
    tab = pl.BlockSpec((tm, LANES), lambda i: (i % seq_blocks, 0))
    kv = lambda c: pl.BlockSpec((1, tm, c), lambda i: (i // seq_blocks, kv_block0 + i % seq_blocks, 0))
    ins = [x2d, modx, g1, w, pmat, gq, gk, cos, sa, sb]
    specs = [
        row(d),
        pl.BlockSpec((1, 1, 6 * d), lambda i: (i // mod_blocks, 0, 0)),
        _const_spec((1, d)),
        _const_spec(w.shape),
        _const_spec(pmat.shape),
        _const_spec((1, LANES)),
        _const_spec((1, LANES)),
        tab, tab, tab,
    ]
    aliases = {}
    if kv_prev is not None:
        aliases = {len(ins): 1, len(ins) + 1: 2}
        ins += list(kv_prev)
        specs += [pl.BlockSpec(memory_space=pl.ANY)] * 2
    batches = n // seq
    kern = functools.partial(_pre_attn_kernel, d=d, nq=nq, nk=nk, nv=nv, rot=rot)
    return pl.pallas_call(
        kern,
        grid=(n // tm,),
        in_specs=specs,
        out_specs=[row(nq * LANES), kv(nk * LANES), kv(nv * LANES)],
        out_shape=[jax.ShapeDtypeStruct((n, nq * LANES), BF16),
                   jax.ShapeDtypeStruct((batches, kv_rows, nk * LANES), BF16),
                   jax.ShapeDtypeStruct((batches, kv_rows, nv * LANES), BF16)],
        input_output_aliases=aliases,
        compiler_params=_params(("parallel",)),
        name="pre_attn",
    )(*ins)


def _pre_gla_kernel(x_ref, mod_ref, g1_ref, w_ref, w2_ref, gb_ref,
                    q_ref, k_ref, v_ref, r_ref, g_ref, *, d, dk, dv, qscale):
    x = x_ref[...]
    m = mod_ref[0]
    h = _norm_mod(x, g1_ref[...], m[:, 0:d], m[:, d:2 * d]).astype(BF16)
    bounds = (dk, 2 * dk, 2 * dk + dv, 2 * dk + 2 * dv)
    def project(c0):
        return _dot(h, w_ref[:, c0:c0 + MXU_N])

    z = _dot(project(bounds[3]).astype(BF16), w2_ref[...]) + gb_ref[...]
    logsig = jnp.minimum(z, 0.0) - jnp.log(1.0 + jnp.exp(-jnp.abs(z)))
    g_ref[...] = logsig * (1.0 / GLA_TAU)
    y_next = project(0)
    for c0 in range(0, bounds[3], MXU_N):
        y = y_next
        if c0 + MXU_N < bounds[3]:
            y_next = project(c0 + MXU_N)
        if c0 < bounds[0]:
            q_ref[:, c0:c0 + MXU_N] = (y * qscale).astype(BF16)
        elif c0 < bounds[1]:
            k_ref[:, c0 - bounds[0]:c0 - bounds[0] + MXU_N] = y.astype(BF16)
        elif c0 < bounds[2]:
            v_ref[:, c0 - bounds[1]:c0 - bounds[1] + MXU_N] = y.astype(BF16)
        else:
            r_ref[:, c0 - bounds[2]:c0 - bounds[2] + MXU_N] = y.astype(BF16)


def _pre_gla(x2d, modx, rows_per_mod, g1, w, w2, gb, *, dk, dv, qscale):
    n, d = x2d.shape
    tm = min(TM_PRE, n)
    assert n % tm == 0 and rows_per_mod % tm == 0 and dk % MXU_N == 0 and dv % MXU_N == 0
    mod_blocks = rows_per_mod // tm
    row = lambda c: pl.BlockSpec((tm, c), lambda i: (i, 0))
    kern = functools.partial(_pre_gla_kernel, d=d, dk=dk, dv=dv, qscale=qscale)
    return pl.pallas_call(
        kern,
        grid=(n // tm,),
        in_specs=[
            row(d),
            pl.BlockSpec((1, 1, 6 * d), lambda i: (i // mod_blocks, 0, 0)),
            _const_spec((1, d)),
            _const_spec(w.shape),
            _const_spec(w2.shape),
            _const_spec((1, 2 * dk)),
        ],
        out_specs=[row(dk), row(dk), row(dv), row(dv), row(2 * dk)],
        out_shape=[jax.ShapeDtypeStruct((n, c), t) for c, t in
                   ((dk, BF16), (dk, BF16), (dv, BF16), (dv, BF16), (2 * dk, F32))],
        compiler_params=_params(("parallel",)),
        name="pre_gla",
    )(x2d, modx, g1, w, w2, gb)


def _attn_kernel(q_ref, k_ref, v_ref, shift_ref, lam_ref, sub_ref, o_ref, m_sc, l1_sc, l_sc, acc_sc,
                 *maybe_p_sc, diff, tq, tk, tk_exact, lambda_init):
    p_sc = maybe_p_sc[0] if diff else None
    kv_rows = k_ref.shape[1]
    n_steps = kv_rows // tk
    lane = lax.broadcasted_iota(jnp.int32, (1, LANES), 1)
    lo_mask = jnp.where(lane < DIFF_DH, 1.0, 0.0).astype(BF16)
    shift = shift_ref[...]

    def stacked_q(r0):
        q = q_ref[0, pl.ds(r0, tq), :]
        if diff:
            return jnp.concatenate([q * lo_mask, q * (1.0 - lo_mask)], axis=0)
        return jnp.concatenate([q[:, g * LANES:(g + 1) * LANES] for g in range(GQA_GROUP)], axis=0)

    def probabilities(qs, i):
        s = _dot_nt(qs, k_ref[0, i * tk:(i + 1) * tk, :])
        psum, ps = None, []
        for c in range(tk // LANES):
            pc = jnp.exp2(s[:, c * LANES:(c + 1) * LANES] - shift)
            psum = pc if psum is None else psum + pc
            ps.append(pc.astype(BF16))
        return jnp.concatenate(ps, axis=1), psum

    def finish_diff(a, r0):
        ms = jnp.mean(a * a, axis=-1, keepdims=True)
        a = a * lax.rsqrt(ms + NORM_EPS) * sub_ref[...] * (1.0 - lambda_init)
        o_ref[0, pl.ds(r0, tq), :] = a.astype(BF16)

    def finish_gqa(o, r0):
        for g in range(GQA_GROUP):
            o_ref[0, pl.ds(r0, tq), g * LANES:(g + 1) * LANES] = o[g * tq:(g + 1) * tq].astype(BF16)

    if diff:
        lp = lam_ref[...]
        lam = (jnp.exp(jnp.sum(lp[0:1] * lp[1:2], axis=-1, keepdims=True))
               - jnp.exp(jnp.sum(lp[2:3] * lp[3:4], axis=-1, keepdims=True)) + lambda_init)

    def q_block(j, carry):
        r0 = pl.multiple_of(j * tq, tq)
        qs = stacked_q(r0)
        if diff:
            for i in range(n_steps):
                p, psum = probabilities(qs, i)
                p_sc[:, i * tk:(i + 1) * tk] = p
                if i == 0:
                    l_sc[...] = psum
                else:
                    l_sc[...] += psum
            denom = jnp.sum(l_sc[...], axis=-1, keepdims=True)
            ok = jnp.min(denom) >= ATTN_MIN_DENOM

            @pl.when(ok)
            def _():
                l1, l2 = denom[:tq], denom[tq:]
                rho = jnp.broadcast_to(lam * l1 / l2, (tq, LANES)).astype(BF16)
                acc = None
                for i in range(n_steps):
                    mix = [p_sc[0:tq, i * tk + c * LANES:i * tk + (c + 1) * LANES]
                           - rho * p_sc[tq:2 * tq, i * tk + c * LANES:i * tk + (c + 1) * LANES]
                           for c in range(tk // LANES)]
                    pv = _dot(jnp.concatenate(mix, axis=1), v_ref[0, i * tk:(i + 1) * tk, :])
                    acc = pv if acc is None else acc + pv
                finish_diff(acc / l1, r0)
        else:
            for i in range(n_steps):
                p, psum = probabilities(qs, i)
                pv = _dot(p, v_ref[0, i * tk:(i + 1) * tk, :])
                if i == 0:
                    l_sc[...] = psum
                    acc_sc[...] = pv
                else:
                    l_sc[...] += psum
                    acc_sc[...] += pv
            denom = jnp.sum(l_sc[...], axis=-1, keepdims=True)
            finish_gqa(acc_sc[...] / denom, r0)
            ok = jnp.min(denom) >= ATTN_MIN_DENOM

        @pl.when(jnp.logical_not(ok))
        def _():
            m_sc[...] = jnp.full(m_sc.shape, -jnp.inf, F32)
            l1_sc[...] = jnp.zeros(l1_sc.shape, F32)
            acc_sc[...] = jnp.zeros(acc_sc.shape, F32)

            def body(i, c):
                off = pl.multiple_of(i * tk_exact, tk_exact)
                s = _dot_nt(qs, k_ref[0, pl.ds(off, tk_exact), :])
                m_old = m_sc[...]
                m_new = jnp.maximum(m_old, jnp.max(s, axis=-1, keepdims=True))
                alpha = jnp.exp2(m_old - m_new)
                p = jnp.exp2(s - m_new)
                l1_sc[...] = alpha * l1_sc[...] + jnp.sum(p, axis=-1, keepdims=True)
                acc_sc[...] = alpha * acc_sc[...] + _dot(p.astype(BF16), v_ref[0, pl.ds(off, tk_exact), :])
                m_sc[...] = m_new
                return c

            lax.fori_loop(0, kv_rows // tk_exact, body, 0)
            o = acc_sc[...] / l1_sc[...]
            if diff:
                finish_diff(o[:tq] - lam * o[tq:], r0)
            else:
                finish_gqa(o, r0)

        return carry

    lax.fori_loop(0, q_ref.shape[1] // tq, q_block, 0)


def _attention(q, k, v, shift, lamp, subln, *, diff, lambda_init, kv_rows, kv_block):
    b, t, _ = q.shape
    tq = min(TQ_DIFF if diff else TQ_GQA, t)
    assert t % tq == 0 and kv_rows % LANES == 0
    qw = LANES if diff else GQA_GROUP * LANES
    heads = k.shape[2] // LANES
    stack = (2 if diff else GQA_GROUP) * tq
    tk_max = TK_ATTN_DIFF
    steps = [c for c in range(LANES, min(tk_max, kv_rows) + 1, LANES) if kv_rows % c == 0]
    tk = max([c for c in steps if c % MXU_N == 0] or steps)
    kv_spec = pl.BlockSpec((1, kv_rows, LANES), lambda bi, h, *_: (bi, kv_block, h))
    grid = (b, heads, t // tq)
    q_spec = pl.BlockSpec((1, tq, qw), lambda bi, h, i: (bi, i, h))
    ins = [q, k, v, shift, lamp, subln]
    specs = [q_spec, kv_spec, kv_spec]
    specs += [pl.BlockSpec(a.shape, lambda bi, h, *_: (0, 0)) for a in (shift, lamp, subln)]
    tk_exact = max(c for c in steps if c <= TK_EXACT)
    kern = functools.partial(_attn_kernel, diff=diff, tq=tq, tk=tk, tk_exact=tk_exact, lambda_init=lambda_init)
    scratch = [pltpu.VMEM((stack, 1), F32), pltpu.VMEM((stack, 1), F32),
               pltpu.VMEM((stack, LANES), F32), pltpu.VMEM((stack, LANES), F32)]
    if diff:
        scratch.append(pltpu.VMEM((stack, kv_rows), BF16))
    return pl.pallas_call(
        kern,
        grid=grid,
        in_specs=specs,
        out_specs=q_spec,
        out_shape=jax.ShapeDtypeStruct((b, t, heads * qw), BF16),
        scratch_shapes=scratch,
        compiler_params=_params(("parallel",) * len(grid), vmem=VMEM_LIMIT_LARGE),
        name="attn_diff" if diff else "attn_gqa",
    )(*ins)


def _gla_cumsum(g_ref, *, nchunk, rev):
    c = GLA_CHUNK
    rows = nchunk * c
    ri = lax.broadcasted_iota(jnp.int32, (rows, rows), 0)
    ci = lax.broadcasted_iota(jnp.int32, (rows, rows), 1)
    causal = (ci >= ri) if rev else (ci <= ri)
    tri = jnp.where(jnp.logical_and(ri // c == ci // c, causal), 1.0, 0.0).astype(BF16)
    hi, mid, lo = _split3(g_ref[0])
    return _dot(tri, hi) + _dot(tri, mid) + _dot(tri, lo)


def _gla_sub_ref(bs, i, rev):
    nsub = GLA_CHUNK // GLA_SUB
    if rev:
        return bs[(i + 1) * GLA_SUB:(i + 1) * GLA_SUB + 1] if i < nsub - 1 else jnp.zeros_like(bs[0:1])
    return bs[i * GLA_SUB - 1:i * GLA_SUB] if i > 0 else jnp.zeros_like(bs[0:1])


def _gla_max_diag_exponent(bsum, *, nchunk, rev):
    c = GLA_CHUNK
    worst = None
    for n in range(nchunk):
        bs = bsum[n * c:(n + 1) * c]
        for i in range(c // GLA_SUB):
            far = i * GLA_SUB if rev else (i + 1) * GLA_SUB - 1
            e = _gla_sub_ref(bs, i, rev) - bs[far:far + 1]
            worst = e if worst is None else jnp.maximum(worst, e)
    return jnp.max(worst, axis=-1, keepdims=True)


def _gla_local(q_ref, k_ref, v_ref, bsum, *, nchunk, dkh, dvh, rev, exact_diag):
    c = GLA_CHUNK
    qf, kf = q_ref[0].astype(F32), k_ref[0].astype(F32)
    qt = (qf * jnp.exp(bsum)).astype(BF16)

    krow = lax.broadcasted_iota(jnp.int32, (c, bsum.shape[1]), 0)
    r64 = lax.broadcasted_iota(jnp.int32, (c, c), 0)
    c64 = lax.broadcasted_iota(jnp.int32, (c, c), 1)
    keep = (c64 >= r64) if rev else (c64 <= r64)
    if exact_diag:
        keep = (c64 // GLA_SUB > r64 // GLA_SUB) if rev else (c64 // GLA_SUB < r64 // GLA_SUB)
    rs = lax.broadcasted_iota(jnp.int32, (GLA_SUB, c), 0)
    cs = lax.broadcasted_iota(jnp.int32, (GLA_SUB, c), 1)
    nsub = c // GLA_SUB
    hks = [slice(h * dkh, (h + 1) * dkh) for h in range(GLA_HEADS)]
    scaled, diag = [], {}
    for n in range(nchunk):
        bs, kc, qc = bsum[n * c:(n + 1) * c], kf[n * c:(n + 1) * c], qf[n * c:(n + 1) * c]
        b_tot = bs[0:1] if rev else bs[c - 1:c]
        kd = (kc * jnp.exp(b_tot - bs)).astype(BF16)
        qis, kis = [], []
        for i in range(nsub):
            lo_r, hi_r = i * GLA_SUB, (i + 1) * GLA_SUB
            ref = _gla_sub_ref(bs, i, rev)
            if exact_diag:
                valid = (krow >= hi_r) if rev else (krow < lo_r)
            else:
                valid = (krow >= lo_r) if rev else (krow < hi_r)
            kis.append((kc * jnp.exp(jnp.where(valid, ref - bs, 0.0))).astype(BF16))
            qis.append((qc[lo_r:hi_r] * jnp.exp(bs[lo_r:hi_r] - ref)).astype(BF16))
            if exact_diag:
                cols = [jnp.zeros((GLA_SUB, c), F32) for _ in range(GLA_HEADS)]
                for s in range(lo_r, hi_r):
                    e = jnp.exp(jnp.minimum(bs[lo_r:hi_r] - bs[s:s + 1], 0.0))
                    prod = qc[lo_r:hi_r] * e * kc[s:s + 1]
                    pair = (rs + lo_r <= s) if rev else (rs + lo_r >= s)
                    for h in range(GLA_HEADS):
                        col = jnp.sum(prod[:, hks[h]], axis=-1, keepdims=True)
                        cols[h] = cols[h] + jnp.where(jnp.logical_and(cs == s, pair), col, 0.0)
                for h in range(GLA_HEADS):
                    diag[n, h, i] = cols[h]
        scaled.append((jnp.exp(b_tot), kd, qis, kis))

    pairs = [(n, h) for n in range(nchunk) for h in range(GLA_HEADS)]
    vals = {(n, h): v_ref[0, n * c:(n + 1) * c, h * dvh:(h + 1) * dvh] for n, h in pairs}
    blocks = {(n, h): [_dot_nt(scaled[n][2][i][:, hks[h]], scaled[n][3][i][:, hks[h]]) for i in range(nsub)]
              for n, h in pairs}
    inc = {(n, h): _dot_tn(vals[n, h], scaled[n][1][:, hks[h]]) for n, h in pairs}
    amat = {}
    for n, h in pairs:
        a = jnp.where(keep, jnp.concatenate(blocks[n, h], axis=0), 0.0)
        if exact_diag:
            a = a + jnp.concatenate([diag[n, h, i] for i in range(nsub)], axis=0)
        amat[n, h] = a.astype(BF16)
    intra = {p: _dot(amat[p], vals[p]) for p in pairs}
    local = {(n, h): (scaled[n][0][:, hks[h]], intra[n, h], inc[n, h]) for n, h in pairs}
    return qt, local


def _gla_kernel(qf_ref, kf_ref, vf_ref, gf_ref, qb_ref, kb_ref, vb_ref, gb_ref, s0_ref,
                of_ref, ob_ref, st_out_ref, st_sc, *, nchunk, dkh, dvh):
    j = pl.program_id(1)

    @pl.when(j == 0)
    def _():
        st_sc[...] = s0_ref[0]

    c = GLA_CHUNK
    bf = _gla_cumsum(gf_ref, nchunk=nchunk, rev=False)
    bb = _gla_cumsum(gb_ref, nchunk=nchunk, rev=True)
    worst = jnp.maximum(_gla_max_diag_exponent(bf, nchunk=nchunk, rev=False),
                        _gla_max_diag_exponent(bb, nchunk=nchunk, rev=True))
    factored_ok = jnp.max(worst) <= GLA_MAX_EXPONENT

    def sweep(exact_diag):
        kw = dict(nchunk=nchunk, dkh=dkh, dvh=dvh, exact_diag=exact_diag)
        sides = ((_gla_local(qf_ref, kf_ref, vf_ref, bf, rev=False, **kw), of_ref),
                 (_gla_local(qb_ref, kb_ref, vb_ref, bb, rev=True, **kw), ob_ref))
        for step in range(nchunk):
            for side, ((qt, local), o_ref) in enumerate(sides):
                n = step if side == 0 else nchunk - 1 - step
                for h in range(GLA_HEADS):
                    hk, hv = slice(h * dkh, (h + 1) * dkh), slice(h * dvh, (h + 1) * dvh)
                    dec, intra, inc = local[n, h]
                    st = st_sc[side, h]
                    o = _dot_nt(qt[n * c:(n + 1) * c, hk], st.astype(BF16)) + intra
                    o_ref[0, n * c:(n + 1) * c, hv] = o.astype(BF16)
                    st_sc[side, h] = dec * st + inc

    @pl.when(factored_ok)
    def _():
        sweep(False)

    @pl.when(jnp.logical_not(factored_ok))
    def _():
        sweep(True)

    @pl.when(j == pl.num_programs(1) - 1)
    def _():
        st_out_ref[0] = st_sc[...]


def _gla_scan(q, k, v, g, s0):
    b, t, dk = q.shape
    dv = v.shape[2]
    dkh, dvh = dk // GLA_HEADS, dv // GLA_HEADS
    nchunk = min(GLA_R, t // GLA_CHUNK)
    rows = nchunk * GLA_CHUNK
    assert t % rows == 0
    n = t // rows
    fwd = lambda w, cb=0: pl.BlockSpec((1, rows, w), lambda bi, j: (bi, j, cb))
    bwd = lambda w, cb=0: pl.BlockSpec((1, rows, w), lambda bi, j: (bi, n - 1 - j, cb))
    st_spec = pl.BlockSpec((1, 2, GLA_HEADS, dvh, dkh), lambda bi, j: (bi, 0, 0, 0, 0))
    kern = functools.partial(_gla_kernel, nchunk=nchunk, dkh=dkh, dvh=dvh)
    return pl.pallas_call(
        kern,
        grid=(b, n),
        in_specs=[fwd(dk), fwd(dk), fwd(dv), fwd(dk, 0), bwd(dk), bwd(dk), bwd(dv), bwd(dk, 1), st_spec],
        out_specs=[fwd(dv), bwd(dv), st_spec],
        out_shape=[jax.ShapeDtypeStruct((b, t, dv), BF16), jax.ShapeDtypeStruct((b, t, dv), BF16),
                   jax.ShapeDtypeStruct(s0.shape, F32)],
        scratch_shapes=[pltpu.VMEM((2, GLA_HEADS, dvh, dkh), F32)],
        compiler_params=_params(("parallel", "arbitrary")),
        name="gla_scan",
    )(q, k, v, g, q, k, v, g, s0)


def _post_kernel(*refs, d, f, gla, dvh):
    if gla:
        x_ref, of_ref, ob_ref, r_ref, on_ref, mod_ref, wo_ref, g2_ref, wgu_ref, wd_ref, o_ref = refs
        r = r_ref[...].astype(F32)
        gate = r * _sigmoid(r)
        parts = []
        for h in range(GLA_HEADS):
            cs = slice(h * dvh, (h + 1) * dvh)
            oh = of_ref[:, cs].astype(F32) + ob_ref[:, cs].astype(F32)
            ms = jnp.mean(oh * oh, axis=-1, keepdims=True)
            parts.append((oh * lax.rsqrt(ms + NORM_EPS) * on_ref[...] * gate[:, cs]).astype(BF16))
        a = jnp.concatenate(parts, axis=1)
    else:
        x_ref, a_ref, mod_ref, wo_ref, g2_ref, wgu_ref, wd_ref, o_ref = refs
        a = a_ref[...]
    m = mod_ref[0]
    gate1, shift2, scale2, gate2 = (m[:, 2 * d:3 * d], m[:, 3 * d:4 * d], m[:, 4 * d:5 * d], m[:, 5 * d:6 * d])
    x1 = x_ref[...] + gate1 * _dot(a, wo_ref[...])
    h2 = _norm_mod(x1, g2_ref[...], shift2, scale2).astype(BF16)
    acts = []
    for c0 in range(0, f, FFN_CHUNK):
        gt = _dot(h2, wgu_ref[:, c0:c0 + FFN_CHUNK])
        up = _dot(h2, wgu_ref[:, f + c0:f + c0 + FFN_CHUNK])
        acts.append((gt * _sigmoid(gt) * up).astype(BF16))
    o_ref[...] = x1 + gate2 * _dot(jnp.concatenate(acts, axis=1), wd_ref[...])


def _post(x2d, mix, modx, rows_per_mod, wo, g2, wgu, wd, *, gla_norm=None):
    n, d = x2d.shape
    f = wd.shape[0]
    tm = min(TM_POST, n)
    assert n % tm == 0 and rows_per_mod % tm == 0 and f % FFN_CHUNK == 0
    mod_blocks = rows_per_mod // tm
    row = lambda c: pl.BlockSpec((tm, c), lambda i: (i, 0))
    gla = gla_norm is not None
    if gla:
        ins = [x2d, *mix, gla_norm]
        specs = [row(d), row(d), row(d), row(d), _const_spec(gla_norm.shape)]
        dvh = gla_norm.shape[1]
    else:
        ins = [x2d, mix]
        specs = [row(d), row(d)]
        dvh = 0
    ins += [modx, wo, g2, wgu, wd]
    specs += [pl.BlockSpec((1, 1, 6 * d), lambda i: (i // mod_blocks, 0, 0)),
              _const_spec(wo.shape), _const_spec((1, d)), _const_spec(wgu.shape), _const_spec(wd.shape)]
    kern = functools.partial(_post_kernel, d=d, f=f, gla=gla, dvh=dvh)
    return pl.pallas_call(
        kern,
        grid=(n // tm,),
        in_specs=specs,
        out_specs=row(d),
        out_shape=jax.ShapeDtypeStruct((n, d), F32),
        compiler_params=_params(("parallel",), vmem=VMEM_LIMIT_LARGE),
        name="post_gla" if gla else "post_attn",
    )(*ins)


def _rope_tables(n_tokens, head_dim):
    rows = n_tokens // GRID_W
    rowp = jnp.broadcast_to(jnp.arange(rows)[:, None], (rows, GRID_W)).reshape(-1)
    colp = jnp.broadcast_to(jnp.arange(GRID_W)[None, :], (rows, GRID_W)).reshape(-1)
    half = head_dim // 2
    quarter = head_dim // 4
    inv = ROPE_BASE ** (-jnp.arange(0, half, 2, dtype=F32) / half)

    def axis_angles(pos):
        a = pos.astype(F32)[:, None] * inv[None, :]
        return jnp.concatenate([a, a], axis=-1)

    ang = jnp.concatenate([axis_angles(rowp), axis_angles(colp)], axis=-1)
    reps = LANES // head_dim
    cos = jnp.tile(jnp.cos(ang), (1, reps))
    sin = jnp.tile(jnp.sin(ang), (1, reps))
    first = (jnp.arange(LANES) % (2 * quarter)) < quarter
    sa = jnp.where(first[None, :], -sin, 0.0)
    sb = jnp.where(first[None, :], 0.0, sin)
    return cos, sa, sb


def _seg_matrix(seg):
    idx = jnp.arange(MXU_N) // seg
    return jnp.where(idx[:, None] == idx[None, :], 1.0 / seg, 0.0).astype(BF16)


def kernel(x, c, ctx, c_ctx, ada_w, ada_b, norm1_g, norm2_g, ffn_w_gu, ffn_w_down, diff_w_in, diff_w_out, diff_q_norm, diff_k_norm, diff_lambda_q1, diff_lambda_k1, diff_lambda_q2, diff_lambda_k2, diff_subln, gqa_w_in, gqa_w_out, gqa_q_norm, gqa_k_norm, gla_w_in, gla_gate_w1_fwd, gla_gate_w2_fwd, gla_gate_b_fwd, gla_gate_w1_bwd, gla_gate_w2_bwd, gla_gate_b_bwd, gla_out_norm, gla_w_out):
    b, l, d = x.shape
    lc = ctx.shape[1]
    depth = ada_w.shape[0]
    nl, nc = b * l, b * lc

    assert l % lc == 0 and l % GRID_W == 0 and lc % GLA_CHUNK == 0
    cond_rows = pl.cdiv(b + 1, SUBLANES) * SUBLANES
    cond = jnp.zeros((cond_rows, d), F32).at[:b].set(c).at[b].set(c_ctx)
    mod = _modulation(cond, ada_w, ada_b)

    xl = x.reshape(nl, d)
    xc = ctx.reshape(nc, d)
    ones_tab = (jnp.ones((lc, LANES), F32), jnp.zeros((lc, LANES), F32), jnp.zeros((lc, LANES), F32))
    dummy_lam = jnp.zeros((4, DIFF_DH), F32)
    dummy_sub = jnp.ones((1, LANES), F32)

    for i in range(depth):
        last = i == depth - 1
        kind, j = i % N_MIXERS, i // N_MIXERS
        modl = mod[i, :b].reshape(b, 1, 6 * d)
        modc = mod[i, b:b + 1].reshape(1, 1, 6 * d)
        g1 = norm1_g[i].reshape(1, d)
        g2 = norm2_g[i].reshape(1, d)
        wgu = ffn_w_gu[i].astype(BF16)
        wd = ffn_w_down[i].astype(BF16)

        if kind in (0, 1):
            if kind == 0:
                diff, dh = True, DIFF_DH
                w_in, w_out = diff_w_in[j].astype(BF16), diff_w_out[j].astype(BF16)
                gq = jnp.tile(diff_q_norm[j], LANES // dh).reshape(1, LANES) * (dh ** -0.5 * LOG2E)
                gk = jnp.tile(diff_k_norm[j], LANES // dh).reshape(1, LANES)
                nq = nk = nv = d // LANES
                lamp = jnp.stack([diff_lambda_q1[j], diff_lambda_k1[j], diff_lambda_q2[j], diff_lambda_k2[j]])
                subln = diff_subln[j].reshape(1, LANES)
                lambda_init = 0.8 - 0.6 * math.exp(-0.3 * i)
            else:
                diff, dh = False, GQA_DH
                w_in, w_out = gqa_w_in[j].astype(BF16), gqa_w_out[j].astype(BF16)
                gq = gqa_q_norm[j].reshape(1, LANES) * (dh ** -0.5 * LOG2E)
                gk = gqa_k_norm[j].reshape(1, LANES)
                nq = d // LANES
                nk = nv = (w_in.shape[1] - d) // (2 * LANES)
                lamp, subln, lambda_init = dummy_lam, dummy_sub, 0.0
            pmat = _seg_matrix(dh)
            cos, sa, sb = _rope_tables(l, dh)
            kw = dict(nq=nq, nk=nk, nv=nv, rot=dh // 4)
            ql, k_all, v_all = _pre_attn(xl, modl, l, g1, w_in, pmat, gq, gk, cos, sa, sb, l, l + lc, 0, None, **kw)
            qc, k_all, v_all = _pre_attn(xc, modc, nc, g1, w_in, pmat, gq, gk, *ones_tab, lc, l + lc, l,
                                         (k_all, v_all), **kw)
            r3 = lambda t, n: t.reshape(b, n, t.shape[1])
            shift = jnp.full((1, LANES), dh, F32) * jnp.max(jnp.abs(gq)) * jnp.max(jnp.abs(gk))
            akw = dict(diff=diff, lambda_init=lambda_init)
            ol = _attention(r3(ql, l), k_all, v_all, shift, lamp, subln, kv_rows=l + lc, kv_block=0, **akw)
            xl = _post(xl, ol.reshape(nl, d), modl, l, w_out, g2, wgu, wd)
            if not last:
                oc = _attention(r3(qc, lc), k_all, v_all, shift, lamp, subln, kv_rows=lc, kv_block=l // lc, **akw)
                xc = _post(xc, oc.reshape(nc, d), modc, nc, w_out, g2, wgu, wd)
        else:
            dk, dv = gla_gate_w2_fwd.shape[2], gla_w_out.shape[1]
            rank = gla_gate_w1_fwd.shape[2]
            w1 = jnp.zeros((d, MXU_N), F32).at[:, :rank].set(gla_gate_w1_fwd[j]).at[:, rank:2 * rank].set(gla_gate_w1_bwd[j])
            w_in = jnp.concatenate([gla_w_in[j], w1], axis=1).astype(BF16)
            w2 = (jnp.zeros((MXU_N, 2 * dk), F32).at[:rank, :dk].set(gla_gate_w2_fwd[j])
                  .at[rank:2 * rank, dk:].set(gla_gate_w2_bwd[j])).astype(BF16)
            gb = jnp.concatenate([gla_gate_b_fwd[j], gla_gate_b_bwd[j]]).reshape(1, 2 * dk)
            w_out = gla_w_out[j].astype(BF16)
            dkh, dvh = dk // GLA_HEADS, dv // GLA_HEADS
            onorm = gla_out_norm[j].reshape(1, dvh)
            kw = dict(dk=dk, dv=dv, qscale=dkh ** -0.5)
            ql, kl, vl, rl, gl = _pre_gla(xl, modl, l, g1, w_in, w2, gb, **kw)
            qc, kc, vc, rc, gc = _pre_gla(xc, modc, nc, g1, w_in, w2, gb, **kw)
            r3 = lambda t, n: t.reshape(b, n, t.shape[1])
            s0 = jnp.zeros((b, 2, GLA_HEADS, dvh, dkh), F32)
            ocf, ocb, sc = _gla_scan(r3(qc, lc), r3(kc, lc), r3(vc, lc), r3(gc, lc), s0)
            olf, olb, _ = _gla_scan(r3(ql, l), r3(kl, l), r3(vl, l), r3(gl, l), sc)
            xl = _post(xl, (olf.reshape(nl, dv), olb.reshape(nl, dv), rl), modl, l, w_out, g2, wgu, wd, gla_norm=onorm)
            if not last:
                xc = _post(xc, (ocf.reshape(nc, dv), ocb.reshape(nc, dv), rc), modc, nc, w_out, g2, wgu, wd, gla_norm=onorm)
    return xl.reshape(b, l, d)
```
